```python
import jax
import jax.numpy as jnp
from jax import lax
import numpy as np

D_MODEL = 1024
BATCH = 1
SEQ = 16384
DEPTH = 4

GRID_W = 64
N_MEM = 256
N_EVEN = (DEPTH + 1) // 2
N_ODD = DEPTH // 2
MIX_W = D_MODEL // 2
A_HEADS = 4
A_DK = MIX_W // A_HEADS
A_DV = A_DK
CHUNK = 64
POOL_WINDOWS = (2, 4, 8, 16)
POOL_GROUPS = len(POOL_WINDOWS)
POOL_GW = MIX_W // POOL_GROUPS
C_HEADS = 4
C_KV_HEADS = 2
C_HD = MIX_W // C_HEADS
KV_W = C_KV_HEADS * C_HD
ROPE_THETA = 10000.0
Q_BLOCK = 128
CONV_W = 31
GLU_W = 2 * MIX_W
XA_HEADS = 4
XA_HD = D_MODEL // XA_HEADS
D_FF = -(-8 * D_MODEL // (3 * 256)) * 256
AB_IN = 6 * MIX_W
CD_IN = MIX_W + 2 * KV_W + GLU_W
ALPHA = (2 * DEPTH) ** 0.25
BETA = (8 * DEPTH) ** -0.25
EPS = 1e-6
F32 = jnp.float32

kernel_name = 'hybrid_hgrn2_pool_gqa_conformer_encoder'


def layer_norm(x, g, b):
    xf = x.astype(F32)
    mu = jnp.mean(xf, axis=-1, keepdims=True)
    var = jnp.mean(jnp.square(xf - mu), axis=-1, keepdims=True)
    return ((xf - mu) * lax.rsqrt(var + EPS) * g + b).astype(x.dtype)


def rms_norm(x, g):
    xf = x.astype(F32)
    return (xf * lax.rsqrt(jnp.mean(xf * xf, axis=-1, keepdims=True) + EPS) * g).astype(x.dtype)


def hgrn2_scan(q, k, v, logf):
    Bn, S, H, K = q.shape
    V = v.shape[-1]
    nc = S // CHUNK

    def chunks(a):
        return a.reshape(Bn, nc, CHUNK, H, a.shape[-1]).transpose(1, 0, 3, 2, 4)

    lower = jnp.tril(jnp.ones((CHUNK, CHUNK), bool))[:, :, None]

    def step(state, inp):
        qc, kc, vc, gc = inp
        b = jnp.cumsum(gc, axis=2)
        o_inter = jnp.einsum('bhck,bhkv->bhcv', qc * jnp.exp(b), state)
        rel = jnp.where(lower, b[:, :, :, None, :] - b[:, :, None, :, :], -jnp.inf)
        scores = jnp.einsum('bhtk,bhsk,bhtsk->bhts', qc, kc, jnp.exp(rel))
        o_intra = jnp.einsum('bhts,bhsv->bhtv', scores, vc)
        b_last = b[:, :, -1, :]
        k_dec = kc * jnp.exp(b_last[:, :, None, :] - b)
        state = jnp.exp(b_last)[..., None] * state + jnp.einsum('bhck,bhcv->bhkv', k_dec, vc)
        return state, o_inter + o_intra

    state0 = jnp.zeros((Bn, H, K, V), F32)
    _, out = lax.scan(step, state0, (chunks(q), chunks(k), chunks(v), chunks(logf)))
    return out.transpose(1, 0, 3, 2, 4).reshape(Bn, S, H, V)


def hgrn2_gates(z, lb):
    lb = lb.reshape(A_HEADS, A_DK)
    logf = jnp.logaddexp(jnp.log(lb), jnp.log1p(-lb) + jax.nn.log_sigmoid(z))
    k = (1.0 - lb) * jax.nn.sigmoid(-z)
    return k, logf


def multiscale_pool(u, pool_w, pool_scale):
    Bn, S, _ = u.shape
    uf = u.astype(F32)
    P = jnp.concatenate([jnp.zeros((Bn, 1, MIX_W), F32), jnp.cumsum(uf, axis=1)], axis=1)
    t = jnp.arange(S)
    outs = []
    for gi, w in enumerate(POOL_WINDOWS):
        lo = jnp.clip(t - w // 2, 0, S - 1)
        hi = jnp.clip(t - w // 2 + w - 1, 0, S - 1)
        sl = slice(gi * POOL_GW, (gi + 1) * POOL_GW)
        Pg = P[:, :, sl]
        cnt = (hi - lo + 1).astype(F32)[None, :, None]
        outs.append((Pg[:, hi + 1] - Pg[:, lo]) / cnt - uf[:, :, sl])
    d = jnp.stack(outs, axis=2)
    y = jnp.einsum('bsgc,gcd->bsgd', d, pool_w.astype(F32)).reshape(Bn, S, MIX_W)
    return (y * pool_scale).astype(u.dtype)


def even_mixer(x, w_in, lb_fwd, lb_bwd, norm_g, pool_w, pool_scale, w_out):
    Bn, S, _ = x.shape
    h = x @ w_in
    q, i, zf, zb, og, u = jnp.split(h, 6, axis=-1)

    def heads(a):
        return a.reshape(Bn, S, A_HEADS, A_DK).astype(F32)

    qh = heads(jax.nn.silu(q))
    vh = heads(i)
    kf, gf = hgrn2_gates(heads(zf), lb_fwd)
    kb, gb = hgrn2_gates(heads(zb), lb_bwd)
    flip = lambda a: jnp.flip(a, axis=1)
    o = hgrn2_scan(qh, kf, vh, gf) + flip(hgrn2_scan(flip(qh), flip(kb), flip(vh), flip(gb)))
    y_a = rms_norm(o, norm_g.reshape(A_HEADS, A_DV)).reshape(Bn, S, MIX_W).astype(x.dtype) * jax.nn.silu(og)
    y_b = multiscale_pool(u, pool_w, pool_scale)
    return jnp.concatenate([y_a, y_b], axis=-1) @ w_out


def axial_rope_tables(S):
    rows = S // GRID_W
    row = jnp.repeat(jnp.arange(rows), GRID_W)
    col = jnp.tile(jnp.arange(GRID_W), rows)
    half = C_HD // 2
    freqs = ROPE_THETA ** (-jnp.arange(0, half, 2, dtype=F32) / half)

    def ang(p):
        a = p.astype(F32)[:, None] * freqs[None, :]
        return jnp.concatenate([a, a], axis=-1)

    angles = jnp.concatenate([ang(row), ang(col)], axis=-1)
    return jnp.cos(angles), jnp.sin(angles)


def apply_rope(x, cos, sin):
    xs = x.reshape(x.shape[:-1] + (2, 2, C_HD // 4))
    rot = jnp.stack([-xs[..., 1, :], xs[..., 0, :]], axis=-2).reshape(x.shape)
    return (x * cos[None, :, None, :] + rot * sin[None, :, None, :]).astype(x.dtype)


def block_attention(q, k, v):
    Bn, S = q.shape[:2]
    G = C_HEADS // C_KV_HEADS
    nb = S // Q_BLOCK
    qb = q.reshape(Bn, nb, Q_BLOCK, C_KV_HEADS, G, C_HD).transpose(1, 0, 2, 3, 4, 5)
    scale = C_HD ** -0.5

    def one_block(qi):
        s = jnp.einsum('bqhgd,bkhd->bhgqk', qi, k).astype(F32) * scale
        p = jax.nn.softmax(s, axis=-1).astype(v.dtype)
        return jnp.einsum('bhgqk,bkhd->bqhgd', p, v)

    out = lax.map(one_block, qb)
    return out.transpose(1, 0, 2, 3, 4, 5).reshape(Bn, S, C_HEADS * C_HD)


def conformer_conv(a, conv_w, conv_b, ln_g, ln_b):
    val, gate = jnp.split(a, 2, axis=-1)
    u = val * jax.nn.sigmoid(gate)
    u = lax.conv_general_dilated(
        u, conv_w[:, None, :], window_strides=(1,),
        padding=[(CONV_W // 2, CONV_W // 2)],
        dimension_numbers=('NWC', 'WIO', 'NWC'),
        feature_group_count=MIX_W) + conv_b
    return jax.nn.silu(layer_norm(u, ln_g, ln_b))


def odd_mixer(x, w_in, q_g, k_g, conv_w, conv_b, ln_g, ln_b, w_out):
    Bn, S, _ = x.shape
    h = x @ w_in
    q, k, v, a = jnp.split(h, [MIX_W, MIX_W + KV_W, MIX_W + 2 * KV_W], axis=-1)
    q = rms_norm(q.reshape(Bn, S, C_HEADS, C_HD), q_g)
    k = rms_norm(k.reshape(Bn, S, C_KV_HEADS, C_HD), k_g)
    v = v.reshape(Bn, S, C_KV_HEADS, C_HD)
    cos, sin = axial_rope_tables(S)
    y_c = block_attention(apply_rope(q, cos, sin), apply_rope(k, cos, sin), v)
    y_d = conformer_conv(a, conv_w, conv_b, ln_g, ln_b)
    return jnp.concatenate([y_c, y_d], axis=-1) @ w_out


def memory_cross_attention(x, mem, wq, wkv, wo):
    Bn, S, _ = x.shape
    M = mem.shape[1]
    q = (x @ wq).reshape(Bn, S, XA_HEADS, XA_HD)
    k, v = jnp.split(mem @ wkv, 2, axis=-1)
    k = k.reshape(Bn, M, XA_HEADS, XA_HD)
    v = v.reshape(Bn, M, XA_HEADS, XA_HD)
    s = jnp.einsum('bshd,bmhd->bhsm', q, k).astype(F32) * (XA_HD ** -0.5)
    p = jax.nn.softmax(s, axis=-1).astype(x.dtype)
    o = jnp.einsum('bhsm,bmhd->bshd', p, v).reshape(Bn, S, D_MODEL)
    return o @ wo


def swiglu_ffn(x, w_gu, w_down):
    gt, up = jnp.split(x @ w_gu, 2, axis=-1)
    return (jax.nn.silu(gt) * up) @ w_down


def setup_inputs(seed: int = 0) -> dict:
    key = jax.random.key(seed)
    ks = jax.random.split(key, 24)

    def nrm(k, shape, scale):
        return jax.random.normal(k, shape, F32) * scale

    def gain(k, shape):
        return 1.0 + nrm(k, shape, 0.02)

    x = nrm(ks[0], (BATCH, SEQ, D_MODEL), 1.0)
    mem = nrm(ks[1], (BATCH, N_MEM, D_MODEL), 1.0)
    w_in_ab = nrm(ks[2], (N_EVEN, D_MODEL, AB_IN), D_MODEL ** -0.5)
    hgrn_lb_logits = nrm(ks[3], (2, DEPTH, MIX_W), 0.5)
    hgrn_norm_g = gain(ks[4], (N_EVEN, MIX_W))
    pool_w = nrm(ks[5], (N_EVEN, POOL_GROUPS, POOL_GW, POOL_GW), POOL_GW ** -0.5)
    pool_scale = gain(ks[6], (N_EVEN, MIX_W))
    w_out_ab = nrm(ks[7], (N_EVEN, 2 * MIX_W, D_MODEL), BETA * (2 * MIX_W) ** -0.5)
    w_in_cd = nrm(ks[8], (N_ODD, D_MODEL, CD_IN), D_MODEL ** -0.5)
    q_norm_g = gain(ks[9], (N_ODD, C_HD))
    k_norm_g = gain(ks[10], (N_ODD, C_HD))
    conv_w = nrm(ks[11], (N_ODD, CONV_W, MIX_W), CONV_W ** -0.5)
    conv_b = nrm(ks[12], (N_ODD, MIX_W), 0.02)
    conv_ln_g = gain(ks[13], (N_ODD, MIX_W))
    conv_ln_b = nrm(ks[14], (N_ODD, MIX_W), 0.02)
    w_out_cd = nrm(ks[15], (N_ODD, 2 * MIX_W, D_MODEL), BETA * (2 * MIX_W) ** -0.5)
    xa_wq = nrm(ks[16], (DEPTH, D_MODEL, D_MODEL), D_MODEL ** -0.5)
    xa_wkv = jnp.concatenate([
        nrm(ks[17], (DEPTH, D_MODEL, D_MODEL), D_MODEL ** -0.5),
        nrm(ks[18], (DEPTH, D_MODEL, D_MODEL), BETA * D_MODEL ** -0.5)], axis=-1)
    xa_wo = nrm(ks[19], (DEPTH, D_MODEL, D_MODEL), BETA * D_MODEL ** -0.5)
    ffn_w_gu = nrm(ks[20], (DEPTH, D_MODEL, 2 * D_FF), BETA * D_MODEL ** -0.5)
    ffn_w_down = nrm(ks[21], (DEPTH, D_FF, D_MODEL), BETA * D_FF ** -0.5)
    ln_g = gain(ks[22], (DEPTH, 3, D_MODEL))
    ln_b = nrm(ks[23], (DEPTH, 3, D_MODEL), 0.02)
    return {'x': x, 'mem': mem, 'w_in_ab': w_in_ab, 'hgrn_lb_logits': hgrn_lb_logits,
            'hgrn_norm_g': hgrn_norm_g, 'pool_w': pool_w, 'pool_scale': pool_scale,
            'w_out_ab': w_out_ab, 'w_in_cd': w_in_cd, 'q_norm_g': q_norm_g,
            'k_norm_g': k_norm_g, 'conv_w': conv_w, 'conv_b': conv_b,
            'conv_ln_g': conv_ln_g, 'conv_ln_b': conv_ln_b, 'w_out_cd': w_out_cd,
            'xa_wq': xa_wq, 'xa_wkv': xa_wkv, 'xa_wo': xa_wo, 'ffn_w_gu': ffn_w_gu,
            'ffn_w_down': ffn_w_down, 'ln_g': ln_g, 'ln_b': ln_b}


def reference(x, mem, w_in_ab, hgrn_lb_logits, hgrn_norm_g, pool_w, pool_scale, w_out_ab,
              w_in_cd, q_norm_g, k_norm_g, conv_w, conv_b, conv_ln_g, conv_ln_b, w_out_cd,
              xa_wq, xa_wkv, xa_wo, ffn_w_gu, ffn_w_down, ln_g, ln_b):
    cum = jnp.cumsum(jax.nn.softmax(hgrn_lb_logits.astype(F32), axis=1), axis=1)
    lb = jnp.maximum(cum - cum[:, :1], 0.0)
    for l in range(DEPTH):
        j = l // 2
        if l % 2 == 0:
            y = even_mixer(x, w_in_ab[j], lb[0, l], lb[1, l], hgrn_norm_g[j],
                           pool_w[j], pool_scale[j], w_out_ab[j])
        else:
            y = odd_mixer(x, w_in_cd[j], q_norm_g[j], k_norm_g[j], conv_w[j], conv_b[j],
                          conv_ln_g[j], conv_ln_b[j], w_out_cd[j])
        x = layer_norm(ALPHA * x + y, ln_g[l, 0], ln_b[l, 0])
        x = layer_norm(ALPHA * x + memory_cross_attention(x, mem, xa_wq[l], xa_wkv[l], xa_wo[l]),
                       ln_g[l, 1], ln_b[l, 1])
        x = layer_norm(ALPHA * x + swiglu_ffn(x, ffn_w_gu[l], ffn_w_down[l]),
                       ln_g[l, 2], ln_b[l, 2])
    return x
```

```python
import functools
import math

import jax
import jax.numpy as jnp
from jax import lax
from jax.experimental import pallas as pl
from jax.experimental.pallas import tpu as pltpu

F32 = jnp.float32
BF16 = jnp.bfloat16

A_HEADS = 4
POOL_WINDOWS = (2, 4, 8, 16)
C_HEADS = 4
C_KV_HEADS = 2
GRID_W = 64
ROPE_THETA = 10000.0
CONV_W = 31
XA_HEADS = 4
EPS = 1e-6

LANES = 128
SUBLANES = 8
VMEM_LIMIT = 56 * 1024 * 1024

HALO = 16
HG_BLOCK = 128
NEG_BIG = -1e30


def _cparams(sem):
    return pltpu.CompilerParams(dimension_semantics=sem, vmem_limit_bytes=VMEM_LIMIT)


def _silu(x):
    return x * jax.nn.sigmoid(x)


def _layer_norm(y, g, b):
    mu = jnp.mean(y, axis=-1, keepdims=True)
    d = y - mu
    var = jnp.mean(d * d, axis=-1, keepdims=True)
    return d * lax.rsqrt(var + EPS) * g + b


def _dot(a, b):
    return jnp.dot(a, b, preferred_element_type=F32)


def _dot_nt(a, b):
    return lax.dot_general(a, b, (((1,), (1,)), ((), ())), preferred_element_type=F32)


def _proj_kernel(x_ref, w_ref, o_ref):
    o_ref[...] = _dot(x_ref[...].astype(BF16), w_ref[...])


def _proj(x, w_bf, tm, name):
    m, k = x.shape
    n = w_bf.shape[1]
    return pl.pallas_call(
        _proj_kernel,
        out_shape=jax.ShapeDtypeStruct((m, n), F32),
        grid=(m // tm,),
        in_specs=[pl.BlockSpec((tm, k), lambda i: (i, 0)),
                  pl.BlockSpec((k, n), lambda i: (0, 0))],
        out_specs=pl.BlockSpec((tm, n), lambda i: (i, 0)),
        compiler_params=_cparams(("parallel",)),
        name=name,
    )(x, w_bf)


def _hgrn_direction(q_ref, v_ref, z_ref, lb, st_ref, o_ref, reverse):
    n = HG_BLOCK
    w = q_ref.shape[1]
    hd = w // A_HEADS
    z = z_ref[...]
    qs = _silu(q_ref[...])
    v = v_ref[...]
    k = (1.0 - lb) * jax.nn.sigmoid(-z)
    g = jnp.logaddexp(jnp.log(lb), jnp.log1p(-lb) + jax.nn.log_sigmoid(z))

    row = lax.broadcasted_iota(jnp.int32, (n, n), 0)
    col = lax.broadcasted_iota(jnp.int32, (n, n), 1)
    rowv = lax.broadcasted_iota(jnp.int32, (n, 1), 0)
    tri = (col >= row) if reverse else (col <= row)
    tri = jnp.where(tri, 1.0, 0.0).astype(BF16)
    g1 = g.astype(BF16)
    r1 = g - g1.astype(F32)
    g2 = r1.astype(BF16)
    g3 = (r1 - g2.astype(F32)).astype(BF16)
    b = _dot(tri, g1) + _dot(tri, g2) + _dot(tri, g3)

    end = 0 if reverse else n - 1
    b_end = b[end:end + 1, :]
    qe = (qs * jnp.exp(b)).astype(BF16)
    kd = (k * jnp.exp(b_end - b)).astype(BF16)
    dec = jnp.exp(b_end)

    lvl = []
    half = n // 2
    while half >= SUBLANES:
        two = 2 * half
        parts = []
        for blk in range(n // two):
            r = blk * two + (half if reverse else half - 1)
            parts.append(jnp.broadcast_to(b[r:r + 1, :], (two, w)))
        bref = jnp.concatenate(parts, axis=0) if len(parts) > 1 else parts[0]
        second = (rowv % two) >= half
        q_side = jnp.logical_not(second) if reverse else second
        d = b - bref
        e = jnp.exp(jnp.where(q_side, d, -d))
        ql = (qs * jnp.where(q_side, e, 0.0)).astype(BF16)
        kl = (k * jnp.where(q_side, 0.0, e)).astype(BF16)
        same = (row // two) == (col // two)
        lvl.append((ql, kl, same))
        half //= 2

    ng = n // SUBLANES
    b3 = b.reshape(ng, SUBLANES, w)
    k3 = k.reshape(ng, SUBLANES, w)
    row8 = rowv % SUBLANES
    diag = []
    for s in range(SUBLANES):
        bs = jnp.broadcast_to(b3[:, s:s + 1, :], (ng, SUBLANES, w)).reshape(n, w)
        ks = jnp.broadcast_to(k3[:, s:s + 1, :], (ng, SUBLANES, w)).reshape(n, w)
        m = (row8 <= s) if reverse else (row8 >= s)
        e = jnp.exp(jnp.where(m, b - bs, NEG_BIG))
        diag.append(qs * ks * e)
    dcol = (row // SUBLANES) * SUBLANES

    for h in range(A_HEADS):
        sl = slice(h * hd, (h + 1) * hd)
        sm = jnp.zeros((n, n), F32)
        for ql, kl, same in lvl:
            sm = sm + jnp.where(same, _dot_nt(ql[:, sl], kl[:, sl]), 0.0)
        for s in range(SUBLANES):
            sc = jnp.sum(diag[s][:, sl], axis=-1, keepdims=True)
            sm = jnp.where(col == dcol + s, sc, sm)
        st = st_ref[h]
        vh = v[:, sl]
        o = _dot_nt(qe[:, sl], st.astype(BF16)) + _dot(sm.astype(BF16), vh.astype(BF16))
        o_ref[:, sl] = o
        st_ref[h] = st * dec[:, sl] + _dot(vh.T.astype(BF16), kd[:, sl])


def _hgrn_kernel(lbf_ref, lbb_ref, qf_ref, vf_ref, zf_ref, qb_ref, vb_ref, zb_ref,
                 of_ref, ob_ref, stf_ref, stb_ref):
    @pl.when(pl.program_id(0) == 0)
    def _():
        stf_ref[...] = jnp.zeros_like(stf_ref)
        stb_ref[...] = jnp.zeros_like(stb_ref)

    _hgrn_direction(qf_ref, vf_ref, zf_ref, lbf_ref[...], stf_ref, of_ref, False)
    _hgrn_direction(qb_ref, vb_ref, zb_ref, lbb_ref[...], stb_ref, ob_ref, True)


def _hgrn(h, lb_f, lb_b, mix_w):
    s = h.shape[0]
    n = HG_BLOCK
    nb = s // n
    hd = mix_w // A_HEADS
    fwd = lambda c: pl.BlockSpec((n, mix_w), lambda i: (i, c))
    bwd = lambda c: pl.BlockSpec((n, mix_w), lambda i: (nb - 1 - i, c))
    vec = pl.BlockSpec((1, mix_w), lambda i: (0, 0))
    return pl.pallas_call(
        _hgrn_kernel,
        out_shape=(jax.ShapeDtypeStruct((s, mix_w), F32), jax.ShapeDtypeStruct((s, mix_w), F32)),
        grid=(nb,),
        in_specs=[vec, vec, fwd(0), fwd(1), fwd(2), bwd(0), bwd(1), bwd(3)],
        out_specs=(pl.BlockSpec((n, mix_w), lambda i: (i, 0)),
                   pl.BlockSpec((n, mix_w), lambda i: (nb - 1 - i, 0))),
        scratch_shapes=[pltpu.VMEM((A_HEADS, hd, hd), F32), pltpu.VMEM((A_HEADS, hd, hd), F32)],
        compiler_params=_cparams(("arbitrary",)),
        name="hgrn_scan",
    )(lb_f, lb_b, h, h, h, h, h, h)


def _halo_ext(prev_ref, cur_ref, next_ref):
    i = pl.program_id(0)
    last = pl.num_programs(0) - 1
    prev = jnp.where(i == 0, 0.0, prev_ref[...])
    nxt = jnp.where(i == last, 0.0, next_ref[...])
    return jnp.concatenate([prev, cur_ref[...], nxt], axis=0)


def _even_tail_kernel(seq_len, alpha, of_ref, ob_ref, og_ref, up_ref, u_ref, un_ref, x_ref,
                      ng_ref, pw_ref, ps_ref, wo_ref, lg_ref, lb_ref, o_ref):
    t = x_ref.shape[0]
    mix_w = of_ref.shape[1]
    hd = mix_w // A_HEADS
    o = of_ref[...] + ob_ref[...]
    gate = _silu(og_ref[...])
    ng = ng_ref[...]
    ya = []
    for h in range(A_HEADS):
        sl = slice(h * hd, (h + 1) * hd)
        oh = o[:, sl]
        r = lax.rsqrt(jnp.mean(oh * oh, axis=-1, keepdims=True) + EPS)
        ya.append(oh * r * ng[:, sl] * gate[:, sl])
    ya = jnp.concatenate(ya, axis=-1).astype(BF16)

    ext = _halo_ext(up_ref, u_ref, un_ref)
    rows = ext.shape[0]
    gw = mix_w // len(POOL_WINDOWS)
    tpos = pl.program_id(0) * t + lax.broadcasted_iota(jnp.int32, (t, 1), 0)
    ps = ps_ref[...]
    yb = []
    for gi, win in enumerate(POOL_WINDOWS):
        sl = slice(gi * gw, (gi + 1) * gw)
        e = ext[:, sl]
        acc = e + pltpu.roll(e, 1, axis=0)
        span = 2
        while span < win:
            sh = span // 2
            acc = pltpu.roll(acc, rows - sh, axis=0) + pltpu.roll(acc, sh, axis=0)
            span *= 2
        wsum = acc[HALO:HALO + t, :]
        lo = jnp.maximum(tpos - win // 2, 0)
        hi = jnp.minimum(tpos - win // 2 + win - 1, seq_len - 1)
        cnt = (hi - lo + 1).astype(F32)
        d = wsum / cnt - e[HALO:HALO + t, :]
        yb.append(_dot(d.astype(BF16), pw_ref[gi]) * ps[:, sl])
    yb = jnp.concatenate(yb, axis=-1).astype(BF16)

    y = _dot(ya, wo_ref[0:mix_w, :]) + _dot(yb, wo_ref[mix_w:2 * mix_w, :])
    o_ref[...] = _layer_norm(alpha * x_ref[...] + y, lg_ref[...], lb_ref[...])


def _even_tail(o_f, o_b, h, x, norm_g, pool_w_bf, pool_scale, w_out_bf, ln_g, ln_b, alpha, tm):
    s, d_model = x.shape
    mix_w = o_f.shape[1]
    hb = tm // HALO
    nhalo = s // HALO
    row = lambda c: pl.BlockSpec((tm, mix_w), lambda i: (i, c))
    vec = lambda n: pl.BlockSpec((1, n), lambda i: (0, 0))
    return pl.pallas_call(
        functools.partial(_even_tail_kernel, s, alpha),
        out_shape=jax.ShapeDtypeStruct((s, d_model), F32),
        grid=(s // tm,),
        in_specs=[row(0), row(0), row(4),
                  pl.BlockSpec((HALO, mix_w), lambda i: (jnp.maximum(i * hb - 1, 0), 5)),
                  row(5),
                  pl.BlockSpec((HALO, mix_w), lambda i: (jnp.minimum((i + 1) * hb, nhalo - 1), 5)),
                  pl.BlockSpec((tm, d_model), lambda i: (i, 0)),
                  vec(mix_w),
                  pl.BlockSpec(pool_w_bf.shape, lambda i: (0, 0, 0)),
                  vec(mix_w),
                  pl.BlockSpec(w_out_bf.shape, lambda i: (0, 0)),
                  vec(d_model), vec(d_model)],
        out_specs=pl.BlockSpec((tm, d_model), lambda i: (i, 0)),
        compiler_params=_cparams(("parallel",)),
        name="even_tail",
    )(o_f, o_b, h, h, h, h, x, norm_g, pool_w_bf, pool_scale, w_out_bf, ln_g, ln_b)


def _rope(x, cos, sin_lo, sin_hi):
    n = x.shape[-1]
    return x * cos + pltpu.roll(x, n - 32, axis=1) * sin_lo + pltpu.roll(x, 32, axis=1) * sin_hi


def _odd_in_kernel(q_scale, x_ref, w_ref, qg_ref, kg_ref, cos_ref, sl_ref, sh_ref,
                   q_ref, k_ref, v_ref, u_ref):
    hd = cos_ref.shape[1]
    qw = q_ref.shape[1]
    kw = k_ref.shape[1]
    mix_w = u_ref.shape[1]
    h = _dot(x_ref[...].astype(BF16), w_ref[...])
    cos, s_lo, s_hi = cos_ref[...], sl_ref[...], sh_ref[...]

    def norm_rope(a, g):
        r = lax.rsqrt(jnp.mean(a * a, axis=-1, keepdims=True) + EPS)
        return _rope(a * r * g, cos, s_lo, s_hi)

    qg = qg_ref[...]
    kg = kg_ref[...]
    for i in range(qw // hd):
        q_ref[:, i * hd:(i + 1) * hd] = (norm_rope(h[:, i * hd:(i + 1) * hd], qg) * q_scale).astype(BF16)
    for i in range(kw // hd):
        k_ref[:, i * hd:(i + 1) * hd] = norm_rope(h[:, qw + i * hd:qw + (i + 1) * hd], kg).astype(BF16)
    v_ref[...] = h[:, qw + kw:qw + 2 * kw].astype(BF16)
    a0 = qw + 2 * kw
    u_ref[...] = h[:, a0:a0 + mix_w] * jax.nn.sigmoid(h[:, a0 + mix_w:a0 + 2 * mix_w])


def _odd_in(x, w_bf, q_g, k_g, cos, sin_lo, sin_hi, mix_w, kv_w, tm):
    s, d_model = x.shape
    hd = cos.shape[1]
    q_scale = hd ** -0.5 * math.log2(math.e)
    vec = pl.BlockSpec((1, hd), lambda i: (0, 0))
    tab = pl.BlockSpec((tm, hd), lambda i: (i, 0))
    return pl.pallas_call(
        functools.partial(_odd_in_kernel, q_scale),
        out_shape=(jax.ShapeDtypeStruct((s, mix_w), BF16), jax.ShapeDtypeStruct((s, kv_w), BF16),
                   jax.ShapeDtypeStruct((s, kv_w), BF16), jax.ShapeDtypeStruct((s, mix_w), F32)),
        grid=(s // tm,),
        in_specs=[pl.BlockSpec((tm, d_model), lambda i: (i, 0)),
                  pl.BlockSpec(w_bf.shape, lambda i: (0, 0)),
                  vec, vec, tab, tab, tab],
        out_specs=(pl.BlockSpec((tm, mix_w), lambda i: (i, 0)), pl.BlockSpec((tm, kv_w), lambda i: (i, 0)),
                   pl.BlockSpec((tm, kv_w), lambda i: (i, 0)), pl.BlockSpec((tm, mix_w), lambda i: (i, 0))),
        compiler_params=_cparams(("parallel",)),
        name="odd_in",
    )(x, w_bf, q_g, k_g, cos, sin_lo, sin_hi)


def _attn_kernel(tk, q_ref, k_ref, v_ref, o_ref, m_ref, l_ref, acc_ref):
    tq = q_ref.shape[0]
    hd = k_ref.shape[1]
    grp = q_ref.shape[1] // hd
    s_len = k_ref.shape[0]
    q = jnp.concatenate([q_ref[:, g * hd:(g + 1) * hd] for g in range(grp)], axis=0)
    m_ref[...] = jnp.full(m_ref.shape, -jnp.inf, F32)
    l_ref[...] = jnp.zeros(l_ref.shape, F32)
    acc_ref[...] = jnp.zeros(acc_ref.shape, F32)

    def body(j, carry):
        off = pl.multiple_of(j * tk, tk)
        kt = k_ref[pl.ds(off, tk), :]
        vt = v_ref[pl.ds(off, tk), :]
        s = _dot_nt(q, kt)
        m_old = m_ref[...]
        m_new = jnp.maximum(m_old, jnp.max(s, axis=-1, keepdims=True))
        p = jnp.exp2(s - m_new)
        alpha = jnp.exp2(m_old - m_new)
        l_ref[...] = alpha * l_ref[...] + jnp.sum(p, axis=-1, keepdims=True)
        acc_ref[...] = alpha * acc_ref[...] + _dot(p.astype(BF16), vt)
        m_ref[...] = m_new
        return carry

    lax.fori_loop(0, s_len // tk, body, 0)
    out = acc_ref[...] / l_ref[...]
    for g in range(grp):
        o_ref[:, g * hd:(g + 1) * hd] = out[g * tq:(g + 1) * tq, :].astype(o_ref.dtype)


def _attention(q, k, v, hd, tq, tk):
    s, qw = q.shape
    kvh = k.shape[1] // hd
    gw = qw // kvh
    return pl.pallas_call(
        functools.partial(_attn_kernel, tk),
        out_shape=jax.ShapeDtypeStruct((s, qw), BF16),
        grid=(kvh, s // tq),
        in_specs=[pl.BlockSpec((tq, gw), lambda h, i: (i, h)),
                  pl.BlockSpec((s, hd), lambda h, i: (0, h)),
                  pl.BlockSpec((s, hd), lambda h, i: (0, h))],
        out_specs=pl.BlockSpec((tq, gw), lambda h, i: (i, h)),
        scratch_shapes=[pltpu.VMEM((gw // hd * tq, 1), F32), pltpu.VMEM((gw // hd * tq, 1), F32),
                        pltpu.VMEM((gw // hd * tq, hd), F32)],
        compiler_params=_cparams(("parallel", "parallel")),
        name="gqa_attention",
    )(q, k, v)


def _odd_tail_kernel(alpha, yc_ref, up_ref, u_ref, un_ref, x_ref, cw_ref, cb_ref, cg_ref, cbeta_ref,
                     wo_ref, lg_ref, lb_ref, o_ref, ext_ref):
    t = x_ref.shape[0]
    mix_w = u_ref.shape[1]
    ext_ref[...] = _halo_ext(up_ref, u_ref, un_ref)
    base = HALO - CONV_W // 2
    acc = jnp.zeros((t, mix_w), F32)
    for j in range(CONV_W):
        acc = acc + ext_ref[pl.ds(base + j, t), :] * cw_ref[j:j + 1, :]
    conv = acc + cb_ref[...]
    yd = _silu(_layer_norm(conv, cg_ref[...], cbeta_ref[...])).astype(BF16)
    y = _dot(yc_ref[...], wo_ref[0:mix_w, :]) + _dot(yd, wo_ref[mix_w:2 * mix_w, :])
    o_ref[...] = _layer_norm(alpha * x_ref[...] + y, lg_ref[...], lb_ref[...])


def _odd_tail(y_c, u, x, conv_w, conv_b, conv_g, conv_beta, w_out_bf, ln_g, ln_b, alpha, tm):
    s, d_model = x.shape
    mix_w = u.shape[1]
    hb = tm // HALO
    nhalo = s // HALO
    row = pl.BlockSpec((tm, mix_w), lambda i: (i, 0))
    vec = lambda n: pl.BlockSpec((1, n), lambda i: (0, 0))
    return pl.pallas_call(
        functools.partial(_odd_tail_kernel, alpha),
        out_shape=jax.ShapeDtypeStruct((s, d_model), F32),
        grid=(s // tm,),
        in_specs=[row,
                  pl.BlockSpec((HALO, mix_w), lambda i: (jnp.maximum(i * hb - 1, 0), 0)),
                  row,
                  pl.BlockSpec((HALO, mix_w), lambda i: (jnp.minimum((i + 1) * hb, nhalo - 1), 0)),
                  pl.BlockSpec((tm, d_model), lambda i: (i, 0)),
                  pl.BlockSpec(conv_w.shape, lambda i: (0, 0)),
                  vec(mix_w), vec(mix_w), vec(mix_w),
                  pl.BlockSpec(w_out_bf.shape, lambda i: (0, 0)),
                  vec(d_model), vec(d_model)],
        out_specs=pl.BlockSpec((tm, d_model), lambda i: (i, 0)),
        scratch_shapes=[pltpu.VMEM((tm + 2 * HALO, mix_w), F32)],
        compiler_params=_cparams(("parallel",)),
        name="odd_tail",
    )(y_c, u, u, u, x, conv_w, conv_b, conv_g, conv_beta, w_out_bf, ln_g, ln_b)


def _xattn_kernel(alpha, x_ref, wq_ref, k_ref, v_ref, wo_ref, lg_ref, lb_ref, o_ref):
    d_model = x_ref.shape[1]
    hd = d_model // XA_HEADS
    x = x_ref[...]
    q = (_dot(x.astype(BF16), wq_ref[...]) * (hd ** -0.5)).astype(BF16)
    outs = []
    for h in range(XA_HEADS):
        sl = slice(h * hd, (h + 1) * hd)
        s = _dot_nt(q[:, sl], k_ref[:, sl])
        m = jnp.max(s, axis=-1, keepdims=True)
        p = jnp.exp(s - m)
        p = p / jnp.sum(p, axis=-1, keepdims=True)
        outs.append(_dot(p.astype(BF16), v_ref[:, sl]))
    o = jnp.concatenate(outs, axis=-1).astype(BF16)
    y = _dot(o, wo_ref[...])
    o_ref[...] = _layer_norm(alpha * x + y, lg_ref[...], lb_ref[...])


def _xattn(x, wq_bf, k_bf, v_bf, wo_bf, ln_g, ln_b, alpha, tm):
    s, d_model = x.shape
    full = lambda a: pl.BlockSpec(a.shape, lambda i: (0, 0))
    vec = pl.BlockSpec((1, d_model), lambda i: (0, 0))
    return pl.pallas_call(
        functools.partial(_xattn_kernel, alpha),
        out_shape=jax.ShapeDtypeStruct((s, d_model), F32),
        grid=(s // tm,),
        in_specs=[pl.BlockSpec((tm, d_model), lambda i: (i, 0)),
                  full(wq_bf), full(k_bf), full(v_bf), full(wo_bf), vec, vec],
        out_specs=pl.BlockSpec((tm, d_model), lambda i: (i, 0)),
        compiler_params=_cparams(("parallel",)),
        name="mem_xattn",
    )(x, wq_bf, k_bf, v_bf, wo_bf, ln_g, ln_b)


def _ffn_kernel(ck, alpha, x_ref, wg_ref, wu_ref, wd_ref, lg_ref, lb_ref, o_ref):
    d_ff = wg_ref.shape[1]
    x = x_ref[...]
    xb = x.astype(BF16)
    y = jnp.zeros(x.shape, F32)
    for c in range(d_ff // ck):
        sl = slice(c * ck, (c + 1) * ck)
        hcn = _silu(_dot(xb, wg_ref[:, sl])) * _dot(xb, wu_ref[:, sl])
        y = y + _dot(hcn.astype(BF16), wd_ref[sl, :])
    o_ref[...] = _layer_norm(alpha * x + y, lg_ref[...], lb_ref[...])


def _ffn(x, wg_bf, wu_bf, wd_bf, ln_g, ln_b, alpha, tm, ck):
    s, d_model = x.shape
    full = lambda a: pl.BlockSpec(a.shape, lambda i: (0, 0))
    vec = pl.BlockSpec((1, d_model), lambda i: (0, 0))
    return pl.pallas_call(
        functools.partial(_ffn_kernel, ck, alpha),
        out_shape=jax.ShapeDtypeStruct((s, d_model), F32),
        grid=(s // tm,),
        in_specs=[pl.BlockSpec((tm, d_model), lambda i: (i, 0)),
                  full(wg_bf), full(wu_bf), full(wd_bf), vec, vec],
        out_specs=pl.BlockSpec((tm, d_model), lambda i: (i, 0)),
        compiler_params=_cparams(("parallel",)),
        name="swiglu_ffn",
    )(x, wg_bf, wu_bf, wd_bf, ln_g, ln_b)


def _rope_tables(s, hd):
    rows = s // GRID_W
    row = jnp.repeat(jnp.arange(rows), GRID_W)
    col = jnp.tile(jnp.arange(GRID_W), rows)
    half = hd // 2
    freqs = ROPE_THETA ** (-jnp.arange(0, half, 2, dtype=F32) / half)

    def ang(p):
        a = p.astype(F32)[:, None] * freqs[None, :]
        return jnp.concatenate([a, a], axis=-1)

    angles = jnp.concatenate([ang(row), ang(col)], axis=-1)
    cos, sin = jnp.cos(angles), jnp.sin(angles)
    lo = (jnp.arange(hd) % (hd // 2)) < (hd // 4)
    return cos, jnp.where(lo, -sin, 0.0), jnp.where(lo, 0.0, sin)


def kernel(x, mem, w_in_ab, hgrn_lb_logits, hgrn_norm_g, pool_w, pool_scale, w_out_ab, w_in_cd, q_norm_g, k_norm_g, conv_w, conv_b, conv_ln_g, conv_ln_b, w_out_cd, xa_wq, xa_wkv, xa_wo, ffn_w_gu, ffn_w_down, ln_g, ln_b):
    depth = xa_wq.shape[0]
    alpha = (2 * depth) ** 0.25
    bsz, s, d_model = x.shape
    mix_w = d_model // 2
    hd_c = mix_w // C_HEADS
    kv_w = C_KV_HEADS * hd_c
    d_ff = ffn_w_down.shape[1]
    tm = 512

    cum = jnp.cumsum(jax.nn.softmax(hgrn_lb_logits.astype(F32), axis=1), axis=1)
    lb = jnp.maximum(cum - cum[:, :1], 0.0)
    cos, sin_lo, sin_hi = _rope_tables(s, hd_c)
    row = lambda a: a.reshape(1, -1)

    outs = []
    for bi in range(bsz):
        xb = x[bi]
        memb = mem[bi]
        for l in range(depth):
            j = l // 2
            if l % 2 == 0:
                h = _proj(xb, w_in_ab[j].astype(BF16), tm, "even_in")
                o_f, o_b = _hgrn(h, row(lb[0, l]), row(lb[1, l]), mix_w)
                xb = _even_tail(o_f, o_b, h, xb, row(hgrn_norm_g[j]), pool_w[j].astype(BF16),
                                row(pool_scale[j]), w_out_ab[j].astype(BF16),
                                row(ln_g[l, 0]), row(ln_b[l, 0]), alpha, tm)
            else:
                q, k, v, u = _odd_in(xb, w_in_cd[j].astype(BF16), row(q_norm_g[j]), row(k_norm_g[j]),
                                     cos, sin_lo, sin_hi, mix_w, kv_w, tm)
                y_c = _attention(q, k, v, hd_c, min(256, s), min(512, s))
                xb = _odd_tail(y_c, u, xb, conv_w[j], row(conv_b[j]), row(conv_ln_g[j]),
                               row(conv_ln_b[j]), w_out_cd[j].astype(BF16),
                               row(ln_g[l, 0]), row(ln_b[l, 0]), alpha, tm)
            kv = _proj(memb, xa_wkv[l].astype(BF16), memb.shape[0], "mem_kv")
            xb = _xattn(xb, xa_wq[l].astype(BF16), kv[:, :d_model].astype(BF16),
                        kv[:, d_model:].astype(BF16), xa_wo[l].astype(BF16),
                        row(ln_g[l, 1]), row(ln_b[l, 1]), alpha, tm)
            xb = _ffn(xb, ffn_w_gu[l, :, :d_ff].astype(BF16), ffn_w_gu[l, :, d_ff:].astype(BF16),
                      ffn_w_down[l].astype(BF16), row(ln_g[l, 2]), row(ln_b[l, 2]), alpha, tm, 256)
        outs.append(xb)
    return jnp.stack(outs, axis=0)
```

```python
import functools
import math

import jax
import jax.numpy as jnp
from jax import lax
from jax.experimental import pallas as pl
from jax.experimental.pallas import tpu as pltpu

F32 = jnp.float32
BF16 = jnp.bfloat16

A_HEADS = 4
POOL_WINDOWS = (2, 4, 8, 16)
C_HEADS = 4
C_KV_HEADS = 2
GRID_W = 64
ROPE_THETA = 10000.0
CONV_W = 31
XA_HEADS = 4
EPS = 1e-6

LANES = 128
SUBLANES = 8
VMEM_LIMIT = 56 * 1024 * 1024

HALO = 16
HG_BLOCK = 128
NEG_BIG = -1e30
ATTN_SUM_ROWS = 16


def _cparams(sem):
    return pltpu.CompilerParams(dimension_semantics=sem, vmem_limit_bytes=VMEM_LIMIT)


def _silu(x):
    return x * jax.nn.sigmoid(x)


def _layer_norm(y, g, b):
    mu = jnp.mean(y, axis=-1, keepdims=True)
    d = y - mu
    var = jnp.mean(d * d, axis=-1, keepdims=True)
    return d * lax.rsqrt(var + EPS) * g + b


def _dot(a, b):
    return jnp.dot(a, b, preferred_element_type=F32)


def _dot_nt(a, b):
    return lax.dot_general(a, b, (((1,), (1,)), ((), ())), preferred_element_type=F32)


def _proj_kernel(x_ref, w_ref, o_ref):
    o_ref[...] = _dot(x_ref[...].astype(BF16), w_ref[...])


def _proj(x, w_bf, tm, name):
    m, k = x.shape
    n = w_bf.shape[1]
    return pl.pallas_call(
        _proj_kernel,
        out_shape=jax.ShapeDtypeStruct((m, n), F32),
        grid=(m // tm,),
        in_specs=[pl.BlockSpec((tm, k), lambda i: (i, 0)),
                  pl.BlockSpec((k, n), lambda i: (0, 0))],
        out_specs=pl.BlockSpec((tm, n), lambda i: (i, 0)),
        compiler_params=_cparams(("parallel",)),
        name=name,
    )(x, w_bf)


def _hgrn_direction(q_ref, v_ref, z_ref, lb, st_ref, o_ref, reverse):
    n = HG_BLOCK
    w = q_ref.shape[1]
    hd = w // A_HEADS
    z = z_ref[...]
    qs = _silu(q_ref[...])
    v = v_ref[...]
    k = (1.0 - lb) * jax.nn.sigmoid(-z)
    g = jnp.logaddexp(jnp.log(lb), jnp.log1p(-lb) + jax.nn.log_sigmoid(z))

    row = lax.broadcasted_iota(jnp.int32, (n, n), 0)
    col = lax.broadcasted_iota(jnp.int32, (n, n), 1)
    rowv = lax.broadcasted_iota(jnp.int32, (n, 1), 0)
    tri = (col >= row) if reverse else (col <= row)
    tri = jnp.where(tri, 1.0, 0.0).astype(BF16)
    g1 = g.astype(BF16)
    r1 = g - g1.astype(F32)
    g2 = r1.astype(BF16)
    g3 = (r1 - g2.astype(F32)).astype(BF16)
    b = _dot(tri, g1) + _dot(tri, g2) + _dot(tri, g3)

    end = 0 if reverse else n - 1
    b_end = b[end:end + 1, :]
    qe = (qs * jnp.exp(b)).astype(BF16)
    kd = (k * jnp.exp(b_end - b)).astype(BF16)
    dec = jnp.exp(b_end)

    lvl = []
    half = n // 2
    while half >= SUBLANES:
        two = 2 * half
        parts = []
        for blk in range(n // two):
            r = blk * two + (half if reverse else half - 1)
            parts.append(jnp.broadcast_to(b[r:r + 1, :], (two, w)))
        bref = jnp.concatenate(parts, axis=0) if len(parts) > 1 else parts[0]
        second = (rowv % two) >= half
        q_side = jnp.logical_not(second) if reverse else second
        d = b - bref
        e = jnp.exp(jnp.where(q_side, d, -d))
        ql = (qs * jnp.where(q_side, e, 0.0)).astype(BF16)
        kl = (k * jnp.where(q_side, 0.0, e)).astype(BF16)
        same = (row // two) == (col // two)
        lvl.append((ql, kl, same))
        half //= 2

    ng = n // SUBLANES
    b3 = b.reshape(ng, SUBLANES, w)
    k3 = k.reshape(ng, SUBLANES, w)
    row8 = rowv % SUBLANES
    diag = []
    for s in range(SUBLANES):
        bs = jnp.broadcast_to(b3[:, s:s + 1, :], (ng, SUBLANES, w)).reshape(n, w)
        ks = jnp.broadcast_to(k3[:, s:s + 1, :], (ng, SUBLANES, w)).reshape(n, w)
        m = (row8 <= s) if reverse else (row8 >= s)
        e = jnp.exp(jnp.where(m, b - bs, NEG_BIG))
        diag.append(qs * ks * e)
    dcol = (row // SUBLANES) * SUBLANES

    for h in range(A_HEADS):
        sl = slice(h * hd, (h + 1) * hd)
        sm = jnp.zeros((n, n), F32)
        for ql, kl, same in lvl:
            sm = sm + jnp.where(same, _dot_nt(ql[:, sl], kl[:, sl]), 0.0)
        for s in range(SUBLANES):
            sc = jnp.sum(diag[s][:, sl], axis=-1, keepdims=True)
            sm = jnp.where(col == dcol + s, sc, sm)
        st = st_ref[h]
        vh = v[:, sl]
        o = _dot_nt(qe[:, sl], st.astype(BF16)) + _dot(sm.astype(BF16), vh.astype(BF16))
        o_ref[:, sl] = o
        st_ref[h] = st * dec[:, sl] + _dot(vh.T.astype(BF16), kd[:, sl])


def _hgrn_kernel(lbf_ref, lbb_ref, qf_ref, vf_ref, zf_ref, qb_ref, vb_ref, zb_ref,
                 of_ref, ob_ref, stf_ref, stb_ref):
    @pl.when(pl.program_id(0) == 0)
    def _():
        stf_ref[...] = jnp.zeros_like(stf_ref)
        stb_ref[...] = jnp.zeros_like(stb_ref)

    _hgrn_direction(qf_ref, vf_ref, zf_ref, lbf_ref[...], stf_ref, of_ref, False)
    _hgrn_direction(qb_ref, vb_ref, zb_ref, lbb_ref[...], stb_ref, ob_ref, True)


def _hgrn(h, lb_f, lb_b, mix_w):
    s = h.shape[0]
    n = HG_BLOCK
    nb = s // n
    hd = mix_w // A_HEADS
    fwd = lambda c: pl.BlockSpec((n, mix_w), lambda i: (i, c))
    bwd = lambda c: pl.BlockSpec((n, mix_w), lambda i: (nb - 1 - i, c))
    vec = pl.BlockSpec((1, mix_w), lambda i: (0, 0))
    return pl.pallas_call(
        _hgrn_kernel,
        out_shape=(jax.ShapeDtypeStruct((s, mix_w), F32), jax.ShapeDtypeStruct((s, mix_w), F32)),
        grid=(nb,),
        in_specs=[vec, vec, fwd(0), fwd(1), fwd(2), bwd(0), bwd(1), bwd(3)],
        out_specs=(pl.BlockSpec((n, mix_w), lambda i: (i, 0)),
                   pl.BlockSpec((n, mix_w), lambda i: (nb - 1 - i, 0))),
        scratch_shapes=[pltpu.VMEM((A_HEADS, hd, hd), F32), pltpu.VMEM((A_HEADS, hd, hd), F32)],
        compiler_params=_cparams(("arbitrary",)),
        name="hgrn_scan",
    )(lb_f, lb_b, h, h, h, h, h, h)


def _halo_ext(prev_ref, cur_ref, next_ref):
    i = pl.program_id(0)
    last = pl.num_programs(0) - 1
    prev = jnp.where(i == 0, 0.0, prev_ref[...])
    nxt = jnp.where(i == last, 0.0, next_ref[...])
    return jnp.concatenate([prev, cur_ref[...], nxt], axis=0)


def _even_tail_kernel(seq_len, alpha, of_ref, ob_ref, og_ref, up_ref, u_ref, un_ref, x_ref,
                      ng_ref, pw_ref, ps_ref, wo_ref, lg_ref, lb_ref, o_ref):
    t = x_ref.shape[0]
    mix_w = of_ref.shape[1]
    hd = mix_w // A_HEADS
    o = of_ref[...] + ob_ref[...]
    gate = _silu(og_ref[...])
    ng = ng_ref[...]
    ya = []
    for h in range(A_HEADS):
        sl = slice(h * hd, (h + 1) * hd)
        oh = o[:, sl]
        r = lax.rsqrt(jnp.mean(oh * oh, axis=-1, keepdims=True) + EPS)
        ya.append(oh * r * ng[:, sl] * gate[:, sl])
    ya = jnp.concatenate(ya, axis=-1).astype(BF16)

    ext = _halo_ext(up_ref, u_ref, un_ref)
    rows = ext.shape[0]
    gw = mix_w // len(POOL_WINDOWS)
    tpos = pl.program_id(0) * t + lax.broadcasted_iota(jnp.int32, (t, 1), 0)
    ps = ps_ref[...]
    yb = []
    for gi, win in enumerate(POOL_WINDOWS):
        sl = slice(gi * gw, (gi + 1) * gw)
        e = ext[:, sl]
        acc = e + pltpu.roll(e, 1, axis=0)
        span = 2
        while span < win:
            sh = span // 2
            acc = pltpu.roll(acc, rows - sh, axis=0) + pltpu.roll(acc, sh, axis=0)
            span *= 2
        wsum = acc[HALO:HALO + t, :]
        lo = jnp.maximum(tpos - win // 2, 0)
        hi = jnp.minimum(tpos - win // 2 + win - 1, seq_len - 1)
        cnt = (hi - lo + 1).astype(F32)
        d = wsum / cnt - e[HALO:HALO + t, :]
        yb.append(_dot(d.astype(BF16), pw_ref[gi]) * ps[:, sl])
    yb = jnp.concatenate(yb, axis=-1).astype(BF16)

    y = _dot(ya, wo_ref[0:mix_w, :]) + _dot(yb, wo_ref[mix_w:2 * mix_w, :])
    o_ref[...] = _layer_norm(alpha * x_ref[...] + y, lg_ref[...], lb_ref[...])


def _even_tail(o_f, o_b, h, x, norm_g, pool_w_bf, pool_scale, w_out_bf, ln_g, ln_b, alpha, tm):
    s, d_model = x.shape
    mix_w = o_f.shape[1]
    hb = tm // HALO
    nhalo = s // HALO
    row = lambda c: pl.BlockSpec((tm, mix_w), lambda i: (i, c))
    vec = lambda n: pl.BlockSpec((1, n), lambda i: (0, 0))
    return pl.pallas_call(
        functools.partial(_even_tail_kernel, s, alpha),
        out_shape=jax.ShapeDtypeStruct((s, d_model), F32),
        grid=(s // tm,),
        in_specs=[row(0), row(0), row(4),
                  pl.BlockSpec((HALO, mix_w), lambda i: (jnp.maximum(i * hb - 1, 0), 5)),
                  row(5),
                  pl.BlockSpec((HALO, mix_w), lambda i: (jnp.minimum((i + 1) * hb, nhalo - 1), 5)),
                  pl.BlockSpec((tm, d_model), lambda i: (i, 0)),
                  vec(mix_w),
                  pl.BlockSpec(pool_w_bf.shape, lambda i: (0, 0, 0)),
                  vec(mix_w),
                  pl.BlockSpec(w_out_bf.shape, lambda i: (0, 0)),
                  vec(d_model), vec(d_model)],
        out_specs=pl.BlockSpec((tm, d_model), lambda i: (i, 0)),
        compiler_params=_cparams(("parallel",)),
        name="even_tail",
    )(o_f, o_b, h, h, h, h, x, norm_g, pool_w_bf, pool_scale, w_out_bf, ln_g, ln_b)


def _rope(x, cos, sin_lo, sin_hi):
    n = x.shape[-1]
    return x * cos + pltpu.roll(x, n - 32, axis=1) * sin_lo + pltpu.roll(x, 32, axis=1) * sin_hi


def _odd_in_kernel(q_scale, x_ref, w_ref, qg_ref, kg_ref, cos_ref, sl_ref, sh_ref,
                   q_ref, k_ref, v_ref, u_ref):
    hd = cos_ref.shape[1]
    qw = q_ref.shape[0]
    kw = k_ref.shape[1]
    mix_w = u_ref.shape[1]
    h = _dot(x_ref[...].astype(BF16), w_ref[...])
    cos, s_lo, s_hi = cos_ref[...], sl_ref[...], sh_ref[...]

    def norm_rope(a, g):
        r = lax.rsqrt(jnp.mean(a * a, axis=-1, keepdims=True) + EPS)
        return _rope(a * r * g, cos, s_lo, s_hi)

    qg = qg_ref[...]
    kg = kg_ref[...]
    for i in range(qw // hd):
        q_ref[i * hd:(i + 1) * hd, :] = (norm_rope(h[:, i * hd:(i + 1) * hd], qg) * q_scale).T.astype(BF16)
    for i in range(kw // hd):
        k_ref[:, i * hd:(i + 1) * hd] = norm_rope(h[:, qw + i * hd:qw + (i + 1) * hd], kg).astype(BF16)
        v_ref[i * hd:(i + 1) * hd, :] = h[:, qw + kw + i * hd:qw + kw + (i + 1) * hd].T.astype(BF16)
    a0 = qw + 2 * kw
    u_ref[...] = h[:, a0:a0 + mix_w] * jax.nn.sigmoid(h[:, a0 + mix_w:a0 + 2 * mix_w])


def _odd_in(x, w_bf, q_g, k_g, cos, sin_lo, sin_hi, mix_w, kv_w, tm):
    s, d_model = x.shape
    hd = cos.shape[1]
    q_scale = hd ** -0.5 * math.log2(math.e)
    vec = pl.BlockSpec((1, hd), lambda i: (0, 0))
    tab = pl.BlockSpec((tm, hd), lambda i: (i, 0))
    return pl.pallas_call(
        functools.partial(_odd_in_kernel, q_scale),
        out_shape=(jax.ShapeDtypeStruct((mix_w, s), BF16), jax.ShapeDtypeStruct((s, kv_w), BF16),
                   jax.ShapeDtypeStruct((kv_w, s), BF16), jax.ShapeDtypeStruct((s, mix_w), F32)),
        grid=(s // tm,),
        in_specs=[pl.BlockSpec((tm, d_model), lambda i: (i, 0)),
                  pl.BlockSpec(w_bf.shape, lambda i: (0, 0)),
                  vec, vec, tab, tab, tab],
        out_specs=(pl.BlockSpec((mix_w, tm), lambda i: (0, i)), pl.BlockSpec((tm, kv_w), lambda i: (i, 0)),
                   pl.BlockSpec((kv_w, tm), lambda i: (0, i)), pl.BlockSpec((tm, mix_w), lambda i: (i, 0))),
        compiler_params=_cparams(("parallel",)),
        name="odd_in",
    )(x, w_bf, q_g, k_g, cos, sin_lo, sin_hi)


def _attn_kernel(tk, qt_ref, k_ref, vt_ref, o_ref, acc_ref, s_ref, p_ref):
    hd = k_ref.shape[1]
    grp = qt_ref.shape[0] // hd
    tq = qt_ref.shape[1]
    s_len = k_ref.shape[0]
    n = grp * tq
    qt = jnp.concatenate([qt_ref[g * hd:(g + 1) * hd, :] for g in range(grp)], axis=1)
    nt = s_len // tk

    def scores(t):
        off = pl.multiple_of(t * tk, tk)
        return _dot(k_ref[pl.ds(off, tk), :], qt)

    ones_rows = jnp.ones((ATTN_SUM_ROWS, tk), BF16)

    def weighted_values(t, slot):
        off = pl.multiple_of(t * tk, tk)
        vt = jnp.concatenate([vt_ref[:, pl.ds(off, tk)], ones_rows], axis=0)
        return _dot(vt, p_ref[slot])

    def step(t, cur, carry):
        m_old, a_prev = carry
        s_ref[1 - cur] = scores(jnp.minimum(t + 1, nt - 1))
        acc_ref[...] = a_prev * acc_ref[...] + weighted_values(jnp.maximum(t - 1, 0), 1 - cur)
        s = s_ref[cur]
        m_new = jnp.maximum(m_old, jnp.max(s, axis=0, keepdims=True))
        p_ref[cur] = jnp.exp2(s - m_new).astype(BF16)
        return m_new, jnp.exp2(m_old - m_new)

    def body(j, carry):
        return step(2 * j + 1, 1, step(2 * j, 0, carry))

    acc_ref[...] = jnp.zeros(acc_ref.shape, F32)
    p_ref[1] = jnp.zeros(p_ref.shape[1:], BF16)
    s_ref[0] = scores(0)
    init = (jnp.full((1, n), -jnp.inf, F32), jnp.ones((1, n), F32))
    _, a_last = lax.fori_loop(0, nt // 2, body, init)
    acc = a_last * acc_ref[...] + weighted_values(nt - 1, 1)
    out = acc[0:hd, :] / acc[hd:hd + 1, :]
    for g in range(grp):
        o_ref[:, g * hd:(g + 1) * hd] = out[:, g * tq:(g + 1) * tq].T.astype(o_ref.dtype)


def _attention(qt, k, vt, hd, tq, tk):
    qw, s = qt.shape
    kvh = k.shape[1] // hd
    gw = qw // kvh
    return pl.pallas_call(
        functools.partial(_attn_kernel, tk),
        out_shape=jax.ShapeDtypeStruct((s, qw), BF16),
        grid=(kvh, s // tq),
        in_specs=[pl.BlockSpec((gw, tq), lambda h, i: (h, i)),
                  pl.BlockSpec((s, hd), lambda h, i: (0, h)),
                  pl.BlockSpec((hd, s), lambda h, i: (h, 0))],
        out_specs=pl.BlockSpec((tq, gw), lambda h, i: (i, h)),
        scratch_shapes=[pltpu.VMEM((hd + ATTN_SUM_ROWS, gw // hd * tq), F32),
                        pltpu.VMEM((2, tk, gw // hd * tq), F32),
                        pltpu.VMEM((2, tk, gw // hd * tq), BF16)],
        compiler_params=_cparams(("parallel", "parallel")),
        name="gqa_attention",
    )(qt, k, vt)


def _odd_tail_kernel(alpha, yc_ref, up_ref, u_ref, un_ref, x_ref, cw_ref, cb_ref, cg_ref, cbeta_ref,
                     wo_ref, lg_ref, lb_ref, o_ref, ext_ref):
    t = x_ref.shape[0]
    mix_w = u_ref.shape[1]
    ext_ref[...] = _halo_ext(up_ref, u_ref, un_ref)
    base = HALO - CONV_W // 2
    acc = jnp.zeros((t, mix_w), F32)
    for j in range(CONV_W):
        acc = acc + ext_ref[pl.ds(base + j, t), :] * cw_ref[j:j + 1, :]
    conv = acc + cb_ref[...]
    yd = _silu(_layer_norm(conv, cg_ref[...], cbeta_ref[...])).astype(BF16)
    y = _dot(yc_ref[...], wo_ref[0:mix_w, :]) + _dot(yd, wo_ref[mix_w:2 * mix_w, :])
    o_ref[...] = _layer_norm(alpha * x_ref[...] + y, lg_ref[...], lb_ref[...])


def _odd_tail(y_c, u, x, conv_w, conv_b, conv_g, conv_beta, w_out_bf, ln_g, ln_b, alpha, tm):
    s, d_model = x.shape
    mix_w = u.shape[1]
    hb = tm // HALO
    nhalo = s // HALO
    row = pl.BlockSpec((tm, mix_w), lambda i: (i, 0))
    vec = lambda n: pl.BlockSpec((1, n), lambda i: (0, 0))
    return pl.pallas_call(
        functools.partial(_odd_tail_kernel, alpha),
        out_shape=jax.ShapeDtypeStruct((s, d_model), F32),
        grid=(s // tm,),
        in_specs=[row,
                  pl.BlockSpec((HALO, mix_w), lambda i: (jnp.maximum(i * hb - 1, 0), 0)),
                  row,
                  pl.BlockSpec((HALO, mix_w), lambda i: (jnp.minimum((i + 1) * hb, nhalo - 1), 0)),
                  pl.BlockSpec((tm, d_model), lambda i: (i, 0)),
                  pl.BlockSpec(conv_w.shape, lambda i: (0, 0)),
                  vec(mix_w), vec(mix_w), vec(mix_w),
                  pl.BlockSpec(w_out_bf.shape, lambda i: (0, 0)),
                  vec(d_model), vec(d_model)],
        out_specs=pl.BlockSpec((tm, d_model), lambda i: (i, 0)),
        scratch_shapes=[pltpu.VMEM((tm + 2 * HALO, mix_w), F32)],
        compiler_params=_cparams(("parallel",)),
        name="odd_tail",
    )(y_c, u, u, u, x, conv_w, conv_b, conv_g, conv_beta, w_out_bf, ln_g, ln_b)


def _xattn_kernel(alpha, x_ref, wq_ref, k_ref, v_ref, wo_ref, lg_ref, lb_ref, o_ref):
    d_model = x_ref.shape[1]
    hd = d_model // XA_HEADS
    x = x_ref[...]
    q = (_dot(x.astype(BF16), wq_ref[...]) * (hd ** -0.5)).astype(BF16)
    outs = []
    for h in range(XA_HEADS):
        sl = slice(h * hd, (h + 1) * hd)
        s = _dot_nt(q[:, sl], k_ref[:, sl])
        m = jnp.max(s, axis=-1, keepdims=True)
        p = jnp.exp(s - m)
        p = p / jnp.sum(p, axis=-1, keepdims=True)
        outs.append(_dot(p.astype(BF16), v_ref[:, sl]))
    o = jnp.concatenate(outs, axis=-1).astype(BF16)
    y = _dot(o, wo_ref[...])
    o_ref[...] = _layer_norm(alpha * x + y, lg_ref[...], lb_ref[...])


def _xattn(x, wq_bf, k_bf, v_bf, wo_bf, ln_g, ln_b, alpha, tm):
    s, d_model = x.shape
    full = lambda a: pl.BlockSpec(a.shape, lambda i: (0, 0))
    vec = pl.BlockSpec((1, d_model), lambda i: (0, 0))
    return pl.pallas_call(
        functools.partial(_xattn_kernel, alpha),
        out_shape=jax.ShapeDtypeStruct((s, d_model), F32),
        grid=(s // tm,),
        in_specs=[pl.BlockSpec((tm, d_model), lambda i: (i, 0)),
                  full(wq_bf), full(k_bf), full(v_bf), full(wo_bf), vec, vec],
        out_specs=pl.BlockSpec((tm, d_model), lambda i: (i, 0)),
        compiler_params=_cparams(("parallel",)),
        name="mem_xattn",
    )(x, wq_bf, k_bf, v_bf, wo_bf, ln_g, ln_b)


def _ffn_kernel(ck, alpha, x_ref, wg_ref, wu_ref, wd_ref, lg_ref, lb_ref, o_ref):
    d_ff = wg_ref.shape[1]
    x = x_ref[...]
    xb = x.astype(BF16)
    y = jnp.zeros(x.shape, F32)
    for c in range(d_ff // ck):
        sl = slice(c * ck, (c + 1) * ck)
        hcn = _silu(_dot(xb, wg_ref[:, sl])) * _dot(xb, wu_ref[:, sl])
        y = y + _dot(hcn.astype(BF16), wd_ref[sl, :])
    o_ref[...] = _layer_norm(alpha * x + y, lg_ref[...], lb_ref[...])


def _ffn(x, wg_bf, wu_bf, wd_bf, ln_g, ln_b, alpha, tm, ck):
    s, d_model = x.shape
    full = lambda a: pl.BlockSpec(a.shape, lambda i: (0, 0))
    vec = pl.BlockSpec((1, d_model), lambda i: (0, 0))
    return pl.pallas_call(
        functools.partial(_ffn_kernel, ck, alpha),
        out_shape=jax.ShapeDtypeStruct((s, d_model), F32),
        grid=(s // tm,),
        in_specs=[pl.BlockSpec((tm, d_model), lambda i: (i, 0)),
                  full(wg_bf), full(wu_bf), full(wd_bf), vec, vec],
        out_specs=pl.BlockSpec((tm, d_model), lambda i: (i, 0)),
        compiler_params=_cparams(("parallel",)),
        name="swiglu_ffn",
    )(x, wg_bf, wu_bf, wd_bf, ln_g, ln_b)


def _rope_tables(s, hd):
    rows = s // GRID_W
    row = jnp.repeat(jnp.arange(rows), GRID_W)
    col = jnp.tile(jnp.arange(GRID_W), rows)
    half = hd // 2
    freqs = ROPE_THETA ** (-jnp.arange(0, half, 2, dtype=F32) / half)

    def ang(p):
        a = p.astype(F32)[:, None] * freqs[None, :]
        return jnp.concatenate([a, a], axis=-1)

    angles = jnp.concatenate([ang(row), ang(col)], axis=-1)
    cos, sin = jnp.cos(angles), jnp.sin(angles)
    lo = (jnp.arange(hd) % (hd // 2)) < (hd // 4)
    return cos, jnp.where(lo, -sin, 0.0), jnp.where(lo, 0.0, sin)


def kernel(x, mem, w_in_ab, hgrn_lb_logits, hgrn_norm_g, pool_w, pool_scale, w_out_ab, w_in_cd, q_norm_g, k_norm_g, conv_w, conv_b, conv_ln_g, conv_ln_b, w_out_cd, xa_wq, xa_wkv, xa_wo, ffn_w_gu, ffn_w_down, ln_g, ln_b):
    depth = xa_wq.shape[0]
    alpha = (2 * depth) ** 0.25
    bsz, s, d_model = x.shape
    mix_w = d_model // 2
    hd_c = mix_w // C_HEADS
    kv_w = C_KV_HEADS * hd_c
    d_ff = ffn_w_down.shape[1]
    tm = 512

    cum = jnp.cumsum(jax.nn.softmax(hgrn_lb_logits.astype(F32), axis=1), axis=1)
    lb = jnp.maximum(cum - cum[:, :1], 0.0)
    cos, sin_lo, sin_hi = _rope_tables(s, hd_c)
    row = lambda a: a.reshape(1, -1)

    outs = []
    for bi in range(bsz):
        xb = x[bi]
        memb = mem[bi]
        for l in range(depth):
            j = l // 2
            if l % 2 == 0:
                h = _proj(xb, w_in_ab[j].astype(BF16), tm, "even_in")
                o_f, o_b = _hgrn(h, row(lb[0, l]), row(lb[1, l]), mix_w)
                xb = _even_tail(o_f, o_b, h, xb, row(hgrn_norm_g[j]), pool_w[j].astype(BF16),
                                row(pool_scale[j]), w_out_ab[j].astype(BF16),
                                row(ln_g[l, 0]), row(ln_b[l, 0]), alpha, tm)
            else:
                q, k, v, u = _odd_in(xb, w_in_cd[j].astype(BF16), row(q_norm_g[j]), row(k_norm_g[j]),
                                     cos, sin_lo, sin_hi, mix_w, kv_w, tm)
                y_c = _attention(q, k, v, hd_c, min(512, s), min(512, s))
                xb = _odd_tail(y_c, u, xb, conv_w[j], row(conv_b[j]), row(conv_ln_g[j]),
                               row(conv_ln_b[j]), w_out_cd[j].astype(BF16),
                               row(ln_g[l, 0]), row(ln_b[l, 0]), alpha, tm)
            kv = _proj(memb, xa_wkv[l].astype(BF16), memb.shape[0], "mem_kv")
            xb = _xattn(xb, xa_wq[l].astype(BF16), kv[:, :d_model].astype(BF16),
                        kv[:, d_model:].astype(BF16), xa_wo[l].astype(BF16),
                        row(ln_g[l, 1]), row(ln_b[l, 1]), alpha, tm)
            xb = _ffn(xb, ffn_w_gu[l, :, :d_ff].astype(BF16), ffn_w_gu[l, :, d_ff:].astype(BF16),
                      ffn_w_down[l].astype(BF16), row(ln_g[l, 2]), row(ln_b[l, 2]), alpha, tm, 256)
        outs.append(xb)
    return jnp.stack(outs, axis=0)
```

```python
import functools
import math

import jax
import jax.numpy as jnp
import numpy as np
from jax import lax
from jax.experimental import pallas as pl
from jax.experimental.pallas import tpu as pltpu

F32 = jnp.float32
BF16 = jnp.bfloat16

A_HEADS = 4
POOL_WINDOWS = (2, 4, 8, 16)
C_HEADS = 4
C_KV_HEADS = 2
GRID_W = 64
ROPE_THETA = 10000.0
CONV_W = 31
XA_HEADS = 4
EPS = 1e-6

LANES = 128
SUBLANES = 8
VMEM_LIMIT = 56 * 1024 * 1024

HALO = 16
HG_BLOCK = 128
LOG2E = math.log2(math.e)
XA_SUB_ROWS = 256
CONV_ROWS = 32
ATTN_SUM_ROWS = 16


def _cparams(sem):
    return pltpu.CompilerParams(dimension_semantics=sem, vmem_limit_bytes=VMEM_LIMIT)


def _silu(x):
    return x * jax.nn.sigmoid(x)


def _layer_norm(y, g, b):
    mu = jnp.mean(y, axis=-1, keepdims=True)
    d = y - mu
    var = jnp.mean(d * d, axis=-1, keepdims=True)
    return d * lax.rsqrt(var + EPS) * g + b


def _dot(a, b):
    return jnp.dot(a, b, preferred_element_type=F32)


def _dot_nt(a, b):
    return lax.dot_general(a, b, (((1,), (1,)), ((), ())), preferred_element_type=F32)


def _proj_kernel(x_ref, w_ref, o_ref):
    o_ref[...] = _dot(x_ref[...].astype(BF16), w_ref[...])


def _proj(x, w_bf, tm, name):
    m, k = x.shape
    n = w_bf.shape[1]
    return pl.pallas_call(
        _proj_kernel,
        out_shape=jax.ShapeDtypeStruct((m, n), F32),
        grid=(m // tm,),
        in_specs=[pl.BlockSpec((tm, k), lambda i: (i, 0)),
                  pl.BlockSpec((k, n), lambda i: (0, 0))],
        out_specs=pl.BlockSpec((tm, n), lambda i: (i, 0)),
        compiler_params=_cparams(("parallel",)),
        name=name,
    )(x, w_bf)


def _hgrn_direction(q_ref, v_ref, z_ref, lb, lv_ref, st_ref, o_ref, reverse):
    n = HG_BLOCK
    w = q_ref.shape[1]
    hd = w // A_HEADS
    z = z_ref[...]
    qs = _silu(q_ref[...])
    v = v_ref[...]
    k = (1.0 - lb) * jax.nn.sigmoid(-z)
    g = jnp.logaddexp(jnp.log(lb), jnp.log1p(-lb) + jax.nn.log_sigmoid(z))

    rowv = lax.broadcasted_iota(jnp.int32, (n, 1), 0)
    tri = jnp.where(lv_ref[...] >= 0, 1.0, 0.0).astype(BF16)
    g1 = g.astype(BF16)
    r1 = g - g1.astype(F32)
    g2 = r1.astype(BF16)
    g3 = (r1 - g2.astype(F32)).astype(BF16)
    b = _dot(tri, g1) + _dot(tri, g2) + _dot(tri, g3)

    end = 0 if reverse else n - 1
    b_end = b[end:end + 1, :]
    qe = (qs * jnp.exp(b)).astype(BF16)
    kd = (k * jnp.exp(b_end - b)).astype(BF16)
    dec = jnp.exp(b_end)

    lv = lv_ref[...]
    ng = n // SUBLANES
    b3 = b.reshape(ng, SUBLANES, w)
    sub3 = lax.broadcasted_iota(jnp.int32, (ng, SUBLANES, 1), 1)
    qs_bf = qs.astype(BF16)
    k_bf = k.astype(BF16)
    lvl = [(qs_bf, k_bf, 0)]
    half = n // 2
    while half >= 1:
        two = 2 * half
        off = half if reverse else half - 1
        if half >= SUBLANES:
            parts = [jnp.broadcast_to(b[blk * two + off:blk * two + off + 1, :], (two, w))
                     for blk in range(n // two)]
            bref = jnp.concatenate(parts, axis=0) if len(parts) > 1 else parts[0]
        else:
            bref3 = jnp.broadcast_to(b3[:, off:off + 1, :], (ng, SUBLANES, w))
            for blk in range(1, SUBLANES // two):
                r = blk * two + off
                bref3 = jnp.where(sub3 >= blk * two,
                                  jnp.broadcast_to(b3[:, r:r + 1, :], (ng, SUBLANES, w)), bref3)
            bref = bref3.reshape(n, w)
        second = (rowv % two) >= half
        q_side = jnp.logical_not(second) if reverse else second
        e = jnp.exp2((b - bref) * jnp.where(q_side, LOG2E, -LOG2E)).astype(BF16)
        lvl.append((qs_bf * e, k_bf * e, half))
        half //= 2

    heads = [slice(h * hd, (h + 1) * hd) for h in range(A_HEADS)]
    sms = [jnp.zeros((n, n), F32) for _ in heads]
    for ql, kl, level_id in lvl:
        mask = lv == level_id
        sms = [jnp.where(mask, _dot_nt(ql[:, sl], kl[:, sl]), sm) for sl, sm in zip(heads, sms)]

    for h, (sl, sm) in enumerate(zip(heads, sms)):
        st = st_ref[h]
        vh = v[:, sl]
        o = _dot_nt(qe[:, sl], st.astype(BF16)) + _dot(sm.astype(BF16), vh.astype(BF16))
        o_ref[:, sl] = o
        st_ref[h] = st * dec[:, sl] + _dot(vh.T.astype(BF16), kd[:, sl])


def _hgrn_kernel(lbf_ref, lbb_ref, lvf_ref, lvb_ref, qf_ref, vf_ref, zf_ref, qb_ref, vb_ref, zb_ref,
                 of_ref, ob_ref, stf_ref, stb_ref):
    @pl.when(pl.program_id(0) == 0)
    def _():
        stf_ref[...] = jnp.zeros_like(stf_ref)
        stb_ref[...] = jnp.zeros_like(stb_ref)

    _hgrn_direction(qf_ref, vf_ref, zf_ref, lbf_ref[...], lvf_ref, stf_ref, of_ref, False)
    _hgrn_direction(qb_ref, vb_ref, zb_ref, lbb_ref[...], lvb_ref, stb_ref, ob_ref, True)


def _hgrn_level_tables(n):
    t, s = np.indices((n, n))
    x = t ^ s
    half = np.where(x > 0, 1 << (np.floor(np.log2(np.maximum(x, 1))).astype(np.int64)), 0)
    fwd = np.where(s <= t, half, -1).astype(np.int32)
    bwd = np.where(s >= t, half, -1).astype(np.int32)
    return jnp.asarray(fwd), jnp.asarray(bwd)


def _hgrn(h, lb_f, lb_b, mix_w):
    s = h.shape[0]
    n = HG_BLOCK
    nb = s // n
    hd = mix_w // A_HEADS
    fwd = lambda c: pl.BlockSpec((n, mix_w), lambda i: (i, c))
    bwd = lambda c: pl.BlockSpec((n, mix_w), lambda i: (nb - 1 - i, c))
    vec = pl.BlockSpec((1, mix_w), lambda i: (0, 0))
    lvs = pl.BlockSpec((n, n), lambda i: (0, 0))
    lv_f, lv_b = _hgrn_level_tables(n)
    return pl.pallas_call(
        _hgrn_kernel,
        out_shape=(jax.ShapeDtypeStruct((s, mix_w), F32), jax.ShapeDtypeStruct((s, mix_w), F32)),
        grid=(nb,),
        in_specs=[vec, vec, lvs, lvs, fwd(0), fwd(1), fwd(2), bwd(0), bwd(1), bwd(3)],
        out_specs=(pl.BlockSpec((n, mix_w), lambda i: (i, 0)),
                   pl.BlockSpec((n, mix_w), lambda i: (nb - 1 - i, 0))),
        scratch_shapes=[pltpu.VMEM((A_HEADS, hd, hd), F32), pltpu.VMEM((A_HEADS, hd, hd), F32)],
        compiler_params=_cparams(("arbitrary",)),
        name="hgrn_scan",
    )(lb_f, lb_b, lv_f, lv_b, h, h, h, h, h, h)


def _halo_ext(prev_ref, cur_ref, next_ref):
    i = pl.program_id(0)
    last = pl.num_programs(0) - 1
    prev = jnp.where(i == 0, 0.0, prev_ref[...])
    nxt = jnp.where(i == last, 0.0, next_ref[...])
    return jnp.concatenate([prev, cur_ref[...], nxt], axis=0)


def _even_tail_kernel(seq_len, alpha, of_ref, ob_ref, og_ref, up_ref, u_ref, un_ref, x_ref,
                      ng_ref, pw_ref, ps_ref, wo_ref, lg_ref, lb_ref, o_ref):
    t = x_ref.shape[0]
    mix_w = of_ref.shape[1]
    hd = mix_w // A_HEADS
    o = of_ref[...] + ob_ref[...]
    gate = _silu(og_ref[...])
    ng = ng_ref[...]
    ya = []
    for h in range(A_HEADS):
        sl = slice(h * hd, (h + 1) * hd)
        oh = o[:, sl]
        r = lax.rsqrt(jnp.mean(oh * oh, axis=-1, keepdims=True) + EPS)
        ya.append(oh * r * ng[:, sl] * gate[:, sl])
    ya = jnp.concatenate(ya, axis=-1).astype(BF16)

    ext = _halo_ext(up_ref, u_ref, un_ref)
    rows = ext.shape[0]
    gw = mix_w // len(POOL_WINDOWS)
    tpos = pl.program_id(0) * t + lax.broadcasted_iota(jnp.int32, (t, 1), 0)
    ps = ps_ref[...]
    yb = []
    for gi, win in enumerate(POOL_WINDOWS):
        sl = slice(gi * gw, (gi + 1) * gw)
        e = ext[:, sl]
        acc = e + pltpu.roll(e, 1, axis=0)
        span = 2
        while span < win:
            sh = span // 2
            acc = pltpu.roll(acc, rows - sh, axis=0) + pltpu.roll(acc, sh, axis=0)
            span *= 2
        wsum = acc[HALO:HALO + t, :]
        lo = jnp.maximum(tpos - win // 2, 0)
        hi = jnp.minimum(tpos - win // 2 + win - 1, seq_len - 1)
        cnt = (hi - lo + 1).astype(F32)
        d = wsum / cnt - e[HALO:HALO + t, :]
        yb.append(_dot(d.astype(BF16), pw_ref[gi]) * ps[:, sl])
    yb = jnp.concatenate(yb, axis=-1).astype(BF16)

    y = _dot(ya, wo_ref[0:mix_w, :]) + _dot(yb, wo_ref[mix_w:2 * mix_w, :])
    o_ref[...] = _layer_norm(alpha * x_ref[...] + y, lg_ref[...], lb_ref[...])


def _even_tail(o_f, o_b, h, x, norm_g, pool_w_bf, pool_scale, w_out_bf, ln_g, ln_b, alpha, tm):
    s, d_model = x.shape
    mix_w = o_f.shape[1]
    hb = tm // HALO
    nhalo = s // HALO
    row = lambda c: pl.BlockSpec((tm, mix_w), lambda i: (i, c))
    vec = lambda n: pl.BlockSpec((1, n), lambda i: (0, 0))
    return pl.pallas_call(
        functools.partial(_even_tail_kernel, s, alpha),
        out_shape=jax.ShapeDtypeStruct((s, d_model), F32),
        grid=(s // tm,),
        in_specs=[row(0), row(0), row(4),
                  pl.BlockSpec((HALO, mix_w), lambda i: (jnp.maximum(i * hb - 1, 0), 5)),
                  row(5),
                  pl.BlockSpec((HALO, mix_w), lambda i: (jnp.minimum((i + 1) * hb, nhalo - 1), 5)),
                  pl.BlockSpec((tm, d_model), lambda i: (i, 0)),
                  vec(mix_w),
                  pl.BlockSpec(pool_w_bf.shape, lambda i: (0, 0, 0)),
                  vec(mix_w),
                  pl.BlockSpec(w_out_bf.shape, lambda i: (0, 0)),
                  vec(d_model), vec(d_model)],
        out_specs=pl.BlockSpec((tm, d_model), lambda i: (i, 0)),
        compiler_params=_cparams(("parallel",)),
        name="even_tail",
    )(o_f, o_b, h, h, h, h, x, norm_g, pool_w_bf, pool_scale, w_out_bf, ln_g, ln_b)


def _rope(x, cos, sin_lo, sin_hi):
    n = x.shape[-1]
    return x * cos + pltpu.roll(x, n - 32, axis=1) * sin_lo + pltpu.roll(x, 32, axis=1) * sin_hi


def _odd_in_kernel(q_scale, x_ref, w_ref, qg_ref, kg_ref, cos_ref, sl_ref, sh_ref,
                   q_ref, k_ref, v_ref, u_ref):
    hd = cos_ref.shape[1]
    qw = q_ref.shape[0]
    kw = k_ref.shape[1]
    mix_w = u_ref.shape[1]
    h = _dot(x_ref[...].astype(BF16), w_ref[...])
    cos, s_lo, s_hi = cos_ref[...], sl_ref[...], sh_ref[...]

    def norm_rope(a, g):
        r = lax.rsqrt(jnp.mean(a * a, axis=-1, keepdims=True) + EPS)
        return _rope(a * r * g, cos, s_lo, s_hi)

    qg = qg_ref[...]
    kg = kg_ref[...]
    for i in range(qw // hd):
        q_ref[i * hd:(i + 1) * hd, :] = (norm_rope(h[:, i * hd:(i + 1) * hd], qg) * q_scale).T.astype(BF16)
    for i in range(kw // hd):
        k_ref[:, i * hd:(i + 1) * hd] = norm_rope(h[:, qw + i * hd:qw + (i + 1) * hd], kg).astype(BF16)
        v_ref[i * hd:(i + 1) * hd, :] = h[:, qw + kw + i * hd:qw + kw + (i + 1) * hd].T.astype(BF16)
    a0 = qw + 2 * kw
    u_ref[...] = h[:, a0:a0 + mix_w] * jax.nn.sigmoid(h[:, a0 + mix_w:a0 + 2 * mix_w])


def _odd_in(x, w_bf, q_g, k_g, cos, sin_lo, sin_hi, mix_w, kv_w, tm):
    s, d_model = x.shape
    hd = cos.shape[1]
    q_scale = hd ** -0.5 * math.log2(math.e)
    vec = pl.BlockSpec((1, hd), lambda i: (0, 0))
    tab = pl.BlockSpec((tm, hd), lambda i: (i, 0))
    return pl.pallas_call(
        functools.partial(_odd_in_kernel, q_scale),
        out_shape=(jax.ShapeDtypeStruct((mix_w, s), BF16), jax.ShapeDtypeStruct((s, kv_w), BF16),
                   jax.ShapeDtypeStruct((kv_w, s), BF16), jax.ShapeDtypeStruct((s, mix_w), F32)),
        grid=(s // tm,),
        in_specs=[pl.BlockSpec((tm, d_model), lambda i: (i, 0)),
                  pl.BlockSpec(w_bf.shape, lambda i: (0, 0)),
                  vec, vec, tab, tab, tab],
        out_specs=(pl.BlockSpec((mix_w, tm), lambda i: (0, i)), pl.BlockSpec((tm, kv_w), lambda i: (i, 0)),
                   pl.BlockSpec((kv_w, tm), lambda i: (0, i)), pl.BlockSpec((tm, mix_w), lambda i: (i, 0))),
        compiler_params=_cparams(("parallel",)),
        name="odd_in",
    )(x, w_bf, q_g, k_g, cos, sin_lo, sin_hi)


def _attn_kernel(tk, qt_ref, k_ref, vt_ref, o_ref, acc_ref, s_ref, p_ref):
    hd = k_ref.shape[1]
    grp = qt_ref.shape[0] // hd
    tq = qt_ref.shape[1]
    s_len = k_ref.shape[0]
    n = grp * tq
    qt = jnp.concatenate([qt_ref[g * hd:(g + 1) * hd, :] for g in range(grp)], axis=1)
    nt = s_len // tk

    def scores(t):
        off = pl.multiple_of(t * tk, tk)
        return _dot(k_ref[pl.ds(off, tk), :], qt)

    ones_rows = jnp.ones((ATTN_SUM_ROWS, tk), BF16)

    def weighted_values(t, slot):
        off = pl.multiple_of(t * tk, tk)
        vt = jnp.concatenate([vt_ref[:, pl.ds(off, tk)], ones_rows], axis=0)
        return _dot(vt, p_ref[slot])

    def step(t, cur, carry):
        m_old, a_prev = carry
        s_ref[1 - cur] = scores(jnp.minimum(t + 1, nt - 1))
        acc_ref[...] = a_prev * acc_ref[...] + weighted_values(jnp.maximum(t - 1, 0), 1 - cur)
        s = s_ref[cur]
        m_new = jnp.maximum(m_old, jnp.max(s, axis=0, keepdims=True))
        p_ref[cur] = jnp.exp2(s - m_new).astype(BF16)
        return m_new, jnp.exp2(m_old - m_new)

    def body(j, carry):
        return step(2 * j + 1, 1, step(2 * j, 0, carry))

    acc_ref[...] = jnp.zeros(acc_ref.shape, F32)
    p_ref[1] = jnp.zeros(p_ref.shape[1:], BF16)
    s_ref[0] = scores(0)
    init = (jnp.full((1, n), -jnp.inf, F32), jnp.ones((1, n), F32))
    _, a_last = lax.fori_loop(0, nt // 2, body, init)
    acc = a_last * acc_ref[...] + weighted_values(nt - 1, 1)
    out = acc[0:hd, :] / acc[hd:hd + 1, :]
    for g in range(grp):
        o_ref[:, g * hd:(g + 1) * hd] = out[:, g * tq:(g + 1) * tq].T.astype(o_ref.dtype)


def _attention(qt, k, vt, hd, tq, tk):
    qw, s = qt.shape
    kvh = k.shape[1] // hd
    gw = qw // kvh
    return pl.pallas_call(
        functools.partial(_attn_kernel, tk),
        out_shape=jax.ShapeDtypeStruct((s, qw), BF16),
        grid=(kvh, s // tq),
        in_specs=[pl.BlockSpec((gw, tq), lambda h, i: (h, i)),
                  pl.BlockSpec((s, hd), lambda h, i: (0, h)),
                  pl.BlockSpec((hd, s), lambda h, i: (h, 0))],
        out_specs=pl.BlockSpec((tq, gw), lambda h, i: (i, h)),
        scratch_shapes=[pltpu.VMEM((hd + ATTN_SUM_ROWS, gw // hd * tq), F32),
                        pltpu.VMEM((2, tk, gw // hd * tq), F32),
                        pltpu.VMEM((2, tk, gw // hd * tq), BF16)],
        compiler_params=_cparams(("parallel", "parallel")),
        name="gqa_attention",
    )(qt, k, vt)


def _odd_tail_kernel(alpha, yc_ref, up_ref, u_ref, un_ref, x_ref, cw_ref, cb_ref, cg_ref, cbeta_ref,
                     wo_ref, lg_ref, lb_ref, o_ref, ext_ref, conv_ref):
    t = x_ref.shape[0]
    mix_w = u_ref.shape[1]
    ext = _halo_ext(up_ref, u_ref, un_ref)
    rows = ext.shape[0]
    ext_ref[0] = ext
    for r in range(1, SUBLANES):
        ext_ref[r] = pltpu.roll(ext, rows - r, axis=0)
    base = HALO - CONV_W // 2
    cb = cb_ref[...]
    groups = CONV_ROWS // SUBLANES
    for c in range(t // CONV_ROWS):
        acc = jnp.zeros((groups, SUBLANES, mix_w), F32)
        for j in range(CONV_W):
            r = (base + j) % SUBLANES
            start = c * CONV_ROWS + (base + j - r)
            xw = ext_ref[r, start:start + CONV_ROWS, :].reshape(groups, SUBLANES, mix_w)
            acc = acc + xw * cw_ref[j]
        conv_ref[c * CONV_ROWS:(c + 1) * CONV_ROWS, :] = acc.reshape(CONV_ROWS, mix_w) + cb
    conv = conv_ref[...]
    yd = _silu(_layer_norm(conv, cg_ref[...], cbeta_ref[...])).astype(BF16)
    y = _dot(yc_ref[...], wo_ref[0:mix_w, :]) + _dot(yd, wo_ref[mix_w:2 * mix_w, :])
    o_ref[...] = _layer_norm(alpha * x_ref[...] + y, lg_ref[...], lb_ref[...])


def _odd_tail(y_c, u, x, conv_w, conv_b, conv_g, conv_beta, w_out_bf, ln_g, ln_b, alpha, tm):
    s, d_model = x.shape
    mix_w = u.shape[1]
    hb = tm // HALO
    nhalo = s // HALO
    row = pl.BlockSpec((tm, mix_w), lambda i: (i, 0))
    vec = lambda n: pl.BlockSpec((1, n), lambda i: (0, 0))
    return pl.pallas_call(
        functools.partial(_odd_tail_kernel, alpha),
        out_shape=jax.ShapeDtypeStruct((s, d_model), F32),
        grid=(s // tm,),
        in_specs=[row,
                  pl.BlockSpec((HALO, mix_w), lambda i: (jnp.maximum(i * hb - 1, 0), 0)),
                  row,
                  pl.BlockSpec((HALO, mix_w), lambda i: (jnp.minimum((i + 1) * hb, nhalo - 1), 0)),
                  pl.BlockSpec((tm, d_model), lambda i: (i, 0)),
                  pl.BlockSpec(conv_w.shape, lambda i: (0, 0, 0)),
                  vec(mix_w), vec(mix_w), vec(mix_w),
                  pl.BlockSpec(w_out_bf.shape, lambda i: (0, 0)),
                  vec(d_model), vec(d_model)],
        out_specs=pl.BlockSpec((tm, d_model), lambda i: (i, 0)),
        scratch_shapes=[pltpu.VMEM((SUBLANES, tm + 2 * HALO, mix_w), F32), pltpu.VMEM((tm, mix_w), F32)],
        compiler_params=_cparams(("parallel",)),
        name="odd_tail",
    )(y_c, u, u, u, x, conv_w, conv_b, conv_g, conv_beta, w_out_bf, ln_g, ln_b)


def _xattn_kernel(alpha, x_ref, wq_ref, k_ref, v_ref, wo_ref, lg_ref, lb_ref, o_ref):
    t, d_model = x_ref.shape
    hd = d_model // XA_HEADS
    for r0 in range(0, t, XA_SUB_ROWS):
        x = x_ref[r0:r0 + XA_SUB_ROWS, :]
        q = (_dot(x.astype(BF16), wq_ref[...]) * (hd ** -0.5)).astype(BF16)
        outs = []
        for h in range(XA_HEADS):
            sl = slice(h * hd, (h + 1) * hd)
            s = _dot_nt(q[:, sl], k_ref[:, sl])
            m = jnp.max(s, axis=-1, keepdims=True)
            p = jnp.exp(s - m)
            l = jnp.sum(p, axis=-1, keepdims=True)
            outs.append(_dot(p.astype(BF16), v_ref[:, sl]) / l)
        o = jnp.concatenate(outs, axis=-1).astype(BF16)
        y = _dot(o, wo_ref[...])
        o_ref[r0:r0 + XA_SUB_ROWS, :] = _layer_norm(alpha * x + y, lg_ref[...], lb_ref[...])


def _xattn(x, wq_bf, k_bf, v_bf, wo_bf, ln_g, ln_b, alpha, tm):
    s, d_model = x.shape
    full = lambda a: pl.BlockSpec(a.shape, lambda i: (0, 0))
    vec = pl.BlockSpec((1, d_model), lambda i: (0, 0))
    return pl.pallas_call(
        functools.partial(_xattn_kernel, alpha),
        out_shape=jax.ShapeDtypeStruct((s, d_model), F32),
        grid=(s // tm,),
        in_specs=[pl.BlockSpec((tm, d_model), lambda i: (i, 0)),
                  full(wq_bf), full(k_bf), full(v_bf), full(wo_bf), vec, vec],
        out_specs=pl.BlockSpec((tm, d_model), lambda i: (i, 0)),
        compiler_params=_cparams(("parallel",)),
        name="mem_xattn",
    )(x, wq_bf, k_bf, v_bf, wo_bf, ln_g, ln_b)


def _ffn_kernel(ck, alpha, x_ref, wg_ref, wu_ref, wd_ref, lg_ref, lb_ref, o_ref):
    d_ff = wg_ref.shape[1]
    x = x_ref[...]
    xb = x.astype(BF16)
    y = jnp.zeros(x.shape, F32)
    for c in range(d_ff // ck):
        sl = slice(c * ck, (c + 1) * ck)
        hcn = _silu(_dot(xb, wg_ref[:, sl])) * _dot(xb, wu_ref[:, sl])
        y = y + _dot(hcn.astype(BF16), wd_ref[sl, :])
    o_ref[...] = _layer_norm(alpha * x + y, lg_ref[...], lb_ref[...])


def _ffn(x, wg_bf, wu_bf, wd_bf, ln_g, ln_b, alpha, tm, ck):
    s, d_model = x.shape
    full = lambda a: pl.BlockSpec(a.shape, lambda i: (0, 0))
    vec = pl.BlockSpec((1, d_model), lambda i: (0, 0))
    return pl.pallas_call(
        functools.partial(_ffn_kernel, ck, alpha),
        out_shape=jax.ShapeDtypeStruct((s, d_model), F32),
        grid=(s // tm,),
        in_specs=[pl.BlockSpec((tm, d_model), lambda i: (i, 0)),
                  full(wg_bf), full(wu_bf), full(wd_bf), vec, vec],
        out_specs=pl.BlockSpec((tm, d_model), lambda i: (i, 0)),
        compiler_params=_cparams(("parallel",)),
        name="swiglu_ffn",
    )(x, wg_bf, wu_bf, wd_bf, ln_g, ln_b)


def _rope_tables(s, hd):
    rows = s // GRID_W
    row = np.repeat(np.arange(rows), GRID_W)
    col = np.tile(np.arange(GRID_W), rows)
    half = hd // 2
    freqs = ROPE_THETA ** (-np.arange(0, half, 2, dtype=np.float64) / half)

    def ang(p):
        a = p.astype(np.float64)[:, None] * freqs[None, :]
        return np.concatenate([a, a], axis=-1)

    angles = np.concatenate([ang(row), ang(col)], axis=-1)
    cos, sin = np.cos(angles), np.sin(angles)
    lo = (np.arange(hd) % (hd // 2)) < (hd // 4)
    as_f32 = lambda a: jnp.asarray(a.astype(np.float32))
    return as_f32(cos), as_f32(np.where(lo, -sin, 0.0)), as_f32(np.where(lo, 0.0, sin))


def kernel(x, mem, w_in_ab, hgrn_lb_logits, hgrn_norm_g, pool_w, pool_scale, w_out_ab, w_in_cd, q_norm_g, k_norm_g, conv_w, conv_b, conv_ln_g, conv_ln_b, w_out_cd, xa_wq, xa_wkv, xa_wo, ffn_w_gu, ffn_w_down, ln_g, ln_b):
    depth = xa_wq.shape[0]
    alpha = (2 * depth) ** 0.25
    bsz, s, d_model = x.shape
    mix_w = d_model // 2
    hd_c = mix_w // C_HEADS
    kv_w = C_KV_HEADS * hd_c
    d_ff = ffn_w_down.shape[1]
    tm = 512

    cum = jnp.cumsum(jax.nn.softmax(hgrn_lb_logits.astype(F32), axis=1), axis=1)
    lb = jnp.maximum(cum - cum[:, :1], 0.0)
    cos, sin_lo, sin_hi = _rope_tables(s, hd_c)
    row = lambda a: a.reshape(1, -1)

    outs = []
    for bi in range(bsz):
        xb = x[bi]
        memb = mem[bi]
        for l in range(depth):
            j = l // 2
            if l % 2 == 0:
                h = _proj(xb, w_in_ab[j].astype(BF16), tm, "even_in")
                o_f, o_b = _hgrn(h, row(lb[0, l]), row(lb[1, l]), mix_w)
                xb = _even_tail(o_f, o_b, h, xb, row(hgrn_norm_g[j]), pool_w[j].astype(BF16),
                                row(pool_scale[j]), w_out_ab[j].astype(BF16),
                                row(ln_g[l, 0]), row(ln_b[l, 0]), alpha, tm)
            else:
                q, k, v, u = _odd_in(xb, w_in_cd[j].astype(BF16), row(q_norm_g[j]), row(k_norm_g[j]),
                                     cos, sin_lo, sin_hi, mix_w, kv_w, tm)
                y_c = _attention(q, k, v, hd_c, min(512, s), min(512, s))
                taps = jnp.broadcast_to(conv_w[j][:, None, :], (conv_w.shape[1], SUBLANES, mix_w))
                xb = _odd_tail(y_c, u, xb, taps, row(conv_b[j]), row(conv_ln_g[j]),
                               row(conv_ln_b[j]), w_out_cd[j].astype(BF16),
                               row(ln_g[l, 0]), row(ln_b[l, 0]), alpha, tm)
            kv = _proj(memb, xa_wkv[l].astype(BF16), memb.shape[0], "mem_kv")
            xb = _xattn(xb, xa_wq[l].astype(BF16), kv[:, :d_model].astype(BF16),
                        kv[:, d_model:].astype(BF16), xa_wo[l].astype(BF16),
                        row(ln_g[l, 1]), row(ln_b[l, 1]), alpha, tm)
            xb = _ffn(xb, ffn_w_gu[l, :, :d_ff].astype(BF16), ffn_w_gu[l, :, d_ff:].astype(BF16),
                      ffn_w_down[l].astype(BF16), row(ln_g[l, 2]), row(ln_b[l, 2]), alpha, tm, 256)
        outs.append(xb)
    return jnp.stack(outs, axis=0)
```

```python
import functools
import math

import jax
import jax.numpy as jnp
import numpy as np
from jax import lax
from jax.experimental import pallas as pl
from jax.experimental.pallas import tpu as pltpu

F32 = jnp.float32
BF16 = jnp.bfloat16

A_HEADS = 4
POOL_WINDOWS = (2, 4, 8, 16)
C_HEADS = 4
C_KV_HEADS = 2
GRID_W = 64
ROPE_THETA = 10000.0
CONV_W = 31
XA_HEADS = 4
EPS = 1e-6

LANES = 128
SUBLANES = 8
VMEM_LIMIT = 56 * 1024 * 1024

HALO = 16
HG_BLOCK = 128
LOG2E = math.log2(math.e)
XA_SUB_ROWS = 256
CONV_ROWS = 32
ATTN_SUM_ROWS = 16


def _cparams(sem):
    return pltpu.CompilerParams(dimension_semantics=sem, vmem_limit_bytes=VMEM_LIMIT)


def _silu(x):
    return x * jax.nn.sigmoid(x)


def _layer_norm(y, g, b):
    mu = jnp.mean(y, axis=-1, keepdims=True)
    d = y - mu
    var = jnp.mean(d * d, axis=-1, keepdims=True)
    return d * lax.rsqrt(var + EPS) * g + b


def _dot(a, b):
    return jnp.dot(a, b, preferred_element_type=F32)


def _dot_nt(a, b):
    return lax.dot_general(a, b, (((1,), (1,)), ((), ())), preferred_element_type=F32)


def _layer_block(stack, layer, cols=None, col_block=0):
    shape = tuple(stack.shape[1:])
    if cols is not None:
        shape = shape[:-1] + (cols,)
    index = (layer,) + (0,) * (len(shape) - 1) + (col_block,)
    return pl.BlockSpec((None,) + shape, lambda *_: index)


def _proj_kernel(x_ref, w_ref, o_ref):
    o_ref[...] = _dot(x_ref[...].astype(BF16), w_ref[...]).astype(o_ref.dtype)


def _proj(x, w_stack, layer, tm, out_dtype, name):
    m, k = x.shape
    n = w_stack.shape[2]
    return pl.pallas_call(
        _proj_kernel,
        out_shape=jax.ShapeDtypeStruct((m, n), out_dtype),
        grid=(m // tm,),
        in_specs=[pl.BlockSpec((tm, k), lambda i: (i, 0)), _layer_block(w_stack, layer)],
        out_specs=pl.BlockSpec((tm, n), lambda i: (i, 0)),
        compiler_params=_cparams(("parallel",)),
        name=name,
    )(x, w_stack)


def _hgrn_direction(q_ref, v_ref, z_ref, lb, lv_ref, st_ref, o_ref, reverse):
    n = HG_BLOCK
    w = q_ref.shape[1]
    hd = w // A_HEADS
    z = z_ref[...]
    qs = _silu(q_ref[...])
    v = v_ref[...]
    k = (1.0 - lb) * jax.nn.sigmoid(-z)
    g = jnp.logaddexp(jnp.log(lb), jnp.log1p(-lb) + jax.nn.log_sigmoid(z))

    rowv = lax.broadcasted_iota(jnp.int32, (n, 1), 0)
    tri = jnp.where(lv_ref[...] >= 0, 1.0, 0.0).astype(BF16)
    g1 = g.astype(BF16)
    r1 = g - g1.astype(F32)
    g2 = r1.astype(BF16)
    g3 = (r1 - g2.astype(F32)).astype(BF16)
    yield
    b = _dot(tri, g1) + _dot(tri, g2) + _dot(tri, g3)
    yield

    end = 0 if reverse else n - 1
    b_end = b[end:end + 1, :]
    qe = (qs * jnp.exp(b)).astype(BF16)
    kd = (k * jnp.exp(b_end - b)).astype(BF16)
    dec = jnp.exp(b_end)
    yield

    lv = lv_ref[...]
    ng = n // SUBLANES
    b3 = b.reshape(ng, SUBLANES, w)
    sub3 = lax.broadcasted_iota(jnp.int32, (ng, SUBLANES, 1), 1)
    qs_bf = qs.astype(BF16)
    k_bf = k.astype(BF16)
    lvl = [(qs_bf, k_bf, 0)]
    half = n // 2
    while half >= 1:
        two = 2 * half
        off = half if reverse else half - 1
        if half >= SUBLANES:
            parts = [jnp.broadcast_to(b[blk * two + off:blk * two + off + 1, :], (two, w))
                     for blk in range(n // two)]
            bref = jnp.concatenate(parts, axis=0) if len(parts) > 1 else parts[0]
        else:
            bref3 = jnp.broadcast_to(b3[:, off:off + 1, :], (ng, SUBLANES, w))
            for blk in range(1, SUBLANES // two):
                r = blk * two + off
                bref3 = jnp.where(sub3 >= blk * two,
                                  jnp.broadcast_to(b3[:, r:r + 1, :], (ng, SUBLANES, w)), bref3)
            bref = bref3.reshape(n, w)
        second = (rowv % two) >= half
        q_side = jnp.logical_not(second) if reverse else second
        e = jnp.exp2((b - bref) * jnp.where(q_side, LOG2E, -LOG2E)).astype(BF16)
        lvl.append((qs_bf * e, k_bf * e, half))
        half //= 2
        yield

    heads = [slice(h * hd, (h + 1) * hd) for h in range(A_HEADS)]
    sms = [jnp.zeros((n, n), F32) for _ in heads]
    for ql, kl, level_id in lvl:
        mask = lv == level_id
        sms = [jnp.where(mask, _dot_nt(ql[:, sl], kl[:, sl]), sm) for sl, sm in zip(heads, sms)]
        yield

    for h, (sl, sm) in enumerate(zip(heads, sms)):
        st = st_ref[h]
        vh = v[:, sl]
        o = _dot_nt(qe[:, sl], st.astype(BF16)) + _dot(sm.astype(BF16), vh.astype(BF16))
        o_ref[:, sl] = o
        st_ref[h] = st * dec[:, sl] + _dot(vh.T.astype(BF16), kd[:, sl])
        yield


_DONE = object()


def _hgrn_kernel(lbf_ref, lbb_ref, lvf_ref, lvb_ref, qf_ref, vf_ref, zf_ref, qb_ref, vb_ref, zb_ref,
                 of_ref, ob_ref, stf_ref, stb_ref):
    @pl.when(pl.program_id(0) == 0)
    def _():
        stf_ref[...] = jnp.zeros_like(stf_ref)
        stb_ref[...] = jnp.zeros_like(stb_ref)

    pending = [_hgrn_direction(qf_ref, vf_ref, zf_ref, lbf_ref[...], lvf_ref, stf_ref, of_ref, False),
               _hgrn_direction(qb_ref, vb_ref, zb_ref, lbb_ref[...], lvb_ref, stb_ref, ob_ref, True)]
    while pending:
        pending = [d for d in pending if next(d, _DONE) is not _DONE]


def _hgrn_level_tables(n):
    t, s = np.indices((n, n))
    x = t ^ s
    half = np.where(x > 0, 1 << (np.floor(np.log2(np.maximum(x, 1))).astype(np.int64)), 0)
    fwd = np.where(s <= t, half, -1).astype(np.int32)
    bwd = np.where(s >= t, half, -1).astype(np.int32)
    return jnp.asarray(fwd), jnp.asarray(bwd)


def _hgrn(h, lb_f, lb_b, mix_w):
    s = h.shape[0]
    n = HG_BLOCK
    nb = s // n
    hd = mix_w // A_HEADS
    fwd = lambda c: pl.BlockSpec((n, mix_w), lambda i: (i, c))
    bwd = lambda c: pl.BlockSpec((n, mix_w), lambda i: (nb - 1 - i, c))
    vec = pl.BlockSpec((1, mix_w), lambda i: (0, 0))
    lvs = pl.BlockSpec((n, n), lambda i: (0, 0))
    lv_f, lv_b = _hgrn_level_tables(n)
    return pl.pallas_call(
        _hgrn_kernel,
        out_shape=(jax.ShapeDtypeStruct((s, mix_w), F32), jax.ShapeDtypeStruct((s, mix_w), F32)),
        grid=(nb,),
        in_specs=[vec, vec, lvs, lvs, fwd(0), fwd(1), fwd(2), bwd(0), bwd(1), bwd(3)],
        out_specs=(pl.BlockSpec((n, mix_w), lambda i: (i, 0)),
                   pl.BlockSpec((n, mix_w), lambda i: (nb - 1 - i, 0))),
        scratch_shapes=[pltpu.VMEM((A_HEADS, hd, hd), F32), pltpu.VMEM((A_HEADS, hd, hd), F32)],
        compiler_params=_cparams(("arbitrary",)),
        name="hgrn_scan",
    )(lb_f, lb_b, lv_f, lv_b, h, h, h, h, h, h)


def _halo_ext(prev_ref, cur_ref, next_ref):
    i = pl.program_id(0)
    last = pl.num_programs(0) - 1
    prev = jnp.where(i == 0, 0.0, prev_ref[...])
    nxt = jnp.where(i == last, 0.0, next_ref[...])
    return jnp.concatenate([prev, cur_ref[...], nxt], axis=0)


def _even_tail_kernel(seq_len, alpha, of_ref, ob_ref, og_ref, up_ref, u_ref, un_ref, x_ref,
                      ng_ref, pw_ref, ps_ref, wo_ref, lg_ref, lb_ref, o_ref):
    t = x_ref.shape[0]
    mix_w = of_ref.shape[1]
    hd = mix_w // A_HEADS
    o = of_ref[...] + ob_ref[...]
    gate = _silu(og_ref[...])
    ng = ng_ref[...]
    ya = []
    for h in range(A_HEADS):
        sl = slice(h * hd, (h + 1) * hd)
        oh = o[:, sl]
        r = lax.rsqrt(jnp.mean(oh * oh, axis=-1, keepdims=True) + EPS)
        ya.append(oh * r * ng[:, sl] * gate[:, sl])
    ya = jnp.concatenate(ya, axis=-1).astype(BF16)

    ext = _halo_ext(up_ref, u_ref, un_ref)
    rows = ext.shape[0]
    gw = mix_w // len(POOL_WINDOWS)
    tpos = pl.program_id(0) * t + lax.broadcasted_iota(jnp.int32, (t, 1), 0)
    ps = ps_ref[...]
    yb = []
    for gi, win in enumerate(POOL_WINDOWS):
        sl = slice(gi * gw, (gi + 1) * gw)
        e = ext[:, sl]
        acc = e + pltpu.roll(e, 1, axis=0)
        span = 2
        while span < win:
            sh = span // 2
            acc = pltpu.roll(acc, rows - sh, axis=0) + pltpu.roll(acc, sh, axis=0)
            span *= 2
        wsum = acc[HALO:HALO + t, :]
        lo = jnp.maximum(tpos - win // 2, 0)
        hi = jnp.minimum(tpos - win // 2 + win - 1, seq_len - 1)
        cnt = (hi - lo + 1).astype(F32)
        d = wsum / cnt - e[HALO:HALO + t, :]
        yb.append(_dot(d.astype(BF16), pw_ref[gi]) * ps[:, sl])
    yb = jnp.concatenate(yb, axis=-1).astype(BF16)

    y = _dot(ya, wo_ref[0:mix_w, :]) + _dot(yb, wo_ref[mix_w:2 * mix_w, :])
    o_ref[...] = _layer_norm(alpha * x_ref[...] + y, lg_ref[...], lb_ref[...])


def _even_tail(o_f, o_b, h, x, norm_g, pool_w_bf, pool_scale, w_out_bf, layer, ln_g, ln_b, alpha, tm):
    s, d_model = x.shape
    mix_w = o_f.shape[1]
    hb = tm // HALO
    nhalo = s // HALO
    row = lambda c: pl.BlockSpec((tm, mix_w), lambda i: (i, c))
    vec = lambda n: pl.BlockSpec((1, n), lambda i: (0, 0))
    return pl.pallas_call(
        functools.partial(_even_tail_kernel, s, alpha),
        out_shape=jax.ShapeDtypeStruct((s, d_model), F32),
        grid=(s // tm,),
        in_specs=[row(0), row(0), row(4),
                  pl.BlockSpec((HALO, mix_w), lambda i: (jnp.maximum(i * hb - 1, 0), 5)),
                  row(5),
                  pl.BlockSpec((HALO, mix_w), lambda i: (jnp.minimum((i + 1) * hb, nhalo - 1), 5)),
                  pl.BlockSpec((tm, d_model), lambda i: (i, 0)),
                  vec(mix_w),
                  _layer_block(pool_w_bf, layer),
                  vec(mix_w),
                  _layer_block(w_out_bf, layer),
                  vec(d_model), vec(d_model)],
        out_specs=pl.BlockSpec((tm, d_model), lambda i: (i, 0)),
        compiler_params=_cparams(("parallel",)),
        name="even_tail",
    )(o_f, o_b, h, h, h, h, x, norm_g, pool_w_bf, pool_scale, w_out_bf, ln_g, ln_b)


def _rope(x, cos, sin_lo, sin_hi):
    n = x.shape[-1]
    return x * cos + pltpu.roll(x, n - 32, axis=1) * sin_lo + pltpu.roll(x, 32, axis=1) * sin_hi


def _odd_in_kernel(q_scale, x_ref, w_ref, qg_ref, kg_ref, cos_ref, sl_ref, sh_ref,
                   q_ref, k_ref, v_ref, u_ref):
    hd = cos_ref.shape[1]
    qw = q_ref.shape[0]
    kw = k_ref.shape[1]
    mix_w = u_ref.shape[1]
    h = _dot(x_ref[...].astype(BF16), w_ref[...])
    cos, s_lo, s_hi = cos_ref[...], sl_ref[...], sh_ref[...]

    def norm_rope(a, g):
        r = lax.rsqrt(jnp.mean(a * a, axis=-1, keepdims=True) + EPS)
        return _rope(a * r * g, cos, s_lo, s_hi)

    qg = qg_ref[...]
    kg = kg_ref[...]
    for i in range(qw // hd):
        q_ref[i * hd:(i + 1) * hd, :] = (norm_rope(h[:, i * hd:(i + 1) * hd], qg) * q_scale).T.astype(BF16)
    for i in range(kw // hd):
        k_ref[:, i * hd:(i + 1) * hd] = norm_rope(h[:, qw + i * hd:qw + (i + 1) * hd], kg).astype(BF16)
        v_ref[i * hd:(i + 1) * hd, :] = h[:, qw + kw + i * hd:qw + kw + (i + 1) * hd].T.astype(BF16)
    a0 = qw + 2 * kw
    u_ref[...] = h[:, a0:a0 + mix_w] * jax.nn.sigmoid(h[:, a0 + mix_w:a0 + 2 * mix_w])


def _odd_in(x, w_bf, layer, q_g, k_g, cos, sin_lo, sin_hi, mix_w, kv_w, tm):
    s, d_model = x.shape
    hd = cos.shape[1]
    q_scale = hd ** -0.5 * math.log2(math.e)
    vec = pl.BlockSpec((1, hd), lambda i: (0, 0))
    tab = pl.BlockSpec((tm, hd), lambda i: (i, 0))
    return pl.pallas_call(
        functools.partial(_odd_in_kernel, q_scale),
        out_shape=(jax.ShapeDtypeStruct((mix_w, s), BF16), jax.ShapeDtypeStruct((s, kv_w), BF16),
                   jax.ShapeDtypeStruct((kv_w, s), BF16), jax.ShapeDtypeStruct((s, mix_w), F32)),
        grid=(s // tm,),
        in_specs=[pl.BlockSpec((tm, d_model), lambda i: (i, 0)),
                  _layer_block(w_bf, layer),
                  vec, vec, tab, tab, tab],
        out_specs=(pl.BlockSpec((mix_w, tm), lambda i: (0, i)), pl.BlockSpec((tm, kv_w), lambda i: (i, 0)),
                   pl.BlockSpec((kv_w, tm), lambda i: (0, i)), pl.BlockSpec((tm, mix_w), lambda i: (i, 0))),
        compiler_params=_cparams(("parallel",)),
        name="odd_in",
    )(x, w_bf, q_g, k_g, cos, sin_lo, sin_hi)


def _attn_kernel(tk, qt_ref, k_ref, vt_ref, o_ref, acc_ref, s_ref, p_ref):
    hd = k_ref.shape[1]
    grp = qt_ref.shape[0] // hd
    tq = qt_ref.shape[1]
    s_len = k_ref.shape[0]
    n = grp * tq
    qt = jnp.concatenate([qt_ref[g * hd:(g + 1) * hd, :] for g in range(grp)], axis=1)
    nt = s_len // tk

    def scores(t):
        off = pl.multiple_of(t * tk, tk)
        return _dot(k_ref[pl.ds(off, tk), :], qt)

    ones_rows = jnp.ones((ATTN_SUM_ROWS, tk), BF16)

    def weighted_values(t, slot):
        off = pl.multiple_of(t * tk, tk)
        vt = jnp.concatenate([vt_ref[:, pl.ds(off, tk)], ones_rows], axis=0)
        return _dot(vt, p_ref[slot])

    def step(t, cur, carry):
        m_old, a_prev, mx = carry
        s_next = scores(jnp.minimum(t + 1, nt - 1))
        s_ref[1 - cur] = s_next
        mx_next = jnp.max(s_next, axis=0, keepdims=True)
        acc_ref[...] = a_prev * acc_ref[...] + weighted_values(jnp.maximum(t - 1, 0), 1 - cur)
        m_new = jnp.maximum(m_old, mx)
        p_ref[cur] = jnp.exp2(s_ref[cur] - m_new).astype(BF16)
        return m_new, jnp.exp2(m_old - m_new), mx_next

    def body(j, carry):
        return step(2 * j + 1, 1, step(2 * j, 0, carry))

    acc_ref[...] = jnp.zeros(acc_ref.shape, F32)
    p_ref[1] = jnp.zeros(p_ref.shape[1:], BF16)
    s_first = scores(0)
    s_ref[0] = s_first
    init = (jnp.full((1, n), -jnp.inf, F32), jnp.ones((1, n), F32), jnp.max(s_first, axis=0, keepdims=True))
    _, a_last, _ = lax.fori_loop(0, nt // 2, body, init)
    acc = a_last * acc_ref[...] + weighted_values(nt - 1, 1)
    out = acc[0:hd, :] / acc[hd:hd + 1, :]
    for g in range(grp):
        o_ref[:, g * hd:(g + 1) * hd] = out[:, g * tq:(g + 1) * tq].T.astype(o_ref.dtype)


def _attention(qt, k, vt, hd, tq, tk):
    qw, s = qt.shape
    kvh = k.shape[1] // hd
    gw = qw // kvh
    return pl.pallas_call(
        functools.partial(_attn_kernel, tk),
        out_shape=jax.ShapeDtypeStruct((s, qw), BF16),
        grid=(kvh, s // tq),
        in_specs=[pl.BlockSpec((gw, tq), lambda h, i: (h, i)),
                  pl.BlockSpec((s, hd), lambda h, i: (0, h)),
                  pl.BlockSpec((hd, s), lambda h, i: (h, 0))],
        out_specs=pl.BlockSpec((tq, gw), lambda h, i: (i, h)),
        scratch_shapes=[pltpu.VMEM((hd + ATTN_SUM_ROWS, gw // hd * tq), F32),
                        pltpu.VMEM((2, tk, gw // hd * tq), F32),
                        pltpu.VMEM((2, tk, gw // hd * tq), BF16)],
        compiler_params=_cparams(("parallel", "parallel")),
        name="gqa_attention",
    )(qt, k, vt)


def _odd_tail_kernel(alpha, yc_ref, up_ref, u_ref, un_ref, x_ref, cw_ref, cb_ref, cg_ref, cbeta_ref,
                     wo_ref, lg_ref, lb_ref, o_ref, ext_ref, conv_ref):
    t = x_ref.shape[0]
    mix_w = u_ref.shape[1]
    ext = _halo_ext(up_ref, u_ref, un_ref)
    rows = ext.shape[0]
    ext_ref[0] = ext
    for r in range(1, SUBLANES):
        ext_ref[r] = pltpu.roll(ext, rows - r, axis=0)
    base = HALO - CONV_W // 2
    cb = cb_ref[...]
    groups = CONV_ROWS // SUBLANES
    for c in range(t // CONV_ROWS):
        acc = jnp.zeros((groups, SUBLANES, mix_w), F32)
        for j in range(CONV_W):
            r = (base + j) % SUBLANES
            start = c * CONV_ROWS + (base + j - r)
            xw = ext_ref[r, start:start + CONV_ROWS, :].reshape(groups, SUBLANES, mix_w)
            acc = acc + xw * cw_ref[j]
        conv_ref[c * CONV_ROWS:(c + 1) * CONV_ROWS, :] = acc.reshape(CONV_ROWS, mix_w) + cb
    conv = conv_ref[...]
    yd = _silu(_layer_norm(conv, cg_ref[...], cbeta_ref[...])).astype(BF16)
    y = _dot(yc_ref[...], wo_ref[0:mix_w, :]) + _dot(yd, wo_ref[mix_w:2 * mix_w, :])
    o_ref[...] = _layer_norm(alpha * x_ref[...] + y, lg_ref[...], lb_ref[...])


def _odd_tail(y_c, u, x, conv_w, conv_b, conv_g, conv_beta, w_out_bf, layer, ln_g, ln_b, alpha, tm):
    s, d_model = x.shape
    mix_w = u.shape[1]
    hb = tm // HALO
    nhalo = s // HALO
    row = pl.BlockSpec((tm, mix_w), lambda i: (i, 0))
    vec = lambda n: pl.BlockSpec((1, n), lambda i: (0, 0))
    return pl.pallas_call(
        functools.partial(_odd_tail_kernel, alpha),
        out_shape=jax.ShapeDtypeStruct((s, d_model), F32),
        grid=(s // tm,),
        in_specs=[row,
                  pl.BlockSpec((HALO, mix_w), lambda i: (jnp.maximum(i * hb - 1, 0), 0)),
                  row,
                  pl.BlockSpec((HALO, mix_w), lambda i: (jnp.minimum((i + 1) * hb, nhalo - 1), 0)),
                  pl.BlockSpec((tm, d_model), lambda i: (i, 0)),
                  pl.BlockSpec(conv_w.shape, lambda i: (0, 0, 0)),
                  vec(mix_w), vec(mix_w), vec(mix_w),
                  _layer_block(w_out_bf, layer),
                  vec(d_model), vec(d_model)],
        out_specs=pl.BlockSpec((tm, d_model), lambda i: (i, 0)),
        scratch_shapes=[pltpu.VMEM((SUBLANES, tm + 2 * HALO, mix_w), F32), pltpu.VMEM((tm, mix_w), F32)],
        compiler_params=_cparams(("parallel",)),
        name="odd_tail",
    )(y_c, u, u, u, x, conv_w, conv_b, conv_g, conv_beta, w_out_bf, ln_g, ln_b)


def _xattn_kernel(alpha, x_ref, wq_ref, k_ref, v_ref, wo_ref, lg_ref, lb_ref, o_ref):
    t, d_model = x_ref.shape
    hd = d_model // XA_HEADS
    for r0 in range(0, t, XA_SUB_ROWS):
        x = x_ref[r0:r0 + XA_SUB_ROWS, :]
        q = (_dot(x.astype(BF16), wq_ref[...]) * (hd ** -0.5)).astype(BF16)
        outs = []
        for h in range(XA_HEADS):
            sl = slice(h * hd, (h + 1) * hd)
            s = _dot_nt(q[:, sl], k_ref[:, sl])
            m = jnp.max(s, axis=-1, keepdims=True)
            p = jnp.exp(s - m)
            l = jnp.sum(p, axis=-1, keepdims=True)
            outs.append(_dot(p.astype(BF16), v_ref[:, sl]) / l)
        o = jnp.concatenate(outs, axis=-1).astype(BF16)
        y = _dot(o, wo_ref[...])
        o_ref[r0:r0 + XA_SUB_ROWS, :] = _layer_norm(alpha * x + y, lg_ref[...], lb_ref[...])


def _xattn(x, wq_bf, kv_bf, wo_bf, layer, ln_g, ln_b, alpha, tm):
    s, d_model = x.shape
    vec = pl.BlockSpec((1, d_model), lambda i: (0, 0))
    return pl.pallas_call(
        functools.partial(_xattn_kernel, alpha),
        out_shape=jax.ShapeDtypeStruct((s, d_model), F32),
        grid=(s // tm,),
        in_specs=[pl.BlockSpec((tm, d_model), lambda i: (i, 0)),
                  _layer_block(wq_bf, layer),
                  pl.BlockSpec((kv_bf.shape[0], d_model), lambda i: (0, 0)),
                  pl.BlockSpec((kv_bf.shape[0], d_model), lambda i: (0, 1)),
                  _layer_block(wo_bf, layer), vec, vec],
        out_specs=pl.BlockSpec((tm, d_model), lambda i: (i, 0)),
        compiler_params=_cparams(("parallel",)),
        name="mem_xattn",
    )(x, wq_bf, kv_bf, kv_bf, wo_bf, ln_g, ln_b)


def _ffn_kernel(ck, alpha, x_ref, wg_ref, wu_ref, wd_ref, lg_ref, lb_ref, o_ref):
    d_ff = wg_ref.shape[1]
    x = x_ref[...]
    xb = x.astype(BF16)
    y = jnp.zeros(x.shape, F32)
    for c in range(d_ff // ck):
        sl = slice(c * ck, (c + 1) * ck)
        hcn = _silu(_dot(xb, wg_ref[:, sl])) * _dot(xb, wu_ref[:, sl])
        y = y + _dot(hcn.astype(BF16), wd_ref[sl, :])
    o_ref[...] = _layer_norm(alpha * x + y, lg_ref[...], lb_ref[...])


def _ffn(x, wgu_bf, wd_bf, layer, ln_g, ln_b, alpha, tm, ck):
    s, d_model = x.shape
    d_ff = wd_bf.shape[1]
    vec = pl.BlockSpec((1, d_model), lambda i: (0, 0))
    return pl.pallas_call(
        functools.partial(_ffn_kernel, ck, alpha),
        out_shape=jax.ShapeDtypeStruct((s, d_model), F32),
        grid=(s // tm,),
        in_specs=[pl.BlockSpec((tm, d_model), lambda i: (i, 0)),
                  _layer_block(wgu_bf, layer, d_ff, 0), _layer_block(wgu_bf, layer, d_ff, 1),
                  _layer_block(wd_bf, layer), vec, vec],
        out_specs=pl.BlockSpec((tm, d_model), lambda i: (i, 0)),
        compiler_params=_cparams(("parallel",)),
        name="swiglu_ffn",
    )(x, wgu_bf, wgu_bf, wd_bf, ln_g, ln_b)


def _rope_tables(s, hd):
    rows = s // GRID_W
    row = np.repeat(np.arange(rows), GRID_W)
    col = np.tile(np.arange(GRID_W), rows)
    half = hd // 2
    freqs = ROPE_THETA ** (-np.arange(0, half, 2, dtype=np.float64) / half)

    def ang(p):
        a = p.astype(np.float64)[:, None] * freqs[None, :]
        return np.concatenate([a, a], axis=-1)

    angles = np.concatenate([ang(row), ang(col)], axis=-1)
    cos, sin = np.cos(angles), np.sin(angles)
    lo = (np.arange(hd) % (hd // 2)) < (hd // 4)
    as_f32 = lambda a: jnp.asarray(a.astype(np.float32))
    return as_f32(cos), as_f32(np.where(lo, -sin, 0.0)), as_f32(np.where(lo, 0.0, sin))


def kernel(x, mem, w_in_ab, hgrn_lb_logits, hgrn_norm_g, pool_w, pool_scale, w_out_ab, w_in_cd, q_norm_g, k_norm_g, conv_w, conv_b, conv_ln_g, conv_ln_b, w_out_cd, xa_wq, xa_wkv, xa_wo, ffn_w_gu, ffn_w_down, ln_g, ln_b):
    depth = xa_wq.shape[0]
    alpha = (2 * depth) ** 0.25
    bsz, s, d_model = x.shape
    mix_w = d_model // 2
    hd_c = mix_w // C_HEADS
    kv_w = C_KV_HEADS * hd_c
    tm = 512

    cum = jnp.cumsum(jax.nn.softmax(hgrn_lb_logits.astype(F32), axis=1), axis=1)
    lb = jnp.maximum(cum - cum[:, :1], 0.0)
    cos, sin_lo, sin_hi = _rope_tables(s, hd_c)
    row = lambda a: a.reshape(1, -1)

    bf = lambda a: a.astype(BF16)
    w_in_ab, pool_w, w_out_ab, w_in_cd, w_out_cd = map(bf, (w_in_ab, pool_w, w_out_ab, w_in_cd, w_out_cd))
    xa_wq, xa_wkv, xa_wo, ffn_w_gu, ffn_w_down = map(bf, (xa_wq, xa_wkv, xa_wo, ffn_w_gu, ffn_w_down))

    outs = []
    for bi in range(bsz):
        xb = x[bi]
        memb = mem[bi]
        for l in range(depth):
            j = l // 2
            if l % 2 == 0:
                h = _proj(xb, w_in_ab, j, tm, F32, "even_in")
                o_f, o_b = _hgrn(h, row(lb[0, l]), row(lb[1, l]), mix_w)
                xb = _even_tail(o_f, o_b, h, xb, row(hgrn_norm_g[j]), pool_w, row(pool_scale[j]),
                                w_out_ab, j, row(ln_g[l, 0]), row(ln_b[l, 0]), alpha, tm)
            else:
                q, k, v, u = _odd_in(xb, w_in_cd, j, row(q_norm_g[j]), row(k_norm_g[j]),
                                     cos, sin_lo, sin_hi, mix_w, kv_w, tm)
                y_c = _attention(q, k, v, hd_c, min(512, s), min(512, s))
                taps = jnp.broadcast_to(conv_w[j][:, None, :], (conv_w.shape[1], SUBLANES, mix_w))
                xb = _odd_tail(y_c, u, xb, taps, row(conv_b[j]), row(conv_ln_g[j]), row(conv_ln_b[j]),
                               w_out_cd, j, row(ln_g[l, 0]), row(ln_b[l, 0]), alpha, tm)
            kv = _proj(memb, xa_wkv, l, memb.shape[0], BF16, "mem_kv")
            xb = _xattn(xb, xa_wq, kv, xa_wo, l, row(ln_g[l, 1]), row(ln_b[l, 1]), alpha, tm)
            xb = _ffn(xb, ffn_w_gu, ffn_w_down, l, row(ln_g[l, 2]), row(ln_b[l, 2]), alpha, tm, 256)
        outs.append(xb)
    return jnp.stack(outs, axis=0)
```

```python
import functools
import math

import jax
import jax.numpy as jnp
import numpy as np
from jax import lax
from jax.experimental import pallas as pl
from jax.experimental.pallas import tpu as pltpu

F32 = jnp.float32
BF16 = jnp.bfloat16

A_HEADS = 4
POOL_WINDOWS = (2, 4, 8, 16)
C_HEADS = 4
C_KV_HEADS = 2
GRID_W = 64
ROPE_THETA = 10000.0
CONV_W = 31
XA_HEADS = 4
EPS = 1e-6

LANES = 128
SUBLANES = 8
VMEM_LIMIT = 56 * 1024 * 1024

HALO = 16
HG_BLOCK = 128
LOG2E = math.log2(math.e)
XA_SUB_ROWS = 256
CONV_ROWS = 32
ATTN_SUM_ROWS = 16


def _cparams(sem):
    return pltpu.CompilerParams(dimension_semantics=sem, vmem_limit_bytes=VMEM_LIMIT)


def _silu(x):
    return x * jax.nn.sigmoid(x)


def _layer_norm(y, g, b):
    mu = jnp.mean(y, axis=-1, keepdims=True)
    d = y - mu
    var = jnp.mean(d * d, axis=-1, keepdims=True)
    return d * lax.rsqrt(var + EPS) * g + b


def _dot(a, b):
    return jnp.dot(a, b, preferred_element_type=F32)


def _dot_nt(a, b):
    return lax.dot_general(a, b, (((1,), (1,)), ((), ())), preferred_element_type=F32)


def _layer_block(stack, layer, cols=None, col_block=0):
    shape = tuple(stack.shape[1:])
    if cols is not None:
        shape = shape[:-1] + (cols,)
    index = (layer,) + (0,) * (len(shape) - 1) + (col_block,)
    return pl.BlockSpec((None,) + shape, lambda *_: index)


def _proj_kernel(x_ref, w_ref, o_ref):
    o_ref[...] = _dot(x_ref[...].astype(BF16), w_ref[...]).astype(o_ref.dtype)


def _proj(x, w_stack, layer, tm, out_dtype, name):
    m, k = x.shape
    n = w_stack.shape[2]
    return pl.pallas_call(
        _proj_kernel,
        out_shape=jax.ShapeDtypeStruct((m, n), out_dtype),
        grid=(m // tm,),
        in_specs=[pl.BlockSpec((tm, k), lambda i: (i, 0)), _layer_block(w_stack, layer)],
        out_specs=pl.BlockSpec((tm, n), lambda i: (i, 0)),
        compiler_params=_cparams(("parallel",)),
        name=name,
    )(x, w_stack)


def _hgrn_direction(q_ref, v_ref, z_ref, lb, lv_ref, st_ref, o_ref, reverse):
    n = HG_BLOCK
    w = q_ref.shape[1]
    hd = w // A_HEADS
    z = z_ref[...]
    qs = _silu(q_ref[...])
    v = v_ref[...]
    t = jnp.exp(-jnp.abs(z))
    k = (1.0 - lb) * (jnp.where(z > 0, t, 1.0) / (1.0 + t))
    log_lb = jnp.log(lb)
    c = jnp.log1p(-lb) + (jnp.minimum(z, 0.0) - jnp.log1p(t))
    g = jnp.maximum(log_lb, c) + jnp.log1p(jnp.exp(-jnp.abs(log_lb - c)))

    rowv = lax.broadcasted_iota(jnp.int32, (n, 1), 0)
    tri = jnp.where(lv_ref[...] >= 0, 1.0, 0.0).astype(BF16)
    g1 = g.astype(BF16)
    r1 = g - g1.astype(F32)
    g2 = r1.astype(BF16)
    g3 = (r1 - g2.astype(F32)).astype(BF16)
    yield
    b = _dot(tri, g1) + _dot(tri, g2) + _dot(tri, g3)
    yield

    end = 0 if reverse else n - 1
    b_end = b[end:end + 1, :]
    qe = (qs * jnp.exp(b)).astype(BF16)
    kd = (k * jnp.exp(b_end - b)).astype(BF16)
    dec = jnp.exp(b_end)
    yield

    lv = lv_ref[...]
    ng = n // SUBLANES
    b3 = b.reshape(ng, SUBLANES, w)
    sub3 = lax.broadcasted_iota(jnp.int32, (ng, SUBLANES, 1), 1)
    qs_bf = qs.astype(BF16)
    k_bf = k.astype(BF16)
    lvl = [(qs_bf, k_bf, 0)]
    half = n // 2
    while half >= 1:
        two = 2 * half
        off = half if reverse else half - 1
        if half >= SUBLANES:
            parts = [jnp.broadcast_to(b[blk * two + off:blk * two + off + 1, :], (two, w))
                     for blk in range(n // two)]
            bref = jnp.concatenate(parts, axis=0) if len(parts) > 1 else parts[0]
        else:
            bref3 = jnp.broadcast_to(b3[:, off:off + 1, :], (ng, SUBLANES, w))
            for blk in range(1, SUBLANES // two):
                r = blk * two + off
                bref3 = jnp.where(sub3 >= blk * two,
                                  jnp.broadcast_to(b3[:, r:r + 1, :], (ng, SUBLANES, w)), bref3)
            bref = bref3.reshape(n, w)
        second = (rowv % two) >= half
        q_side = jnp.logical_not(second) if reverse else second
        e = jnp.exp2((b - bref) * jnp.where(q_side, LOG2E, -LOG2E)).astype(BF16)
        lvl.append((qs_bf * e, k_bf * e, half))
        half //= 2
        yield

    heads = [slice(h * hd, (h + 1) * hd) for h in range(A_HEADS)]
    sms = [jnp.zeros((n, n), F32) for _ in heads]
    for ql, kl, level_id in lvl:
        mask = lv == level_id
        sms = [jnp.where(mask, _dot_nt(ql[:, sl], kl[:, sl]), sm) for sl, sm in zip(heads, sms)]
        yield

    for h, (sl, sm) in enumerate(zip(heads, sms)):
        st = st_ref[h]
        vh = v[:, sl]
        o = _dot_nt(qe[:, sl], st.astype(BF16)) + _dot(sm.astype(BF16), vh.astype(BF16))
        o_ref[:, sl] = o
        st_ref[h] = st * dec[:, sl] + _dot(vh.T.astype(BF16), kd[:, sl])
        yield


_DONE = object()


def _hgrn_kernel(lbf_ref, lbb_ref, lvf_ref, lvb_ref, qf_ref, vf_ref, zf_ref, qb_ref, vb_ref, zb_ref,
                 of_ref, ob_ref, stf_ref, stb_ref):
    @pl.when(pl.program_id(0) == 0)
    def _():
        stf_ref[...] = jnp.zeros_like(stf_ref)
        stb_ref[...] = jnp.zeros_like(stb_ref)

    pending = [_hgrn_direction(qf_ref, vf_ref, zf_ref, lbf_ref[...], lvf_ref, stf_ref, of_ref, False),
               _hgrn_direction(qb_ref, vb_ref, zb_ref, lbb_ref[...], lvb_ref, stb_ref, ob_ref, True)]
    while pending:
        pending = [d for d in pending if next(d, _DONE) is not _DONE]


def _hgrn_level_tables(n):
    t, s = np.indices((n, n))
    x = t ^ s
    half = np.where(x > 0, 1 << (np.floor(np.log2(np.maximum(x, 1))).astype(np.int64)), 0)
    fwd = np.where(s <= t, half, -1).astype(np.int32)
    bwd = np.where(s >= t, half, -1).astype(np.int32)
    return jnp.asarray(fwd), jnp.asarray(bwd)


def _hgrn(h, lb_f, lb_b, mix_w):
    s = h.shape[0]
    n = HG_BLOCK
    nb = s // n
    hd = mix_w // A_HEADS
    fwd = lambda c: pl.BlockSpec((n, mix_w), lambda i: (i, c))
    bwd = lambda c: pl.BlockSpec((n, mix_w), lambda i: (nb - 1 - i, c))
    vec = pl.BlockSpec((1, mix_w), lambda i: (0, 0))
    lvs = pl.BlockSpec((n, n), lambda i: (0, 0))
    lv_f, lv_b = _hgrn_level_tables(n)
    return pl.pallas_call(
        _hgrn_kernel,
        out_shape=(jax.ShapeDtypeStruct((s, mix_w), F32), jax.ShapeDtypeStruct((s, mix_w), F32)),
        grid=(nb,),
        in_specs=[vec, vec, lvs, lvs, fwd(0), fwd(1), fwd(2), bwd(0), bwd(1), bwd(3)],
        out_specs=(pl.BlockSpec((n, mix_w), lambda i: (i, 0)),
                   pl.BlockSpec((n, mix_w), lambda i: (nb - 1 - i, 0))),
        scratch_shapes=[pltpu.VMEM((A_HEADS, hd, hd), F32), pltpu.VMEM((A_HEADS, hd, hd), F32)],
        compiler_params=_cparams(("arbitrary",)),
        name="hgrn_scan",
    )(lb_f, lb_b, lv_f, lv_b, h, h, h, h, h, h)


def _halo_ext(prev_ref, cur_ref, next_ref):
    i = pl.program_id(0)
    last = pl.num_programs(0) - 1
    prev = jnp.where(i == 0, 0.0, prev_ref[...])
    nxt = jnp.where(i == last, 0.0, next_ref[...])
    return jnp.concatenate([prev, cur_ref[...], nxt], axis=0)


def _even_tail_kernel(seq_len, alpha, of_ref, ob_ref, og_ref, up_ref, u_ref, un_ref, x_ref,
                      ng_ref, pw_ref, ps_ref, wo_ref, lg_ref, lb_ref, o_ref):
    t = x_ref.shape[0]
    mix_w = of_ref.shape[1]
    hd = mix_w // A_HEADS
    o = of_ref[...] + ob_ref[...]
    gate = _silu(og_ref[...])
    ng = ng_ref[...]
    ya = []
    for h in range(A_HEADS):
        sl = slice(h * hd, (h + 1) * hd)
        oh = o[:, sl]
        r = lax.rsqrt(jnp.mean(oh * oh, axis=-1, keepdims=True) + EPS)
        ya.append(oh * r * ng[:, sl] * gate[:, sl])
    ya = jnp.concatenate(ya, axis=-1).astype(BF16)

    ext = _halo_ext(up_ref, u_ref, un_ref)
    rows = ext.shape[0]
    gw = mix_w // len(POOL_WINDOWS)
    tpos = pl.program_id(0) * t + lax.broadcasted_iota(jnp.int32, (t, 1), 0)
    ps = ps_ref[...]
    yb = []
    for gi, win in enumerate(POOL_WINDOWS):
        sl = slice(gi * gw, (gi + 1) * gw)
        e = ext[:, sl]
        acc = e + pltpu.roll(e, 1, axis=0)
        span = 2
        while span < win:
            sh = span // 2
            acc = pltpu.roll(acc, rows - sh, axis=0) + pltpu.roll(acc, sh, axis=0)
            span *= 2
        wsum = acc[HALO:HALO + t, :]
        lo = jnp.maximum(tpos - win // 2, 0)
        hi = jnp.minimum(tpos - win // 2 + win - 1, seq_len - 1)
        cnt = (hi - lo + 1).astype(F32)
        d = wsum / cnt - e[HALO:HALO + t, :]
        yb.append(_dot(d.astype(BF16), pw_ref[gi]) * ps[:, sl])
    yb = jnp.concatenate(yb, axis=-1).astype(BF16)

    y = _dot(ya, wo_ref[0:mix_w, :]) + _dot(yb, wo_ref[mix_w:2 * mix_w, :])
    o_ref[...] = _layer_norm(alpha * x_ref[...] + y, lg_ref[...], lb_ref[...])


def _even_tail(o_f, o_b, h, x, norm_g, pool_w_bf, pool_scale, w_out_bf, layer, ln_g, ln_b, alpha, tm):
    s, d_model = x.shape
    mix_w = o_f.shape[1]
    hb = tm // HALO
    nhalo = s // HALO
    row = lambda c: pl.BlockSpec((tm, mix_w), lambda i: (i, c))
    vec = lambda n: pl.BlockSpec((1, n), lambda i: (0, 0))
    return pl.pallas_call(
        functools.partial(_even_tail_kernel, s, alpha),
        out_shape=jax.ShapeDtypeStruct((s, d_model), F32),
        grid=(s // tm,),
        in_specs=[row(0), row(0), row(4),
                  pl.BlockSpec((HALO, mix_w), lambda i: (jnp.maximum(i * hb - 1, 0), 5)),
                  row(5),
                  pl.BlockSpec((HALO, mix_w), lambda i: (jnp.minimum((i + 1) * hb, nhalo - 1), 5)),
                  pl.BlockSpec((tm, d_model), lambda i: (i, 0)),
                  vec(mix_w),
                  _layer_block(pool_w_bf, layer),
                  vec(mix_w),
                  _layer_block(w_out_bf, layer),
                  vec(d_model), vec(d_model)],
        out_specs=pl.BlockSpec((tm, d_model), lambda i: (i, 0)),
        compiler_params=_cparams(("parallel",)),
        name="even_tail",
    )(o_f, o_b, h, h, h, h, x, norm_g, pool_w_bf, pool_scale, w_out_bf, ln_g, ln_b)


def _rope(x, cos, sin_lo, sin_hi):
    n = x.shape[-1]
    return x * cos + pltpu.roll(x, n - 32, axis=1) * sin_lo + pltpu.roll(x, 32, axis=1) * sin_hi


def _odd_in_kernel(q_scale, x_ref, w_ref, qg_ref, kg_ref, cos_ref, sl_ref, sh_ref,
                   q_ref, k_ref, v_ref, u_ref):
    hd = cos_ref.shape[1]
    qw = q_ref.shape[0]
    kw = k_ref.shape[1]
    mix_w = u_ref.shape[1]
    h = _dot(x_ref[...].astype(BF16), w_ref[...])
    cos, s_lo, s_hi = cos_ref[...], sl_ref[...], sh_ref[...]

    def norm_rope(a, g):
        r = lax.rsqrt(jnp.mean(a * a, axis=-1, keepdims=True) + EPS)
        return _rope(a * r * g, cos, s_lo, s_hi)

    qg = qg_ref[...]
    kg = kg_ref[...]
    for i in range(qw // hd):
        q_ref[i * hd:(i + 1) * hd, :] = (norm_rope(h[:, i * hd:(i + 1) * hd], qg) * q_scale).T.astype(BF16)
    for i in range(kw // hd):
        k_ref[:, i * hd:(i + 1) * hd] = norm_rope(h[:, qw + i * hd:qw + (i + 1) * hd], kg).astype(BF16)
        v_ref[i * hd:(i + 1) * hd, :] = h[:, qw + kw + i * hd:qw + kw + (i + 1) * hd].T.astype(BF16)
    a0 = qw + 2 * kw
    u_ref[...] = h[:, a0:a0 + mix_w] * jax.nn.sigmoid(h[:, a0 + mix_w:a0 + 2 * mix_w])


def _odd_in(x, w_bf, layer, q_g, k_g, cos, sin_lo, sin_hi, mix_w, kv_w, tm):
    s, d_model = x.shape
    hd = cos.shape[1]
    q_scale = hd ** -0.5 * math.log2(math.e)
    vec = pl.BlockSpec((1, hd), lambda i: (0, 0))
    tab = pl.BlockSpec((tm, hd), lambda i: (i, 0))
    return pl.pallas_call(
        functools.partial(_odd_in_kernel, q_scale),
        out_shape=(jax.ShapeDtypeStruct((mix_w, s), BF16), jax.ShapeDtypeStruct((s, kv_w), BF16),
                   jax.ShapeDtypeStruct((kv_w, s), BF16), jax.ShapeDtypeStruct((s, mix_w), F32)),
        grid=(s // tm,),
        in_specs=[pl.BlockSpec((tm, d_model), lambda i: (i, 0)),
                  _layer_block(w_bf, layer),
                  vec, vec, tab, tab, tab],
        out_specs=(pl.BlockSpec((mix_w, tm), lambda i: (0, i)), pl.BlockSpec((tm, kv_w), lambda i: (i, 0)),
                   pl.BlockSpec((kv_w, tm), lambda i: (0, i)), pl.BlockSpec((tm, mix_w), lambda i: (i, 0))),
        compiler_params=_cparams(("parallel",)),
        name="odd_in",
    )(x, w_bf, q_g, k_g, cos, sin_lo, sin_hi)


def _attn_kernel(tk, qt_ref, k_ref, vt_ref, o_ref, acc_ref, s_ref, p_ref):
    hd = k_ref.shape[1]
    grp = qt_ref.shape[0] // hd
    tq = qt_ref.shape[1]
    s_len = k_ref.shape[0]
    n = grp * tq
    qt = jnp.concatenate([qt_ref[g * hd:(g + 1) * hd, :] for g in range(grp)], axis=1)
    nt = s_len // tk

    def scores(t):
        off = pl.multiple_of(t * tk, tk)
        return _dot(k_ref[pl.ds(off, tk), :], qt)

    ones_rows = jnp.ones((ATTN_SUM_ROWS, tk), BF16)

    def weighted_values(t, slot):
        off = pl.multiple_of(t * tk, tk)
        vt = jnp.concatenate([vt_ref[:, pl.ds(off, tk)], ones_rows], axis=0)
        return _dot(vt, p_ref[slot])

    def step(t, cur, carry):
        m_old, a_prev, mx = carry
        s_next = scores(jnp.minimum(t + 1, nt - 1))
        s_ref[1 - cur] = s_next
        mx_next = jnp.max(s_next, axis=0, keepdims=True)
        acc_ref[...] = a_prev * acc_ref[...] + weighted_values(jnp.maximum(t - 1, 0), 1 - cur)
        m_new = jnp.maximum(m_old, mx)
        p_ref[cur] = jnp.exp2(s_ref[cur] - m_new).astype(BF16)
        return m_new, jnp.exp2(m_old - m_new), mx_next

    def body(j, carry):
        return step(2 * j + 1, 1, step(2 * j, 0, carry))

    acc_ref[...] = jnp.zeros(acc_ref.shape, F32)
    p_ref[1] = jnp.zeros(p_ref.shape[1:], BF16)
    s_first = scores(0)
    s_ref[0] = s_first
    init = (jnp.full((1, n), -jnp.inf, F32), jnp.ones((1, n), F32), jnp.max(s_first, axis=0, keepdims=True))
    _, a_last, _ = lax.fori_loop(0, nt // 2, body, init)
    acc = a_last * acc_ref[...] + weighted_values(nt - 1, 1)
    out = acc[0:hd, :] / acc[hd:hd + 1, :]
    for g in range(grp):
        o_ref[:, g * hd:(g + 1) * hd] = out[:, g * tq:(g + 1) * tq].T.astype(o_ref.dtype)


def _attention(qt, k, vt, hd, tq, tk):
    qw, s = qt.shape
    kvh = k.shape[1] // hd
    gw = qw // kvh
    return pl.pallas_call(
        functools.partial(_attn_kernel, tk),
        out_shape=jax.ShapeDtypeStruct((s, qw), BF16),
        grid=(kvh, s // tq),
        in_specs=[pl.BlockSpec((gw, tq), lambda h, i: (h, i)),
                  pl.BlockSpec((s, hd), lambda h, i: (0, h)),
                  pl.BlockSpec((hd, s), lambda h, i: (h, 0))],
        out_specs=pl.BlockSpec((tq, gw), lambda h, i: (i, h)),
        scratch_shapes=[pltpu.VMEM((hd + ATTN_SUM_ROWS, gw // hd * tq), F32),
                        pltpu.VMEM((2, tk, gw // hd * tq), F32),
                        pltpu.VMEM((2, tk, gw // hd * tq), BF16)],
        compiler_params=_cparams(("parallel", "parallel")),
        name="gqa_attention",
    )(qt, k, vt)


def _odd_tail_kernel(alpha, yc_ref, up_ref, u_ref, un_ref, x_ref, cw_ref, cb_ref, cg_ref, cbeta_ref,
                     wo_ref, lg_ref, lb_ref, o_ref, ext_ref, conv_ref):
    t = x_ref.shape[0]
    mix_w = u_ref.shape[1]
    ext = _halo_ext(up_ref, u_ref, un_ref)
    rows = ext.shape[0]
    ext_ref[0] = ext
    for r in range(1, SUBLANES):
        ext_ref[r] = pltpu.roll(ext, rows - r, axis=0)
    base = HALO - CONV_W // 2
    cb = cb_ref[...]
    groups = CONV_ROWS // SUBLANES
    for c in range(t // CONV_ROWS):
        acc = jnp.zeros((groups, SUBLANES, mix_w), F32)
        for j in range(CONV_W):
            r = (base + j) % SUBLANES
            start = c * CONV_ROWS + (base + j - r)
            xw = ext_ref[r, start:start + CONV_ROWS, :].reshape(groups, SUBLANES, mix_w)
            acc = acc + xw * cw_ref[j]
        conv_ref[c * CONV_ROWS:(c + 1) * CONV_ROWS, :] = acc.reshape(CONV_ROWS, mix_w) + cb
    conv = conv_ref[...]
    yd = _silu(_layer_norm(conv, cg_ref[...], cbeta_ref[...])).astype(BF16)
    y = _dot(yc_ref[...], wo_ref[0:mix_w, :]) + _dot(yd, wo_ref[mix_w:2 * mix_w, :])
    o_ref[...] = _layer_norm(alpha * x_ref[...] + y, lg_ref[...], lb_ref[...])


def _odd_tail(y_c, u, x, conv_w, conv_b, conv_g, conv_beta, w_out_bf, layer, ln_g, ln_b, alpha, tm):
    s, d_model = x.shape
    mix_w = u.shape[1]
    hb = tm // HALO
    nhalo = s // HALO
    row = pl.BlockSpec((tm, mix_w), lambda i: (i, 0))
    vec = lambda n: pl.BlockSpec((1, n), lambda i: (0, 0))
    return pl.pallas_call(
        functools.partial(_odd_tail_kernel, alpha),
        out_shape=jax.ShapeDtypeStruct((s, d_model), F32),
        grid=(s // tm,),
        in_specs=[row,
                  pl.BlockSpec((HALO, mix_w), lambda i: (jnp.maximum(i * hb - 1, 0), 0)),
                  row,
                  pl.BlockSpec((HALO, mix_w), lambda i: (jnp.minimum((i + 1) * hb, nhalo - 1), 0)),
                  pl.BlockSpec((tm, d_model), lambda i: (i, 0)),
                  pl.BlockSpec(conv_w.shape, lambda i: (0, 0, 0)),
                  vec(mix_w), vec(mix_w), vec(mix_w),
                  _layer_block(w_out_bf, layer),
                  vec(d_model), vec(d_model)],
        out_specs=pl.BlockSpec((tm, d_model), lambda i: (i, 0)),
        scratch_shapes=[pltpu.VMEM((SUBLANES, tm + 2 * HALO, mix_w), F32), pltpu.VMEM((tm, mix_w), F32)],
        compiler_params=_cparams(("parallel",)),
        name="odd_tail",
    )(y_c, u, u, u, x, conv_w, conv_b, conv_g, conv_beta, w_out_bf, ln_g, ln_b)


def _xattn_kernel(alpha, x_ref, wq_ref, k_ref, v_ref, wo_ref, lg_ref, lb_ref, o_ref):
    t, d_model = x_ref.shape
    hd = d_model // XA_HEADS
    for r0 in range(0, t, XA_SUB_ROWS):
        x = x_ref[r0:r0 + XA_SUB_ROWS, :]
        q = (_dot(x.astype(BF16), wq_ref[...]) * (hd ** -0.5)).astype(BF16)
        outs = []
        for h in range(XA_HEADS):
            sl = slice(h * hd, (h + 1) * hd)
            s = _dot_nt(q[:, sl], k_ref[:, sl])
            m = jnp.max(s, axis=-1, keepdims=True)
            p = jnp.exp(s - m)
            l = jnp.sum(p, axis=-1, keepdims=True)
            outs.append(_dot(p.astype(BF16), v_ref[:, sl]) / l)
        o = jnp.concatenate(outs, axis=-1).astype(BF16)
        y = _dot(o, wo_ref[...])
        o_ref[r0:r0 + XA_SUB_ROWS, :] = _layer_norm(alpha * x + y, lg_ref[...], lb_ref[...])


def _xattn(x, wq_bf, kv_bf, wo_bf, layer, ln_g, ln_b, alpha, tm):
    s, d_model = x.shape
    vec = pl.BlockSpec((1, d_model), lambda i: (0, 0))
    return pl.pallas_call(
        functools.partial(_xattn_kernel, alpha),
        out_shape=jax.ShapeDtypeStruct((s, d_model), F32),
        grid=(s // tm,),
        in_specs=[pl.BlockSpec((tm, d_model), lambda i: (i, 0)),
                  _layer_block(wq_bf, layer),
                  pl.BlockSpec((kv_bf.shape[0], d_model), lambda i: (0, 0)),
                  pl.BlockSpec((kv_bf.shape[0], d_model), lambda i: (0, 1)),
                  _layer_block(wo_bf, layer), vec, vec],
        out_specs=pl.BlockSpec((tm, d_model), lambda i: (i, 0)),
        compiler_params=_cparams(("parallel",)),
        name="mem_xattn",
    )(x, wq_bf, kv_bf, kv_bf, wo_bf, ln_g, ln_b)


def _ffn_kernel(ck, alpha, x_ref, wg_ref, wu_ref, wd_ref, lg_ref, lb_ref, o_ref):
    d_ff = wg_ref.shape[1]
    x = x_ref[...]
    xb = x.astype(BF16)
    y = jnp.zeros(x.shape, F32)
    for c in range(d_ff // ck):
        sl = slice(c * ck, (c + 1) * ck)
        hcn = _silu(_dot(xb, wg_ref[:, sl])) * _dot(xb, wu_ref[:, sl])
        y = y + _dot(hcn.astype(BF16), wd_ref[sl, :])
    o_ref[...] = _layer_norm(alpha * x + y, lg_ref[...], lb_ref[...])


def _ffn(x, wgu_bf, wd_bf, layer, ln_g, ln_b, alpha, tm, ck):
    s, d_model = x.shape
    d_ff = wd_bf.shape[1]
    vec = pl.BlockSpec((1, d_model), lambda i: (0, 0))
    return pl.pallas_call(
        functools.partial(_ffn_kernel, ck, alpha),
        out_shape=jax.ShapeDtypeStruct((s, d_model), F32),
        grid=(s // tm,),
        in_specs=[pl.BlockSpec((tm, d_model), lambda i: (i, 0)),
                  _layer_block(wgu_bf, layer, d_ff, 0), _layer_block(wgu_bf, layer, d_ff, 1),
                  _layer_block(wd_bf, layer), vec, vec],
        out_specs=pl.BlockSpec((tm, d_model), lambda i: (i, 0)),
        compiler_params=_cparams(("parallel",)),
        name="swiglu_ffn",
    )(x, wgu_bf, wgu_bf, wd_bf, ln_g, ln_b)


def _rope_tables(s, hd):
    rows = s // GRID_W
    row = np.repeat(np.arange(rows), GRID_W)
    col = np.tile(np.arange(GRID_W), rows)
    half = hd // 2
    freqs = ROPE_THETA ** (-np.arange(0, half, 2, dtype=np.float64) / half)

    def ang(p):
        a = p.astype(np.float64)[:, None] * freqs[None, :]
        return np.concatenate([a, a], axis=-1)

    angles = np.concatenate([ang(row), ang(col)], axis=-1)
    cos, sin = np.cos(angles), np.sin(angles)
    lo = (np.arange(hd) % (hd // 2)) < (hd // 4)
    as_f32 = lambda a: jnp.asarray(a.astype(np.float32))
    return as_f32(cos), as_f32(np.where(lo, -sin, 0.0)), as_f32(np.where(lo, 0.0, sin))


def kernel(x, mem, w_in_ab, hgrn_lb_logits, hgrn_norm_g, pool_w, pool_scale, w_out_ab, w_in_cd, q_norm_g, k_norm_g, conv_w, conv_b, conv_ln_g, conv_ln_b, w_out_cd, xa_wq, xa_wkv, xa_wo, ffn_w_gu, ffn_w_down, ln_g, ln_b):
    depth = xa_wq.shape[0]
    alpha = (2 * depth) ** 0.25
    bsz, s, d_model = x.shape
    mix_w = d_model // 2
    hd_c = mix_w // C_HEADS
    kv_w = C_KV_HEADS * hd_c
    tm = 512

    cum = jnp.cumsum(jax.nn.softmax(hgrn_lb_logits.astype(F32), axis=1), axis=1)
    lb = jnp.maximum(cum - cum[:, :1], 0.0)
    cos, sin_lo, sin_hi = _rope_tables(s, hd_c)
    row = lambda a: a.reshape(1, -1)

    bf = lambda a: a.astype(BF16)
    w_in_ab, pool_w, w_out_ab, w_in_cd, w_out_cd = map(bf, (w_in_ab, pool_w, w_out_ab, w_in_cd, w_out_cd))
    xa_wq, xa_wkv, xa_wo, ffn_w_gu, ffn_w_down = map(bf, (xa_wq, xa_wkv, xa_wo, ffn_w_gu, ffn_w_down))

    outs = []
    for bi in range(bsz):
        xb = x[bi]
        memb = mem[bi]
        for l in range(depth):
            j = l // 2
            if l % 2 == 0:
                h = _proj(xb, w_in_ab, j, tm, F32, "even_in")
                o_f, o_b = _hgrn(h, row(lb[0, l]), row(lb[1, l]), mix_w)
                xb = _even_tail(o_f, o_b, h, xb, row(hgrn_norm_g[j]), pool_w, row(pool_scale[j]),
                                w_out_ab, j, row(ln_g[l, 0]), row(ln_b[l, 0]), alpha, tm)
            else:
                q, k, v, u = _odd_in(xb, w_in_cd, j, row(q_norm_g[j]), row(k_norm_g[j]),
                                     cos, sin_lo, sin_hi, mix_w, kv_w, tm)
                tq, tk = ((512, 1024), (1024, 512))[j % 2]
                y_c = _attention(q, k, v, hd_c, min(tq, s), min(tk, s))
                taps = jnp.broadcast_to(conv_w[j][:, None, :], (conv_w.shape[1], SUBLANES, mix_w))
                xb = _odd_tail(y_c, u, xb, taps, row(conv_b[j]), row(conv_ln_g[j]), row(conv_ln_b[j]),
                               w_out_cd, j, row(ln_g[l, 0]), row(ln_b[l, 0]), alpha, tm)
            kv = _proj(memb, xa_wkv, l, memb.shape[0], BF16, "mem_kv")
            xb = _xattn(xb, xa_wq, kv, xa_wo, l, row(ln_g[l, 1]), row(ln_b[l, 1]), alpha, tm)
            xb = _ffn(xb, ffn_w_gu, ffn_w_down, l, row(ln_g[l, 2]), row(ln_b[l, 2]), alpha, tm, 256)
        outs.append(xb)
    return jnp.stack(outs, axis=0)
```

```python
import functools
import math

import jax
import jax.numpy as jnp
import numpy as np
from jax import lax
from jax.experimental import pallas as pl
from jax.experimental.pallas import tpu as pltpu

F32 = jnp.float32
BF16 = jnp.bfloat16

A_HEADS = 4
POOL_WINDOWS = (2, 4, 8, 16)
C_HEADS = 4
C_KV_HEADS = 2
GRID_W = 64
ROPE_THETA = 10000.0
CONV_W = 31
XA_HEADS = 4
EPS = 1e-6
LOG2E = math.log2(math.e)

SUBLANES = 8
VMEM_LIMIT = 56 * 1024 * 1024

ROW_TILE = 512
HG_BLOCK = 128
ATTN_TQ = 2048
ATTN_TK = 512
FFN_CHUNK = 256
XA_SUB_ROWS = 256
CONV_ROWS = 32
HALO = 16
ATTN_SUM_ROWS = 16


def _cparams(sem):
    return pltpu.CompilerParams(dimension_semantics=sem, vmem_limit_bytes=VMEM_LIMIT)


def _silu(x):
    return x * jax.nn.sigmoid(x)


def _layer_norm(y, g, b):
    mu = jnp.mean(y, axis=-1, keepdims=True)
    d = y - mu
    var = jnp.mean(d * d, axis=-1, keepdims=True)
    return d * lax.rsqrt(var + EPS) * g + b


def _dot(a, b):
    return jnp.dot(a, b, preferred_element_type=F32)


def _dot_nt(a, b):
    return lax.dot_general(a, b, (((1,), (1,)), ((), ())), preferred_element_type=F32)


def _layer_block(stack, layer, cols=None, col_block=0):
    shape = tuple(stack.shape[1:])
    if cols is not None:
        shape = shape[:-1] + (cols,)
    index = (layer,) + (0,) * (len(shape) - 1) + (col_block,)
    return pl.BlockSpec((None,) + shape, lambda *_: index)


def _proj_kernel(x_ref, w_ref, o_ref):
    o_ref[...] = _dot(x_ref[...].astype(BF16), w_ref[...]).astype(o_ref.dtype)


def _proj(x, w_stack, layer, tm, out_dtype, name):
    m, k = x.shape
    n = w_stack.shape[2]
    return pl.pallas_call(
        _proj_kernel,
        out_shape=jax.ShapeDtypeStruct((m, n), out_dtype),
        grid=(m // tm,),
        in_specs=[pl.BlockSpec((tm, k), lambda i: (i, 0)), _layer_block(w_stack, layer)],
        out_specs=pl.BlockSpec((tm, n), lambda i: (i, 0)),
        compiler_params=_cparams(("parallel",)),
        name=name,
    )(x, w_stack)


def _hgrn_direction(q_ref, v_ref, z_ref, lb, lv_ref, st_ref, o_ref, reverse):
    n = HG_BLOCK
    w = q_ref.shape[1]
    hd = w // A_HEADS
    z = z_ref[...]
    qs = _silu(q_ref[...])
    v = v_ref[...]
    t = jnp.exp(-jnp.abs(z))
    k = (1.0 - lb) * (jnp.where(z > 0, t, 1.0) / (1.0 + t))
    log_lb = jnp.log(lb)
    c = jnp.log1p(-lb) + (jnp.minimum(z, 0.0) - jnp.log1p(t))
    g = jnp.maximum(log_lb, c) + jnp.log1p(jnp.exp(-jnp.abs(log_lb - c)))

    rowv = lax.broadcasted_iota(jnp.int32, (n, 1), 0)
    tri = jnp.where(lv_ref[...] >= 0, 1.0, 0.0).astype(BF16)
    g1 = g.astype(BF16)
    r1 = g - g1.astype(F32)
    g2 = r1.astype(BF16)
    g3 = (r1 - g2.astype(F32)).astype(BF16)
    yield
    b = _dot(tri, g1) + _dot(tri, g2) + _dot(tri, g3)
    yield

    end = 0 if reverse else n - 1
    b_end = b[end:end + 1, :]
    qe = (qs * jnp.exp(b)).astype(BF16)
    kd = (k * jnp.exp(b_end - b)).astype(BF16)
    dec = jnp.exp(b_end)
    yield

    lv = lv_ref[...]
    ng = n // SUBLANES
    b3 = b.reshape(ng, SUBLANES, w)
    sub3 = lax.broadcasted_iota(jnp.int32, (ng, SUBLANES, 1), 1)
    qs_bf = qs.astype(BF16)
    k_bf = k.astype(BF16)
    lvl = [(qs_bf, k_bf, 0)]
    half = n // 2
    while half >= 1:
        two = 2 * half
        off = half if reverse else half - 1
        if half >= SUBLANES:
            parts = [jnp.broadcast_to(b[blk * two + off:blk * two + off + 1, :], (two, w))
                     for blk in range(n // two)]
            bref = jnp.concatenate(parts, axis=0) if len(parts) > 1 else parts[0]
        else:
            bref3 = jnp.broadcast_to(b3[:, off:off + 1, :], (ng, SUBLANES, w))
            for blk in range(1, SUBLANES // two):
                r = blk * two + off
                bref3 = jnp.where(sub3 >= blk * two,
                                  jnp.broadcast_to(b3[:, r:r + 1, :], (ng, SUBLANES, w)), bref3)
            bref = bref3.reshape(n, w)
        second = (rowv % two) >= half
        q_side = jnp.logical_not(second) if reverse else second
        e = jnp.exp2((b - bref) * jnp.where(q_side, LOG2E, -LOG2E)).astype(BF16)
        lvl.append((qs_bf * e, k_bf * e, half))
        half //= 2
        yield

    heads = [slice(h * hd, (h + 1) * hd) for h in range(A_HEADS)]
    sms = [jnp.zeros((n, n), F32) for _ in heads]
    for ql, kl, level_id in lvl:
        mask = lv == level_id
        sms = [jnp.where(mask, _dot_nt(ql[:, sl], kl[:, sl]), sm) for sl, sm in zip(heads, sms)]
        yield

    for h, (sl, sm) in enumerate(zip(heads, sms)):
        st = st_ref[h]
        vh = v[:, sl]
        o = _dot_nt(qe[:, sl], st.astype(BF16)) + _dot(sm.astype(BF16), vh.astype(BF16))
        o_ref[:, sl] = o
        st_ref[h] = st * dec[:, sl] + _dot(vh.T.astype(BF16), kd[:, sl])
        yield


_DONE = object()


def _hgrn_kernel(lbf_ref, lbb_ref, lvf_ref, lvb_ref, qf_ref, vf_ref, zf_ref, qb_ref, vb_ref, zb_ref,
                 of_ref, ob_ref, stf_ref, stb_ref):
    @pl.when(pl.program_id(0) == 0)
    def _():
        stf_ref[...] = jnp.zeros_like(stf_ref)
        stb_ref[...] = jnp.zeros_like(stb_ref)

    pending = [_hgrn_direction(qf_ref, vf_ref, zf_ref, lbf_ref[...], lvf_ref, stf_ref, of_ref, False),
               _hgrn_direction(qb_ref, vb_ref, zb_ref, lbb_ref[...], lvb_ref, stb_ref, ob_ref, True)]
    while pending:
        pending = [d for d in pending if next(d, _DONE) is not _DONE]


def _hgrn_level_tables(n):
    t, s = np.indices((n, n))
    x = t ^ s
    half = np.where(x > 0, 1 << (np.floor(np.log2(np.maximum(x, 1))).astype(np.int64)), 0)
    fwd = np.where(s <= t, half, -1).astype(np.int32)
    bwd = np.where(s >= t, half, -1).astype(np.int32)
    return jnp.asarray(fwd), jnp.asarray(bwd)


def _hgrn(h, lb_f, lb_b, mix_w):
    s = h.shape[0]
    n = HG_BLOCK
    nb = s // n
    hd = mix_w // A_HEADS
    fwd = lambda c: pl.BlockSpec((n, mix_w), lambda i: (i, c))
    bwd = lambda c: pl.BlockSpec((n, mix_w), lambda i: (nb - 1 - i, c))
    vec = pl.BlockSpec((1, mix_w), lambda i: (0, 0))
    lvs = pl.BlockSpec((n, n), lambda i: (0, 0))
    lv_f, lv_b = _hgrn_level_tables(n)
    return pl.pallas_call(
        _hgrn_kernel,
        out_shape=(jax.ShapeDtypeStruct((s, mix_w), F32), jax.ShapeDtypeStruct((s, mix_w), F32)),
        grid=(nb,),
        in_specs=[vec, vec, lvs, lvs, fwd(0), fwd(1), fwd(2), bwd(0), bwd(1), bwd(3)],
        out_specs=(pl.BlockSpec((n, mix_w), lambda i: (i, 0)),
                   pl.BlockSpec((n, mix_w), lambda i: (nb - 1 - i, 0))),
        scratch_shapes=[pltpu.VMEM((A_HEADS, hd, hd), F32), pltpu.VMEM((A_HEADS, hd, hd), F32)],
        compiler_params=_cparams(("arbitrary",)),
        name="hgrn_scan",
    )(lb_f, lb_b, lv_f, lv_b, h, h, h, h, h, h)


def _halo_ext(prev_ref, cur_ref, next_ref):
    i = pl.program_id(0)
    last = pl.num_programs(0) - 1
    prev = jnp.where(i == 0, 0.0, prev_ref[...])
    nxt = jnp.where(i == last, 0.0, next_ref[...])
    return jnp.concatenate([prev, cur_ref[...], nxt], axis=0)


def _even_tail_kernel(seq_len, alpha, of_ref, ob_ref, og_ref, up_ref, u_ref, un_ref, x_ref,
                      ng_ref, pw_ref, ps_ref, wo_ref, lg_ref, lb_ref, o_ref):
    t = x_ref.shape[0]
    mix_w = of_ref.shape[1]
    hd = mix_w // A_HEADS
    o = of_ref[...] + ob_ref[...]
    gate = _silu(og_ref[...])
    ng = ng_ref[...]
    ya = []
    for h in range(A_HEADS):
        sl = slice(h * hd, (h + 1) * hd)
        oh = o[:, sl]
        r = lax.rsqrt(jnp.mean(oh * oh, axis=-1, keepdims=True) + EPS)
        ya.append(oh * r * ng[:, sl] * gate[:, sl])
    ya = jnp.concatenate(ya, axis=-1).astype(BF16)

    ext = _halo_ext(up_ref, u_ref, un_ref)
    rows = ext.shape[0]
    gw = mix_w // len(POOL_WINDOWS)
    tpos = pl.program_id(0) * t + lax.broadcasted_iota(jnp.int32, (t, 1), 0)
    ps = ps_ref[...]
    yb = []
    for gi, win in enumerate(POOL_WINDOWS):
        sl = slice(gi * gw, (gi + 1) * gw)
        e = ext[:, sl]
        acc = e + pltpu.roll(e, 1, axis=0)
        span = 2
        while span < win:
            sh = span // 2
            acc = pltpu.roll(acc, rows - sh, axis=0) + pltpu.roll(acc, sh, axis=0)
            span *= 2
        wsum = acc[HALO:HALO + t, :]
        lo = jnp.maximum(tpos - win // 2, 0)
        hi = jnp.minimum(tpos - win // 2 + win - 1, seq_len - 1)
        cnt = (hi - lo + 1).astype(F32)
        d = wsum / cnt - e[HALO:HALO + t, :]
        yb.append(_dot(d.astype(BF16), pw_ref[gi]) * ps[:, sl])
    yb = jnp.concatenate(yb, axis=-1).astype(BF16)

    y = _dot(ya, wo_ref[0:mix_w, :]) + _dot(yb, wo_ref[mix_w:2 * mix_w, :])
    o_ref[...] = _layer_norm(alpha * x_ref[...] + y, lg_ref[...], lb_ref[...])


def _even_tail(o_f, o_b, h, x, norm_g, pool_w_bf, pool_scale, w_out_bf, layer, ln_g, ln_b, alpha, tm):
    s, d_model = x.shape
    mix_w = o_f.shape[1]
    hb = tm // HALO
    nhalo = s // HALO
    row = lambda c: pl.BlockSpec((tm, mix_w), lambda i: (i, c))
    vec = lambda n: pl.BlockSpec((1, n), lambda i: (0, 0))
    return pl.pallas_call(
        functools.partial(_even_tail_kernel, s, alpha),
        out_shape=jax.ShapeDtypeStruct((s, d_model), F32),
        grid=(s // tm,),
        in_specs=[row(0), row(0), row(4),
                  pl.BlockSpec((HALO, mix_w), lambda i: (jnp.maximum(i * hb - 1, 0), 5)),
                  row(5),
                  pl.BlockSpec((HALO, mix_w), lambda i: (jnp.minimum((i + 1) * hb, nhalo - 1), 5)),
                  pl.BlockSpec((tm, d_model), lambda i: (i, 0)),
                  vec(mix_w),
                  _layer_block(pool_w_bf, layer),
                  vec(mix_w),
                  _layer_block(w_out_bf, layer),
                  vec(d_model), vec(d_model)],
        out_specs=pl.BlockSpec((tm, d_model), lambda i: (i, 0)),
        compiler_params=_cparams(("parallel",)),
        name="even_tail",
    )(o_f, o_b, h, h, h, h, x, norm_g, pool_w_bf, pool_scale, w_out_bf, ln_g, ln_b)


def _rope(x, cos, sin_lo, sin_hi):
    n = x.shape[-1]
    return x * cos + pltpu.roll(x, n - 32, axis=1) * sin_lo + pltpu.roll(x, 32, axis=1) * sin_hi


def _odd_in_kernel(q_scale, x_ref, w_ref, qg_ref, kg_ref, cos_ref, sl_ref, sh_ref,
                   q_ref, k_ref, v_ref, u_ref):
    hd = cos_ref.shape[1]
    qw = q_ref.shape[0]
    kw = k_ref.shape[1]
    mix_w = u_ref.shape[1]
    h = _dot(x_ref[...].astype(BF16), w_ref[...])
    cos, s_lo, s_hi = cos_ref[...], sl_ref[...], sh_ref[...]

    def norm_rope(a, g):
        r = lax.rsqrt(jnp.mean(a * a, axis=-1, keepdims=True) + EPS)
        return _rope(a * r * g, cos, s_lo, s_hi)

    qg = qg_ref[...]
    kg = kg_ref[...]
    for i in range(qw // hd):
        q_ref[i * hd:(i + 1) * hd, :] = (norm_rope(h[:, i * hd:(i + 1) * hd], qg) * q_scale).T.astype(BF16)
    for i in range(kw // hd):
        k_ref[:, i * hd:(i + 1) * hd] = norm_rope(h[:, qw + i * hd:qw + (i + 1) * hd], kg).astype(BF16)
        v_ref[i * hd:(i + 1) * hd, :] = h[:, qw + kw + i * hd:qw + kw + (i + 1) * hd].T.astype(BF16)
    a0 = qw + 2 * kw
    u_ref[...] = h[:, a0:a0 + mix_w] * jax.nn.sigmoid(h[:, a0 + mix_w:a0 + 2 * mix_w])


def _odd_in(x, w_bf, layer, q_g, k_g, cos, sin_lo, sin_hi, mix_w, kv_w, tm):
    s, d_model = x.shape
    hd = cos.shape[1]
    q_scale = hd ** -0.5 * LOG2E
    vec = pl.BlockSpec((1, hd), lambda i: (0, 0))
    tab = pl.BlockSpec((tm, hd), lambda i: (i, 0))
    return pl.pallas_call(
        functools.partial(_odd_in_kernel, q_scale),
        out_shape=(jax.ShapeDtypeStruct((mix_w, s), BF16), jax.ShapeDtypeStruct((s, kv_w), BF16),
                   jax.ShapeDtypeStruct((kv_w, s), BF16), jax.ShapeDtypeStruct((s, mix_w), F32)),
        grid=(s // tm,),
        in_specs=[pl.BlockSpec((tm, d_model), lambda i: (i, 0)),
                  _layer_block(w_bf, layer),
                  vec, vec, tab, tab, tab],
        out_specs=(pl.BlockSpec((mix_w, tm), lambda i: (0, i)), pl.BlockSpec((tm, kv_w), lambda i: (i, 0)),
                   pl.BlockSpec((kv_w, tm), lambda i: (0, i)), pl.BlockSpec((tm, mix_w), lambda i: (i, 0))),
        compiler_params=_cparams(("parallel",)),
        name="odd_in",
    )(x, w_bf, q_g, k_g, cos, sin_lo, sin_hi)


def _attn_kernel(tk, qt_ref, k_ref, vt_ref, o_ref, acc_ref, s_ref, p_ref):
    hd = k_ref.shape[1]
    grp = qt_ref.shape[0] // hd
    tq = qt_ref.shape[1]
    s_len = k_ref.shape[0]
    n = grp * tq
    qt = jnp.concatenate([qt_ref[g * hd:(g + 1) * hd, :] for g in range(grp)], axis=1)
    nt = s_len // tk

    def scores(t):
        off = pl.multiple_of(t * tk, tk)
        return _dot(k_ref[pl.ds(off, tk), :], qt)

    ones_rows = jnp.ones((ATTN_SUM_ROWS, tk), BF16)

    def weighted_values(t, slot):
        off = pl.multiple_of(t * tk, tk)
        vt = jnp.concatenate([vt_ref[:, pl.ds(off, tk)], ones_rows], axis=0)
        return _dot(vt, p_ref[slot])

    def step(t, cur, carry):
        m_old, a_prev, mx = carry
        s_next = scores(jnp.minimum(t + 1, nt - 1))
        s_ref[1 - cur] = s_next
        mx_next = jnp.max(s_next, axis=0, keepdims=True)
        acc_ref[...] = a_prev * acc_ref[...] + weighted_values(jnp.maximum(t - 1, 0), 1 - cur)
        m_new = jnp.maximum(m_old, mx)
        p_ref[cur] = jnp.exp2(s_ref[cur] - m_new).astype(BF16)
        return m_new, jnp.exp2(m_old - m_new), mx_next

    def body(j, carry):
        return step(2 * j + 1, 1, step(2 * j, 0, carry))

    acc_ref[...] = jnp.zeros(acc_ref.shape, F32)
    p_ref[1] = jnp.zeros(p_ref.shape[1:], BF16)
    s_first = scores(0)
    s_ref[0] = s_first
    init = (jnp.full((1, n), -jnp.inf, F32), jnp.ones((1, n), F32), jnp.max(s_first, axis=0, keepdims=True))
    _, a_last, _ = lax.fori_loop(0, nt // 2, body, init)
    acc = a_last * acc_ref[...] + weighted_values(nt - 1, 1)
    out = acc[0:hd, :] / acc[hd:hd + 1, :]
    for g in range(grp):
        o_ref[:, g * hd:(g + 1) * hd] = out[:, g * tq:(g + 1) * tq].T.astype(o_ref.dtype)


def _attention(qt, k, vt, hd, tq, tk):
    qw, s = qt.shape
    kvh = k.shape[1] // hd
    gw = qw // kvh
    return pl.pallas_call(
        functools.partial(_attn_kernel, tk),
        out_shape=jax.ShapeDtypeStruct((s, qw), BF16),
        grid=(kvh, s // tq),
        in_specs=[pl.BlockSpec((gw, tq), lambda h, i: (h, i)),
                  pl.BlockSpec((s, hd), lambda h, i: (0, h)),
                  pl.BlockSpec((hd, s), lambda h, i: (h, 0))],
        out_specs=pl.BlockSpec((tq, gw), lambda h, i: (i, h)),
        scratch_shapes=[pltpu.VMEM((hd + ATTN_SUM_ROWS, gw // hd * tq), F32),
                        pltpu.VMEM((2, tk, gw // hd * tq), F32),
                        pltpu.VMEM((2, tk, gw // hd * tq), BF16)],
        compiler_params=_cparams(("parallel", "parallel")),
        name="gqa_attention",
    )(qt, k, vt)


def _odd_tail_kernel(alpha, yc_ref, up_ref, u_ref, un_ref, x_ref, cw_ref, cb_ref, cg_ref, cbeta_ref,
                     wo_ref, lg_ref, lb_ref, o_ref, ext_ref, conv_ref):
    t = x_ref.shape[0]
    mix_w = u_ref.shape[1]
    ext = _halo_ext(up_ref, u_ref, un_ref)
    rows = ext.shape[0]
    ext_ref[0] = ext
    for r in range(1, SUBLANES):
        ext_ref[r] = pltpu.roll(ext, rows - r, axis=0)
    base = HALO - CONV_W // 2
    cb = cb_ref[...]
    groups = CONV_ROWS // SUBLANES
    for c in range(t // CONV_ROWS):
        acc = jnp.zeros((groups, SUBLANES, mix_w), F32)
        for j in range(CONV_W):
            r = (base + j) % SUBLANES
            start = c * CONV_ROWS + (base + j - r)
            xw = ext_ref[r, start:start + CONV_ROWS, :].reshape(groups, SUBLANES, mix_w)
            acc = acc + xw * cw_ref[j]
        conv_ref[c * CONV_ROWS:(c + 1) * CONV_ROWS, :] = acc.reshape(CONV_ROWS, mix_w) + cb
    conv = conv_ref[...]
    yd = _silu(_layer_norm(conv, cg_ref[...], cbeta_ref[...])).astype(BF16)
    y = _dot(yc_ref[...], wo_ref[0:mix_w, :]) + _dot(yd, wo_ref[mix_w:2 * mix_w, :])
    o_ref[...] = _layer_norm(alpha * x_ref[...] + y, lg_ref[...], lb_ref[...])


def _odd_tail(y_c, u, x, conv_w, conv_b, conv_g, conv_beta, w_out_bf, layer, ln_g, ln_b, alpha, tm):
    s, d_model = x.shape
    mix_w = u.shape[1]
    hb = tm // HALO
    nhalo = s // HALO
    row = pl.BlockSpec((tm, mix_w), lambda i: (i, 0))
    vec = lambda n: pl.BlockSpec((1, n), lambda i: (0, 0))
    return pl.pallas_call(
        functools.partial(_odd_tail_kernel, alpha),
        out_shape=jax.ShapeDtypeStruct((s, d_model), F32),
        grid=(s // tm,),
        in_specs=[row,
                  pl.BlockSpec((HALO, mix_w), lambda i: (jnp.maximum(i * hb - 1, 0), 0)),
                  row,
                  pl.BlockSpec((HALO, mix_w), lambda i: (jnp.minimum((i + 1) * hb, nhalo - 1), 0)),
                  pl.BlockSpec((tm, d_model), lambda i: (i, 0)),
                  pl.BlockSpec(conv_w.shape, lambda i: (0, 0, 0)),
                  vec(mix_w), vec(mix_w), vec(mix_w),
                  _layer_block(w_out_bf, layer),
                  vec(d_model), vec(d_model)],
        out_specs=pl.BlockSpec((tm, d_model), lambda i: (i, 0)),
        scratch_shapes=[pltpu.VMEM((SUBLANES, tm + 2 * HALO, mix_w), F32), pltpu.VMEM((tm, mix_w), F32)],
        compiler_params=_cparams(("parallel",)),
        name="odd_tail",
    )(y_c, u, u, u, x, conv_w, conv_b, conv_g, conv_beta, w_out_bf, ln_g, ln_b)


def _xattn_kernel(alpha, x_ref, wq_ref, k_ref, v_ref, wo_ref, lg_ref, lb_ref, o_ref):
    t, d_model = x_ref.shape
    hd = d_model // XA_HEADS
    for r0 in range(0, t, XA_SUB_ROWS):
        x = x_ref[r0:r0 + XA_SUB_ROWS, :]
        q = (_dot(x.astype(BF16), wq_ref[...]) * (hd ** -0.5)).astype(BF16)
        outs = []
        for h in range(XA_HEADS):
            sl = slice(h * hd, (h + 1) * hd)
            s = _dot_nt(q[:, sl], k_ref[:, sl])
            m = jnp.max(s, axis=-1, keepdims=True)
            p = jnp.exp(s - m)
            l = jnp.sum(p, axis=-1, keepdims=True)
            outs.append(_dot(p.astype(BF16), v_ref[:, sl]) / l)
        o = jnp.concatenate(outs, axis=-1).astype(BF16)
        y = _dot(o, wo_ref[...])
        o_ref[r0:r0 + XA_SUB_ROWS, :] = _layer_norm(alpha * x + y, lg_ref[...], lb_ref[...])


def _xattn(x, wq_bf, kv_bf, wo_bf, layer, ln_g, ln_b, alpha, tm):
    s, d_model = x.shape
    vec = pl.BlockSpec((1, d_model), lambda i: (0, 0))
    return pl.pallas_call(
        functools.partial(_xattn_kernel, alpha),
        out_shape=jax.ShapeDtypeStruct((s, d_model), F32),
        grid=(s // tm,),
        in_specs=[pl.BlockSpec((tm, d_model), lambda i: (i, 0)),
                  _layer_block(wq_bf, layer),
                  pl.BlockSpec((kv_bf.shape[0], d_model), lambda i: (0, 0)),
                  pl.BlockSpec((kv_bf.shape[0], d_model), lambda i: (0, 1)),
                  _layer_block(wo_bf, layer), vec, vec],
        out_specs=pl.BlockSpec((tm, d_model), lambda i: (i, 0)),
        compiler_params=_cparams(("parallel",)),
        name="mem_xattn",
    )(x, wq_bf, kv_bf, kv_bf, wo_bf, ln_g, ln_b)


def _ffn_kernel(ck, alpha, x_ref, wg_ref, wu_ref, wd_ref, lg_ref, lb_ref, o_ref):
    d_ff = wg_ref.shape[1]
    x = x_ref[...]
    xb = x.astype(BF16)
    y = jnp.zeros(x.shape, F32)
    for c in range(d_ff // ck):
        sl = slice(c * ck, (c + 1) * ck)
        hcn = _silu(_dot(xb, wg_ref[:, sl])) * _dot(xb, wu_ref[:, sl])
        y = y + _dot(hcn.astype(BF16), wd_ref[sl, :])
    o_ref[...] = _layer_norm(alpha * x + y, lg_ref[...], lb_ref[...])


def _ffn(x, wgu_bf, wd_bf, layer, ln_g, ln_b, alpha, tm, ck):
    s, d_model = x.shape
    d_ff = wd_bf.shape[1]
    assert d_ff % ck == 0, (d_ff, ck)
    vec = pl.BlockSpec((1, d_model), lambda i: (0, 0))
    return pl.pallas_call(
        functools.partial(_ffn_kernel, ck, alpha),
        out_shape=jax.ShapeDtypeStruct((s, d_model), F32),
        grid=(s // tm,),
        in_specs=[pl.BlockSpec((tm, d_model), lambda i: (i, 0)),
                  _layer_block(wgu_bf, layer, d_ff, 0), _layer_block(wgu_bf, layer, d_ff, 1),
                  _layer_block(wd_bf, layer), vec, vec],
        out_specs=pl.BlockSpec((tm, d_model), lambda i: (i, 0)),
        compiler_params=_cparams(("parallel",)),
        name="swiglu_ffn",
    )(x, wgu_bf, wgu_bf, wd_bf, ln_g, ln_b)


def _rope_tables(s, hd):
    rows = s // GRID_W
    row = np.repeat(np.arange(rows), GRID_W)
    col = np.tile(np.arange(GRID_W), rows)
    half = hd // 2
    freqs = ROPE_THETA ** (-np.arange(0, half, 2, dtype=np.float64) / half)

    def ang(p):
        a = p.astype(np.float64)[:, None] * freqs[None, :]
        return np.concatenate([a, a], axis=-1)

    angles = np.concatenate([ang(row), ang(col)], axis=-1)
    cos, sin = np.cos(angles), np.sin(angles)
    lo = (np.arange(hd) % (hd // 2)) < (hd // 4)
    as_f32 = lambda a: jnp.asarray(a.astype(np.float32))
    return as_f32(cos), as_f32(np.where(lo, -sin, 0.0)), as_f32(np.where(lo, 0.0, sin))


def kernel(x, mem, w_in_ab, hgrn_lb_logits, hgrn_norm_g, pool_w, pool_scale, w_out_ab, w_in_cd, q_norm_g, k_norm_g, conv_w, conv_b, conv_ln_g, conv_ln_b, w_out_cd, xa_wq, xa_wkv, xa_wo, ffn_w_gu, ffn_w_down, ln_g, ln_b):
    depth = xa_wq.shape[0]
    alpha = (2 * depth) ** 0.25
    bsz, s, d_model = x.shape
    mix_w = d_model // 2
    hd_c = mix_w // C_HEADS
    kv_w = C_KV_HEADS * hd_c
    tm = min(ROW_TILE, s)

    cum = jnp.cumsum(jax.nn.softmax(hgrn_lb_logits.astype(F32), axis=1), axis=1)
    lb = jnp.maximum(cum - cum[:, :1], 0.0)
    cos, sin_lo, sin_hi = _rope_tables(s, hd_c)
    row = lambda a: a.reshape(1, -1)

    bf = lambda a: a.astype(BF16)
    w_in_ab, pool_w, w_out_ab, w_in_cd, w_out_cd = map(bf, (w_in_ab, pool_w, w_out_ab, w_in_cd, w_out_cd))
    xa_wq, xa_wkv, xa_wo, ffn_w_gu, ffn_w_down = map(bf, (xa_wq, xa_wkv, xa_wo, ffn_w_gu, ffn_w_down))

    outs = []
    for bi in range(bsz):
        xb = x[bi]
        memb = mem[bi]
        for l in range(depth):
            j = l // 2
            if l % 2 == 0:
                h = _proj(xb, w_in_ab, j, tm, F32, "even_in")
                o_f, o_b = _hgrn(h, row(lb[0, l]), row(lb[1, l]), mix_w)
                xb = _even_tail(o_f, o_b, h, xb, row(hgrn_norm_g[j]), pool_w, row(pool_scale[j]),
                                w_out_ab, j, row(ln_g[l, 0]), row(ln_b[l, 0]), alpha, tm)
            else:
                q, k, v, u = _odd_in(xb, w_in_cd, j, row(q_norm_g[j]), row(k_norm_g[j]),
                                     cos, sin_lo, sin_hi, mix_w, kv_w, tm)
                y_c = _attention(q, k, v, hd_c, min(ATTN_TQ, s), min(ATTN_TK, s))
                taps = jnp.broadcast_to(conv_w[j][:, None, :], (conv_w.shape[1], SUBLANES, mix_w))
                xb = _odd_tail(y_c, u, xb, taps, row(conv_b[j]), row(conv_ln_g[j]), row(conv_ln_b[j]),
                               w_out_cd, j, row(ln_g[l, 0]), row(ln_b[l, 0]), alpha, tm)
            kv = _proj(memb, xa_wkv, l, memb.shape[0], BF16, "mem_kv")
            xb = _xattn(xb, xa_wq, kv, xa_wo, l, row(ln_g[l, 1]), row(ln_b[l, 1]), alpha, tm)
            xb = _ffn(xb, ffn_w_gu, ffn_w_down, l, row(ln_g[l, 2]), row(ln_b[l, 2]), alpha, tm, FFN_CHUNK)
        outs.append(xb)
    return jnp.stack(outs, axis=0)
```

```python
import functools
import math

import jax
import jax.numpy as jnp
import numpy as np
from jax import lax
from jax.experimental import pallas as pl
from jax.experimental.pallas import tpu as pltpu

F32 = jnp.float32
BF16 = jnp.bfloat16

A_HEADS = 4
POOL_WINDOWS = (2, 4, 8, 16)
C_HEADS = 4
C_KV_HEADS = 2
GRID_W = 64
ROPE_THETA = 10000.0
CONV_W = 31
XA_HEADS = 4
EPS = 1e-6
LOG2E = math.log2(math.e)

SUBLANES = 8
VMEM_LIMIT = 56 * 1024 * 1024

ROW_TILE = 512
HG_BLOCK = 128
ATTN_TQ = 2048
ATTN_TK = 512
FFN_CHUNK = 256
XA_ROW_TILE = 1024
CONV_ROWS = 32
HALO = 16
ATTN_SUM_ROWS = 16


def _cparams(sem):
    return pltpu.CompilerParams(dimension_semantics=sem, vmem_limit_bytes=VMEM_LIMIT)


def _silu(x):
    return x * jax.nn.sigmoid(x)


def _layer_norm(y, g, b):
    mu = jnp.mean(y, axis=-1, keepdims=True)
    d = y - mu
    var = jnp.mean(d * d, axis=-1, keepdims=True)
    return d * lax.rsqrt(var + EPS) * g + b


def _dot(a, b):
    return jnp.dot(a, b, preferred_element_type=F32)


def _w(w):
    return w.astype(BF16)


def _dot_nt(a, b):
    return lax.dot_general(a, b, (((1,), (1,)), ((), ())), preferred_element_type=F32)


def _layer_block(stack, layer, cols=None, col_block=0):
    shape = tuple(stack.shape[1:])
    if cols is not None:
        shape = shape[:-1] + (cols,)
    index = (layer,) + (0,) * (len(shape) - 1) + (col_block,)
    return pl.BlockSpec((None,) + shape, lambda *_: index, pipeline_mode=pl.Buffered(1))


def _proj_kernel(x_ref, w_ref, o_ref):
    o_ref[...] = _dot(x_ref[...].astype(BF16), _w(w_ref[...])).astype(o_ref.dtype)


def _proj(x, w_stack, layer, tm, out_dtype, name):
    m, k = x.shape
    n = w_stack.shape[2]
    return pl.pallas_call(
        _proj_kernel,
        out_shape=jax.ShapeDtypeStruct((m, n), out_dtype),
        grid=(m // tm,),
        in_specs=[pl.BlockSpec((tm, k), lambda i: (i, 0)), _layer_block(w_stack, layer)],
        out_specs=pl.BlockSpec((tm, n), lambda i: (i, 0)),
        compiler_params=_cparams(("parallel",)),
        name=name,
    )(x, w_stack)


def _hgrn_direction(q_ref, v_ref, z_ref, lb, lv_ref, st_ref, o_ref, reverse):
    n = HG_BLOCK
    w = q_ref.shape[1]
    hd = w // A_HEADS
    z = z_ref[...]
    qs = _silu(q_ref[...])
    v = v_ref[...]
    t = jnp.exp(-jnp.abs(z))
    k = (1.0 - lb) * (jnp.where(z > 0, t, 1.0) / (1.0 + t))
    log_lb = jnp.log(lb)
    c = jnp.log1p(-lb) + (jnp.minimum(z, 0.0) - jnp.log1p(t))
    g = jnp.maximum(log_lb, c) + jnp.log1p(jnp.exp(-jnp.abs(log_lb - c)))

    rowv = lax.broadcasted_iota(jnp.int32, (n, 1), 0)
    tri = jnp.where(lv_ref[...] >= 0, 1.0, 0.0).astype(BF16)
    g1 = g.astype(BF16)
    r1 = g - g1.astype(F32)
    g2 = r1.astype(BF16)
    g3 = (r1 - g2.astype(F32)).astype(BF16)
    yield
    b = _dot(tri, g1) + _dot(tri, g2) + _dot(tri, g3)
    yield

    end = 0 if reverse else n - 1
    b_end = b[end:end + 1, :]
    qe = (qs * jnp.exp(b)).astype(BF16)
    kd = (k * jnp.exp(b_end - b)).astype(BF16)
    dec = jnp.exp(b_end)
    yield

    lv = lv_ref[...]
    ng = n // SUBLANES
    b3 = b.reshape(ng, SUBLANES, w)
    sub3 = lax.broadcasted_iota(jnp.int32, (ng, SUBLANES, 1), 1)
    qs_bf = qs.astype(BF16)
    k_bf = k.astype(BF16)
    lvl = [(qs_bf, k_bf, 0)]
    half = n // 2
    while half >= 1:
        two = 2 * half
        off = half if reverse else half - 1
        if half >= SUBLANES:
            parts = [jnp.broadcast_to(b[blk * two + off:blk * two + off + 1, :], (two, w))
                     for blk in range(n // two)]
            bref = jnp.concatenate(parts, axis=0) if len(parts) > 1 else parts[0]
        else:
            bref3 = jnp.broadcast_to(b3[:, off:off + 1, :], (ng, SUBLANES, w))
            for blk in range(1, SUBLANES // two):
                r = blk * two + off
                bref3 = jnp.where(sub3 >= blk * two,
                                  jnp.broadcast_to(b3[:, r:r + 1, :], (ng, SUBLANES, w)), bref3)
            bref = bref3.reshape(n, w)
        second = (rowv % two) >= half
        q_side = jnp.logical_not(second) if reverse else second
        e = jnp.exp2((b - bref) * jnp.where(q_side, LOG2E, -LOG2E)).astype(BF16)
        lvl.append((qs_bf * e, k_bf * e, half))
        half //= 2
        yield

    heads = [slice(h * hd, (h + 1) * hd) for h in range(A_HEADS)]
    sms = [jnp.zeros((n, n), F32) for _ in heads]
    for ql, kl, level_id in lvl:
        mask = lv == level_id
        sms = [jnp.where(mask, _dot_nt(ql[:, sl], kl[:, sl]), sm) for sl, sm in zip(heads, sms)]
        yield

    for h, (sl, sm) in enumerate(zip(heads, sms)):
        st = st_ref[h]
        vh = v[:, sl]
        o = _dot_nt(qe[:, sl], st.astype(BF16)) + _dot(sm.astype(BF16), vh.astype(BF16))
        o_ref[:, sl] = o
        st_ref[h] = st * dec[:, sl] + _dot(vh.T.astype(BF16), kd[:, sl])
        yield


_DONE = object()


def _hgrn_kernel(lbf_ref, lbb_ref, lvf_ref, lvb_ref, qf_ref, vf_ref, zf_ref, qb_ref, vb_ref, zb_ref,
                 of_ref, ob_ref, stf_ref, stb_ref):
    @pl.when(pl.program_id(0) == 0)
    def _():
        stf_ref[...] = jnp.zeros_like(stf_ref)
        stb_ref[...] = jnp.zeros_like(stb_ref)

    pending = [_hgrn_direction(qf_ref, vf_ref, zf_ref, lbf_ref[...], lvf_ref, stf_ref, of_ref, False),
               _hgrn_direction(qb_ref, vb_ref, zb_ref, lbb_ref[...], lvb_ref, stb_ref, ob_ref, True)]
    while pending:
        pending = [d for d in pending if next(d, _DONE) is not _DONE]


def _hgrn_level_tables(n):
    t, s = np.indices((n, n))
    x = t ^ s
    half = np.where(x > 0, 1 << (np.floor(np.log2(np.maximum(x, 1))).astype(np.int64)), 0)
    fwd = np.where(s <= t, half, -1).astype(np.int32)
    bwd = np.where(s >= t, half, -1).astype(np.int32)
    return jnp.asarray(fwd), jnp.asarray(bwd)


def _hgrn(h, lb_f, lb_b, mix_w):
    s = h.shape[0]
    n = HG_BLOCK
    nb = s // n
    hd = mix_w // A_HEADS
    fwd = lambda c: pl.BlockSpec((n, mix_w), lambda i: (i, c))
    bwd = lambda c: pl.BlockSpec((n, mix_w), lambda i: (nb - 1 - i, c))
    vec = pl.BlockSpec((1, mix_w), lambda i: (0, 0))
    lvs = pl.BlockSpec((n, n), lambda i: (0, 0))
    lv_f, lv_b = _hgrn_level_tables(n)
    return pl.pallas_call(
        _hgrn_kernel,
        out_shape=(jax.ShapeDtypeStruct((s, mix_w), F32), jax.ShapeDtypeStruct((s, mix_w), F32)),
        grid=(nb,),
        in_specs=[vec, vec, lvs, lvs, fwd(0), fwd(1), fwd(2), bwd(0), bwd(1), bwd(3)],
        out_specs=(pl.BlockSpec((n, mix_w), lambda i: (i, 0)),
                   pl.BlockSpec((n, mix_w), lambda i: (nb - 1 - i, 0))),
        scratch_shapes=[pltpu.VMEM((A_HEADS, hd, hd), F32), pltpu.VMEM((A_HEADS, hd, hd), F32)],
        compiler_params=_cparams(("arbitrary",)),
        name="hgrn_scan",
    )(lb_f, lb_b, lv_f, lv_b, h, h, h, h, h, h)


def _halo_ext(prev_ref, cur_ref, next_ref):
    i = pl.program_id(0)
    last = pl.num_programs(0) - 1
    prev = jnp.where(i == 0, 0.0, prev_ref[...])
    nxt = jnp.where(i == last, 0.0, next_ref[...])
    return jnp.concatenate([prev, cur_ref[...], nxt], axis=0)


def _even_tail_kernel(seq_len, alpha, of_ref, ob_ref, og_ref, up_ref, u_ref, un_ref, x_ref,
                      ng_ref, pw_ref, ps_ref, wo_ref, lg_ref, lb_ref, o_ref):
    t = x_ref.shape[0]
    mix_w = of_ref.shape[1]
    hd = mix_w // A_HEADS
    o = of_ref[...] + ob_ref[...]
    gate = _silu(og_ref[...])
    ng = ng_ref[...]
    ya = []
    for h in range(A_HEADS):
        sl = slice(h * hd, (h + 1) * hd)
        oh = o[:, sl]
        r = lax.rsqrt(jnp.mean(oh * oh, axis=-1, keepdims=True) + EPS)
        ya.append(oh * r * ng[:, sl] * gate[:, sl])
    ya = jnp.concatenate(ya, axis=-1).astype(BF16)

    ext = _halo_ext(up_ref, u_ref, un_ref)
    rows = ext.shape[0]
    gw = mix_w // len(POOL_WINDOWS)
    tpos = pl.program_id(0) * t + lax.broadcasted_iota(jnp.int32, (t, 1), 0)
    ps = ps_ref[...]
    yb = []
    for gi, win in enumerate(POOL_WINDOWS):
        sl = slice(gi * gw, (gi + 1) * gw)
        e = ext[:, sl]
        acc = e + pltpu.roll(e, 1, axis=0)
        span = 2
        while span < win:
            sh = span // 2
            acc = pltpu.roll(acc, rows - sh, axis=0) + pltpu.roll(acc, sh, axis=0)
            span *= 2
        wsum = acc[HALO:HALO + t, :]
        lo = jnp.maximum(tpos - win // 2, 0)
        hi = jnp.minimum(tpos - win // 2 + win - 1, seq_len - 1)
        cnt = (hi - lo + 1).astype(F32)
        d = wsum / cnt - e[HALO:HALO + t, :]
        yb.append(_dot(d.astype(BF16), _w(pw_ref[gi])) * ps[:, sl])
    yb = jnp.concatenate(yb, axis=-1).astype(BF16)

    y = _dot(ya, _w(wo_ref[0:mix_w, :])) + _dot(yb, _w(wo_ref[mix_w:2 * mix_w, :]))
    o_ref[...] = _layer_norm(alpha * x_ref[...] + y, lg_ref[...], lb_ref[...])


def _even_tail(o_f, o_b, h, x, norm_g, pool_w, pool_scale, w_out, layer, ln_g, ln_b, alpha, tm):
    s, d_model = x.shape
    mix_w = o_f.shape[1]
    hb = tm // HALO
    nhalo = s // HALO
    row = lambda c: pl.BlockSpec((tm, mix_w), lambda i: (i, c))
    vec = lambda n: pl.BlockSpec((1, n), lambda i: (0, 0))
    return pl.pallas_call(
        functools.partial(_even_tail_kernel, s, alpha),
        out_shape=jax.ShapeDtypeStruct((s, d_model), F32),
        grid=(s // tm,),
        in_specs=[row(0), row(0), row(4),
                  pl.BlockSpec((HALO, mix_w), lambda i: (jnp.maximum(i * hb - 1, 0), 5)),
                  row(5),
                  pl.BlockSpec((HALO, mix_w), lambda i: (jnp.minimum((i + 1) * hb, nhalo - 1), 5)),
                  pl.BlockSpec((tm, d_model), lambda i: (i, 0)),
                  vec(mix_w),
                  _layer_block(pool_w, layer),
                  vec(mix_w),
                  _layer_block(w_out, layer),
                  vec(d_model), vec(d_model)],
        out_specs=pl.BlockSpec((tm, d_model), lambda i: (i, 0)),
        compiler_params=_cparams(("parallel",)),
        name="even_tail",
    )(o_f, o_b, h, h, h, h, x, norm_g, pool_w, pool_scale, w_out, ln_g, ln_b)


def _rope(x, cos, sin_lo, sin_hi):
    n = x.shape[-1]
    return x * cos + pltpu.roll(x, n - 32, axis=1) * sin_lo + pltpu.roll(x, 32, axis=1) * sin_hi


def _odd_in_kernel(q_scale, x_ref, w_ref, qg_ref, kg_ref, cos_ref, sl_ref, sh_ref,
                   q_ref, k_ref, v_ref, u_ref):
    hd = cos_ref.shape[1]
    qw = q_ref.shape[0]
    kw = k_ref.shape[1]
    mix_w = u_ref.shape[1]
    h = _dot(x_ref[...].astype(BF16), _w(w_ref[...]))
    cos, s_lo, s_hi = cos_ref[...], sl_ref[...], sh_ref[...]

    def norm_rope(a, g):
        r = lax.rsqrt(jnp.mean(a * a, axis=-1, keepdims=True) + EPS)
        return _rope(a * r * g, cos, s_lo, s_hi)

    qg = qg_ref[...]
    kg = kg_ref[...]
    for i in range(qw // hd):
        q_ref[i * hd:(i + 1) * hd, :] = (norm_rope(h[:, i * hd:(i + 1) * hd], qg) * q_scale).T.astype(BF16)
    for i in range(kw // hd):
        k_ref[:, i * hd:(i + 1) * hd] = norm_rope(h[:, qw + i * hd:qw + (i + 1) * hd], kg).astype(BF16)
        v_ref[i * hd:(i + 1) * hd, :] = h[:, qw + kw + i * hd:qw + kw + (i + 1) * hd].T.astype(BF16)
    a0 = qw + 2 * kw
    u_ref[...] = h[:, a0:a0 + mix_w] * jax.nn.sigmoid(h[:, a0 + mix_w:a0 + 2 * mix_w])


def _odd_in(x, w_in, layer, q_g, k_g, cos, sin_lo, sin_hi, mix_w, kv_w, tm):
    s, d_model = x.shape
    hd = cos.shape[1]
    q_scale = hd ** -0.5 * LOG2E
    vec = pl.BlockSpec((1, hd), lambda i: (0, 0))
    tab = pl.BlockSpec((tm, hd), lambda i: (i, 0))
    return pl.pallas_call(
        functools.partial(_odd_in_kernel, q_scale),
        out_shape=(jax.ShapeDtypeStruct((mix_w, s), BF16), jax.ShapeDtypeStruct((s, kv_w), BF16),
                   jax.ShapeDtypeStruct((kv_w, s), BF16), jax.ShapeDtypeStruct((s, mix_w), F32)),
        grid=(s // tm,),
        in_specs=[pl.BlockSpec((tm, d_model), lambda i: (i, 0)),
                  _layer_block(w_in, layer),
                  vec, vec, tab, tab, tab],
        out_specs=(pl.BlockSpec((mix_w, tm), lambda i: (0, i)), pl.BlockSpec((tm, kv_w), lambda i: (i, 0)),
                   pl.BlockSpec((kv_w, tm), lambda i: (0, i)), pl.BlockSpec((tm, mix_w), lambda i: (i, 0))),
        compiler_params=_cparams(("parallel",)),
        name="odd_in",
    )(x, w_in, q_g, k_g, cos, sin_lo, sin_hi)


def _attn_kernel(tk, qt_ref, k_ref, vt_ref, o_ref, acc_ref, s_ref, p_ref):
    hd = k_ref.shape[1]
    grp = qt_ref.shape[0] // hd
    tq = qt_ref.shape[1]
    s_len = k_ref.shape[0]
    n = grp * tq
    qt = jnp.concatenate([qt_ref[g * hd:(g + 1) * hd, :] for g in range(grp)], axis=1)
    nt = s_len // tk

    def scores(t):
        off = pl.multiple_of(t * tk, tk)
        return _dot(k_ref[pl.ds(off, tk), :], qt)

    ones_rows = jnp.ones((ATTN_SUM_ROWS, tk), BF16)

    def weighted_values(t, slot):
        off = pl.multiple_of(t * tk, tk)
        vt = jnp.concatenate([vt_ref[:, pl.ds(off, tk)], ones_rows], axis=0)
        return _dot(vt, p_ref[slot])

    def step(t, cur, carry):
        m_old, a_prev, mx = carry
        s_next = scores(jnp.minimum(t + 1, nt - 1))
        s_ref[1 - cur] = s_next
        mx_next = jnp.max(s_next, axis=0, keepdims=True)
        acc_ref[...] = a_prev * acc_ref[...] + weighted_values(jnp.maximum(t - 1, 0), 1 - cur)
        m_new = jnp.maximum(m_old, mx)
        p_ref[cur] = jnp.exp2(s_ref[cur] - m_new).astype(BF16)
        return m_new, jnp.exp2(m_old - m_new), mx_next

    def body(j, carry):
        return step(2 * j + 1, 1, step(2 * j, 0, carry))

    acc_ref[...] = jnp.zeros(acc_ref.shape, F32)
    p_ref[1] = jnp.zeros(p_ref.shape[1:], BF16)
    s_first = scores(0)
    s_ref[0] = s_first
    init = (jnp.full((1, n), -jnp.inf, F32), jnp.ones((1, n), F32), jnp.max(s_first, axis=0, keepdims=True))
    _, a_last, _ = lax.fori_loop(0, nt // 2, body, init)
    acc = a_last * acc_ref[...] + weighted_values(nt - 1, 1)
    out = acc[0:hd, :] / acc[hd:hd + 1, :]
    for g in range(grp):
        o_ref[:, g * hd:(g + 1) * hd] = out[:, g * tq:(g + 1) * tq].T.astype(o_ref.dtype)


def _attention(qt, k, vt, hd, tq, tk):
    qw, s = qt.shape
    kvh = k.shape[1] // hd
    gw = qw // kvh
    return pl.pallas_call(
        functools.partial(_attn_kernel, tk),
        out_shape=jax.ShapeDtypeStruct((s, qw), BF16),
        grid=(kvh, s // tq),
        in_specs=[pl.BlockSpec((gw, tq), lambda h, i: (h, i)),
                  pl.BlockSpec((s, hd), lambda h, i: (0, h)),
                  pl.BlockSpec((hd, s), lambda h, i: (h, 0))],
        out_specs=pl.BlockSpec((tq, gw), lambda h, i: (i, h)),
        scratch_shapes=[pltpu.VMEM((hd + ATTN_SUM_ROWS, gw // hd * tq), F32),
                        pltpu.VMEM((2, tk, gw // hd * tq), F32),
                        pltpu.VMEM((2, tk, gw // hd * tq), BF16)],
        compiler_params=_cparams(("parallel", "parallel")),
        name="gqa_attention",
    )(qt, k, vt)


def _odd_tail_kernel(alpha, yc_ref, up_ref, u_ref, un_ref, x_ref, cw_ref, cb_ref, cg_ref, cbeta_ref,
                     wo_ref, lg_ref, lb_ref, o_ref, ext_ref, conv_ref):
    t = x_ref.shape[0]
    mix_w = u_ref.shape[1]
    ext = _halo_ext(up_ref, u_ref, un_ref)
    rows = ext.shape[0]
    ext_ref[0] = ext
    for r in range(1, SUBLANES):
        ext_ref[r] = pltpu.roll(ext, rows - r, axis=0)
    base = HALO - CONV_W // 2
    cb = cb_ref[...]
    groups = CONV_ROWS // SUBLANES
    for c in range(t // CONV_ROWS):
        acc = jnp.zeros((groups, SUBLANES, mix_w), F32)
        for j in range(CONV_W):
            r = (base + j) % SUBLANES
            start = c * CONV_ROWS + (base + j - r)
            xw = ext_ref[r, start:start + CONV_ROWS, :].reshape(groups, SUBLANES, mix_w)
            acc = acc + xw * cw_ref[j]
        conv_ref[c * CONV_ROWS:(c + 1) * CONV_ROWS, :] = acc.reshape(CONV_ROWS, mix_w) + cb
    conv = conv_ref[...]
    yd = _silu(_layer_norm(conv, cg_ref[...], cbeta_ref[...])).astype(BF16)
    y = _dot(yc_ref[...], _w(wo_ref[0:mix_w, :])) + _dot(yd, _w(wo_ref[mix_w:2 * mix_w, :]))
    o_ref[...] = _layer_norm(alpha * x_ref[...] + y, lg_ref[...], lb_ref[...])


def _odd_tail(y_c, u, x, conv_w, conv_b, conv_g, conv_beta, w_out, layer, ln_g, ln_b, alpha, tm):
    s, d_model = x.shape
    mix_w = u.shape[1]
    hb = tm // HALO
    nhalo = s // HALO
    row = pl.BlockSpec((tm, mix_w), lambda i: (i, 0))
    vec = lambda n: pl.BlockSpec((1, n), lambda i: (0, 0))
    return pl.pallas_call(
        functools.partial(_odd_tail_kernel, alpha),
        out_shape=jax.ShapeDtypeStruct((s, d_model), F32),
        grid=(s // tm,),
        in_specs=[row,
                  pl.BlockSpec((HALO, mix_w), lambda i: (jnp.maximum(i * hb - 1, 0), 0)),
                  row,
                  pl.BlockSpec((HALO, mix_w), lambda i: (jnp.minimum((i + 1) * hb, nhalo - 1), 0)),
                  pl.BlockSpec((tm, d_model), lambda i: (i, 0)),
                  pl.BlockSpec(conv_w.shape, lambda i: (0, 0, 0)),
                  vec(mix_w), vec(mix_w), vec(mix_w),
                  _layer_block(w_out, layer),
                  vec(d_model), vec(d_model)],
        out_specs=pl.BlockSpec((tm, d_model), lambda i: (i, 0)),
        scratch_shapes=[pltpu.VMEM((SUBLANES, tm + 2 * HALO, mix_w), F32), pltpu.VMEM((tm, mix_w), F32)],
        compiler_params=_cparams(("parallel",)),
        name="odd_tail",
    )(y_c, u, u, u, x, conv_w, conv_b, conv_g, conv_beta, w_out, ln_g, ln_b)


def _xattn_kernel(alpha, x_ref, wq_ref, k_ref, v_ref, wo_ref, lg_ref, lb_ref, o_ref):
    d_model = x_ref.shape[1]
    hd = d_model // XA_HEADS
    x = x_ref[...]
    q = (_dot(x.astype(BF16), _w(wq_ref[...])) * (hd ** -0.5)).astype(BF16)
    outs = []
    for h in range(XA_HEADS):
        sl = slice(h * hd, (h + 1) * hd)
        s = _dot_nt(q[:, sl], k_ref[:, sl])
        m = jnp.max(s, axis=-1, keepdims=True)
        p = jnp.exp(s - m)
        l = jnp.sum(p, axis=-1, keepdims=True)
        outs.append(_dot(p.astype(BF16), v_ref[:, sl]) / l)
    o = jnp.concatenate(outs, axis=-1).astype(BF16)
    y = _dot(o, _w(wo_ref[...]))
    o_ref[...] = _layer_norm(alpha * x + y, lg_ref[...], lb_ref[...])


def _xattn(x, wq, kv_bf, wo, layer, ln_g, ln_b, alpha, tm):
    s, d_model = x.shape
    vec = pl.BlockSpec((1, d_model), lambda i: (0, 0))
    return pl.pallas_call(
        functools.partial(_xattn_kernel, alpha),
        out_shape=jax.ShapeDtypeStruct((s, d_model), F32),
        grid=(s // tm,),
        in_specs=[pl.BlockSpec((tm, d_model), lambda i: (i, 0)),
                  _layer_block(wq, layer),
                  pl.BlockSpec((kv_bf.shape[0], d_model), lambda i: (0, 0)),
                  pl.BlockSpec((kv_bf.shape[0], d_model), lambda i: (0, 1)),
                  _layer_block(wo, layer), vec, vec],
        out_specs=pl.BlockSpec((tm, d_model), lambda i: (i, 0)),
        compiler_params=_cparams(("parallel",)),
        name="mem_xattn",
    )(x, wq, kv_bf, kv_bf, wo, ln_g, ln_b)


def _ffn_kernel(ck, alpha, x_ref, wg_ref, wu_ref, wd_ref, lg_ref, lb_ref, o_ref):
    d_ff = wg_ref.shape[1]
    x = x_ref[...]
    xb = x.astype(BF16)
    y = jnp.zeros(x.shape, F32)
    for c in range(d_ff // ck):
        sl = slice(c * ck, (c + 1) * ck)
        hcn = _silu(_dot(xb, _w(wg_ref[:, sl]))) * _dot(xb, _w(wu_ref[:, sl]))
        y = y + _dot(hcn.astype(BF16), _w(wd_ref[sl, :]))
    o_ref[...] = _layer_norm(alpha * x + y, lg_ref[...], lb_ref[...])


def _ffn(x, w_gu, w_down, layer, ln_g, ln_b, alpha, tm, ck):
    s, d_model = x.shape
    d_ff = w_down.shape[1]
    assert d_ff % ck == 0, (d_ff, ck)
    vec = pl.BlockSpec((1, d_model), lambda i: (0, 0))
    return pl.pallas_call(
        functools.partial(_ffn_kernel, ck, alpha),
        out_shape=jax.ShapeDtypeStruct((s, d_model), F32),
        grid=(s // tm,),
        in_specs=[pl.BlockSpec((tm, d_model), lambda i: (i, 0)),
                  _layer_block(w_gu, layer, d_ff, 0), _layer_block(w_gu, layer, d_ff, 1),
                  _layer_block(w_down, layer), vec, vec],
        out_specs=pl.BlockSpec((tm, d_model), lambda i: (i, 0)),
        compiler_params=_cparams(("parallel",)),
        name="swiglu_ffn",
    )(x, w_gu, w_gu, w_down, ln_g, ln_b)


def _rope_tables(s, hd):
    rows = s // GRID_W
    row = np.repeat(np.arange(rows), GRID_W)
    col = np.tile(np.arange(GRID_W), rows)
    half = hd // 2
    freqs = ROPE_THETA ** (-np.arange(0, half, 2, dtype=np.float64) / half)

    def ang(p):
        a = p.astype(np.float64)[:, None] * freqs[None, :]
        return np.concatenate([a, a], axis=-1)

    angles = np.concatenate([ang(row), ang(col)], axis=-1)
    cos, sin = np.cos(angles), np.sin(angles)
    lo = (np.arange(hd) % (hd // 2)) < (hd // 4)
    as_f32 = lambda a: jnp.asarray(a.astype(np.float32))
    return as_f32(cos), as_f32(np.where(lo, -sin, 0.0)), as_f32(np.where(lo, 0.0, sin))


def kernel(x, mem, w_in_ab, hgrn_lb_logits, hgrn_norm_g, pool_w, pool_scale, w_out_ab, w_in_cd, q_norm_g, k_norm_g, conv_w, conv_b, conv_ln_g, conv_ln_b, w_out_cd, xa_wq, xa_wkv, xa_wo, ffn_w_gu, ffn_w_down, ln_g, ln_b):
    depth = xa_wq.shape[0]
    alpha = (2 * depth) ** 0.25
    bsz, s, d_model = x.shape
    mix_w = d_model // 2
    hd_c = mix_w // C_HEADS
    kv_w = C_KV_HEADS * hd_c
    tm = min(ROW_TILE, s)

    cum = jnp.cumsum(jax.nn.softmax(hgrn_lb_logits.astype(F32), axis=1), axis=1)
    lb = jnp.maximum(cum - cum[:, :1], 0.0)
    cos, sin_lo, sin_hi = _rope_tables(s, hd_c)
    row = lambda a: a.reshape(1, -1)

    outs = []
    for bi in range(bsz):
        xb = x[bi]
        memb = mem[bi]
        for l in range(depth):
            j = l // 2
            if l % 2 == 0:
                h = _proj(xb, w_in_ab, j, tm, F32, "even_in")
                o_f, o_b = _hgrn(h, row(lb[0, l]), row(lb[1, l]), mix_w)
                xb = _even_tail(o_f, o_b, h, xb, row(hgrn_norm_g[j]), pool_w, row(pool_scale[j]),
                                w_out_ab, j, row(ln_g[l, 0]), row(ln_b[l, 0]), alpha, tm)
            else:
                q, k, v, u = _odd_in(xb, w_in_cd, j, row(q_norm_g[j]), row(k_norm_g[j]),
                                     cos, sin_lo, sin_hi, mix_w, kv_w, tm)
                y_c = _attention(q, k, v, hd_c, min(ATTN_TQ, s), min(ATTN_TK, s))
                taps = jnp.broadcast_to(conv_w[j][:, None, :], (conv_w.shape[1], SUBLANES, mix_w))
                xb = _odd_tail(y_c, u, xb, taps, row(conv_b[j]), row(conv_ln_g[j]), row(conv_ln_b[j]),
                               w_out_cd, j, row(ln_g[l, 0]), row(ln_b[l, 0]), alpha, tm)
            kv = _proj(memb, xa_wkv, l, memb.shape[0], BF16, "mem_kv")
            xb = _xattn(xb, xa_wq, kv, xa_wo, l, row(ln_g[l, 1]), row(ln_b[l, 1]), alpha,
                        min(XA_ROW_TILE, s))
            xb = _ffn(xb, ffn_w_gu, ffn_w_down, l, row(ln_g[l, 2]), row(ln_b[l, 2]), alpha, tm, FFN_CHUNK)
        outs.append(xb)
    return jnp.stack(outs, axis=0)
```

```python
import functools
import math

import jax
import jax.numpy as jnp
import numpy as np
from jax import lax
from jax.experimental import pallas as pl
from jax.experimental.pallas import tpu as pltpu

F32 = jnp.float32
BF16 = jnp.bfloat16

A_HEADS = 4
POOL_WINDOWS = (2, 4, 8, 16)
C_HEADS = 4
C_KV_HEADS = 2
GRID_W = 64
ROPE_THETA = 10000.0
CONV_W = 31
XA_HEADS = 4
EPS = 1e-6
LOG2E = math.log2(math.e)

SUBLANES = 8
VMEM_LIMIT = 56 * 1024 * 1024

ROW_TILE = 512
HG_BLOCK = 128
HG_SAFE_SPAN = 80.0
ATTN_TQ = 2048
ATTN_TK = 512
FFN_CHUNK = 256
XA_ROW_TILE = 1024
CONV_ROWS = 32
HALO = 16
ATTN_SUM_ROWS = 16


def _cparams(sem):
    return pltpu.CompilerParams(dimension_semantics=sem, vmem_limit_bytes=VMEM_LIMIT)


def _silu(x):
    return x * jax.nn.sigmoid(x)


def _layer_norm(y, g, b):
    mu = jnp.mean(y, axis=-1, keepdims=True)
    d = y - mu
    var = jnp.mean(d * d, axis=-1, keepdims=True)
    return d * lax.rsqrt(var + EPS) * g + b


def _dot(a, b):
    return jnp.dot(a, b, preferred_element_type=F32)


def _w(w):
    return w.astype(BF16)


def _dot_nt(a, b):
    return lax.dot_general(a, b, (((1,), (1,)), ((), ())), preferred_element_type=F32)


def _layer_block(stack, layer, cols=None, col_block=0):
    shape = tuple(stack.shape[1:])
    if cols is not None:
        shape = shape[:-1] + (cols,)
    index = (layer,) + (0,) * (len(shape) - 1) + (col_block,)
    return pl.BlockSpec((None,) + shape, lambda *_: index, pipeline_mode=pl.Buffered(1))


def _proj_kernel(x_ref, w_ref, o_ref):
    o_ref[...] = _dot(x_ref[...].astype(BF16), _w(w_ref[...])).astype(o_ref.dtype)


def _proj(x, w_stack, layer, tm, out_dtype, name):
    m, k = x.shape
    n = w_stack.shape[2]
    return pl.pallas_call(
        _proj_kernel,
        out_shape=jax.ShapeDtypeStruct((m, n), out_dtype),
        grid=(m // tm,),
        in_specs=[pl.BlockSpec((tm, k), lambda i: (i, 0)), _layer_block(w_stack, layer)],
        out_specs=pl.BlockSpec((tm, n), lambda i: (i, 0)),
        compiler_params=_cparams(("parallel",)),
        name=name,
    )(x, w_stack)


def _hgrn_direction(q_ref, v_ref, z_ref, lb, lv_ref, st_ref, o_ref, reverse):
    n = HG_BLOCK
    w = q_ref.shape[1]
    hd = w // A_HEADS
    z = z_ref[...]
    qs = _silu(q_ref[...])
    v = v_ref[...]
    t = jnp.exp(-jnp.abs(z))
    k = (1.0 - lb) * (jnp.where(z > 0, t, 1.0) / (1.0 + t))
    log_lb = jnp.log(lb)
    c = jnp.log1p(-lb) + (jnp.minimum(z, 0.0) - jnp.log1p(t))
    g = jnp.maximum(log_lb, c) + jnp.log1p(jnp.exp(-jnp.abs(log_lb - c)))

    rowv = lax.broadcasted_iota(jnp.int32, (n, 1), 0)
    tri = jnp.where(lv_ref[...] >= 0, 1.0, 0.0).astype(BF16)
    g1 = g.astype(BF16)
    r1 = g - g1.astype(F32)
    g2 = r1.astype(BF16)
    g3 = (r1 - g2.astype(F32)).astype(BF16)
    yield
    b = _dot(tri, g1) + _dot(tri, g2) + _dot(tri, g3)
    yield

    end = 0 if reverse else n - 1
    b_end = b[end:end + 1, :]
    qe = (qs * jnp.exp(b)).astype(BF16)
    kd = (k * jnp.exp(b_end - b)).astype(BF16)
    dec = jnp.exp(b_end)
    yield

    lv = lv_ref[...]
    heads = [slice(h * hd, (h + 1) * hd) for h in range(A_HEADS)]
    d_mid = b - b[n // 2:n // 2 + 1, :]

    def single_reference(_):
        ql = (qs * jnp.exp2(d_mid * LOG2E)).astype(BF16)
        kl = (k * jnp.exp2(d_mid * -LOG2E)).astype(BF16)
        interacts = lv >= 0
        return [jnp.where(interacts, _dot_nt(ql[:, sl], kl[:, sl]), 0.0) for sl in heads]

    def by_levels(_):
        ng = n // SUBLANES
        b3 = b.reshape(ng, SUBLANES, w)
        sub3 = lax.broadcasted_iota(jnp.int32, (ng, SUBLANES, 1), 1)
        qs_bf = qs.astype(BF16)
        k_bf = k.astype(BF16)
        lvl = [(qs_bf, k_bf, 0)]
        half = n // 2
        while half >= 1:
            two = 2 * half
            off = half if reverse else half - 1
            if half >= SUBLANES:
                parts = [jnp.broadcast_to(b[blk * two + off:blk * two + off + 1, :], (two, w))
                         for blk in range(n // two)]
                bref = jnp.concatenate(parts, axis=0) if len(parts) > 1 else parts[0]
            else:
                bref3 = jnp.broadcast_to(b3[:, off:off + 1, :], (ng, SUBLANES, w))
                for blk in range(1, SUBLANES // two):
                    r = blk * two + off
                    bref3 = jnp.where(sub3 >= blk * two,
                                      jnp.broadcast_to(b3[:, r:r + 1, :], (ng, SUBLANES, w)), bref3)
                bref = bref3.reshape(n, w)
            second = (rowv % two) >= half
            q_side = jnp.logical_not(second) if reverse else second
            e = jnp.exp2((b - bref) * jnp.where(q_side, LOG2E, -LOG2E)).astype(BF16)
            lvl.append((qs_bf * e, k_bf * e, half))
            half //= 2
        sms = [jnp.zeros((n, n), F32) for _ in heads]
        for ql, kl, level_id in lvl:
            mask = lv == level_id
            sms = [jnp.where(mask, _dot_nt(ql[:, sl], kl[:, sl]), sm) for sl, sm in zip(heads, sms)]
        return sms

    reach = jnp.max(jnp.abs(d_mid)) + jnp.log(jnp.maximum(jnp.max(jnp.abs(qs)), 1.0))
    sms = lax.cond(reach < HG_SAFE_SPAN, single_reference, by_levels, None)
    yield

    for h, (sl, sm) in enumerate(zip(heads, sms)):
        st = st_ref[h]
        vh = v[:, sl]
        o = _dot_nt(qe[:, sl], st.astype(BF16)) + _dot(sm.astype(BF16), vh.astype(BF16))
        o_ref[:, sl] = o
        st_ref[h] = st * dec[:, sl] + _dot(vh.T.astype(BF16), kd[:, sl])
        yield


_DONE = object()


def _hgrn_kernel(lbf_ref, lbb_ref, lvf_ref, lvb_ref, qf_ref, vf_ref, zf_ref, qb_ref, vb_ref, zb_ref,
                 of_ref, ob_ref, stf_ref, stb_ref):
    @pl.when(pl.program_id(0) == 0)
    def _():
        stf_ref[...] = jnp.zeros_like(stf_ref)
        stb_ref[...] = jnp.zeros_like(stb_ref)

    pending = [_hgrn_direction(qf_ref, vf_ref, zf_ref, lbf_ref[...], lvf_ref, stf_ref, of_ref, False),
               _hgrn_direction(qb_ref, vb_ref, zb_ref, lbb_ref[...], lvb_ref, stb_ref, ob_ref, True)]
    while pending:
        pending = [d for d in pending if next(d, _DONE) is not _DONE]


def _hgrn_level_tables(n):
    t, s = np.indices((n, n))
    x = t ^ s
    half = np.where(x > 0, 1 << (np.floor(np.log2(np.maximum(x, 1))).astype(np.int64)), 0)
    fwd = np.where(s <= t, half, -1).astype(np.int32)
    bwd = np.where(s >= t, half, -1).astype(np.int32)
    return jnp.asarray(fwd), jnp.asarray(bwd)


def _hgrn(h, lb_f, lb_b, mix_w):
    s = h.shape[0]
    n = HG_BLOCK
    nb = s // n
    hd = mix_w // A_HEADS
    fwd = lambda c: pl.BlockSpec((n, mix_w), lambda i: (i, c))
    bwd = lambda c: pl.BlockSpec((n, mix_w), lambda i: (nb - 1 - i, c))
    vec = pl.BlockSpec((1, mix_w), lambda i: (0, 0))
    lvs = pl.BlockSpec((n, n), lambda i: (0, 0))
    lv_f, lv_b = _hgrn_level_tables(n)
    return pl.pallas_call(
        _hgrn_kernel,
        out_shape=(jax.ShapeDtypeStruct((s, mix_w), F32), jax.ShapeDtypeStruct((s, mix_w), F32)),
        grid=(nb,),
        in_specs=[vec, vec, lvs, lvs, fwd(0), fwd(1), fwd(2), bwd(0), bwd(1), bwd(3)],
        out_specs=(pl.BlockSpec((n, mix_w), lambda i: (i, 0)),
                   pl.BlockSpec((n, mix_w), lambda i: (nb - 1 - i, 0))),
        scratch_shapes=[pltpu.VMEM((A_HEADS, hd, hd), F32), pltpu.VMEM((A_HEADS, hd, hd), F32)],
        compiler_params=_cparams(("arbitrary",)),
        name="hgrn_scan",
    )(lb_f, lb_b, lv_f, lv_b, h, h, h, h, h, h)


def _halo_ext(prev_ref, cur_ref, next_ref):
    i = pl.program_id(0)
    last = pl.num_programs(0) - 1
    prev = jnp.where(i == 0, 0.0, prev_ref[...])
    nxt = jnp.where(i == last, 0.0, next_ref[...])
    return jnp.concatenate([prev, cur_ref[...], nxt], axis=0)


def _even_tail_kernel(seq_len, alpha, of_ref, ob_ref, og_ref, up_ref, u_ref, un_ref, x_ref,
                      ng_ref, pw_ref, ps_ref, wo_ref, lg_ref, lb_ref, o_ref):
    t = x_ref.shape[0]
    mix_w = of_ref.shape[1]
    hd = mix_w // A_HEADS
    o = of_ref[...] + ob_ref[...]
    gate = _silu(og_ref[...])
    ng = ng_ref[...]
    ya = []
    for h in range(A_HEADS):
        sl = slice(h * hd, (h + 1) * hd)
        oh = o[:, sl]
        r = lax.rsqrt(jnp.mean(oh * oh, axis=-1, keepdims=True) + EPS)
        ya.append(oh * r * ng[:, sl] * gate[:, sl])
    ya = jnp.concatenate(ya, axis=-1).astype(BF16)

    ext = _halo_ext(up_ref, u_ref, un_ref)
    rows = ext.shape[0]
    gw = mix_w // len(POOL_WINDOWS)
    tpos = pl.program_id(0) * t + lax.broadcasted_iota(jnp.int32, (t, 1), 0)
    ps = ps_ref[...]
    yb = []
    for gi, win in enumerate(POOL_WINDOWS):
        sl = slice(gi * gw, (gi + 1) * gw)
        e = ext[:, sl]
        acc = e + pltpu.roll(e, 1, axis=0)
        span = 2
        while span < win:
            sh = span // 2
            acc = pltpu.roll(acc, rows - sh, axis=0) + pltpu.roll(acc, sh, axis=0)
            span *= 2
        wsum = acc[HALO:HALO + t, :]
        lo = jnp.maximum(tpos - win // 2, 0)
        hi = jnp.minimum(tpos - win // 2 + win - 1, seq_len - 1)
        cnt = (hi - lo + 1).astype(F32)
        d = wsum / cnt - e[HALO:HALO + t, :]
        yb.append(_dot(d.astype(BF16), _w(pw_ref[gi])) * ps[:, sl])
    yb = jnp.concatenate(yb, axis=-1).astype(BF16)

    y = _dot(ya, _w(wo_ref[0:mix_w, :])) + _dot(yb, _w(wo_ref[mix_w:2 * mix_w, :]))
    o_ref[...] = _layer_norm(alpha * x_ref[...] + y, lg_ref[...], lb_ref[...])


def _even_tail(o_f, o_b, h, x, norm_g, pool_w, pool_scale, w_out, layer, ln_g, ln_b, alpha, tm):
    s, d_model = x.shape
    mix_w = o_f.shape[1]
    hb = tm // HALO
    nhalo = s // HALO
    row = lambda c: pl.BlockSpec((tm, mix_w), lambda i: (i, c))
    vec = lambda n: pl.BlockSpec((1, n), lambda i: (0, 0))
    return pl.pallas_call(
        functools.partial(_even_tail_kernel, s, alpha),
        out_shape=jax.ShapeDtypeStruct((s, d_model), F32),
        grid=(s // tm,),
        in_specs=[row(0), row(0), row(4),
                  pl.BlockSpec((HALO, mix_w), lambda i: (jnp.maximum(i * hb - 1, 0), 5)),
                  row(5),
                  pl.BlockSpec((HALO, mix_w), lambda i: (jnp.minimum((i + 1) * hb, nhalo - 1), 5)),
                  pl.BlockSpec((tm, d_model), lambda i: (i, 0)),
                  vec(mix_w),
                  _layer_block(pool_w, layer),
                  vec(mix_w),
                  _layer_block(w_out, layer),
                  vec(d_model), vec(d_model)],
        out_specs=pl.BlockSpec((tm, d_model), lambda i: (i, 0)),
        compiler_params=_cparams(("parallel",)),
        name="even_tail",
    )(o_f, o_b, h, h, h, h, x, norm_g, pool_w, pool_scale, w_out, ln_g, ln_b)


def _rope(x, cos, sin_lo, sin_hi):
    n = x.shape[-1]
    return x * cos + pltpu.roll(x, n - 32, axis=1) * sin_lo + pltpu.roll(x, 32, axis=1) * sin_hi


def _odd_in_kernel(q_scale, x_ref, w_ref, qg_ref, kg_ref, cos_ref, sl_ref, sh_ref,
                   q_ref, k_ref, v_ref, u_ref):
    hd = cos_ref.shape[1]
    qw = q_ref.shape[0]
    kw = k_ref.shape[1]
    mix_w = u_ref.shape[1]
    h = _dot(x_ref[...].astype(BF16), _w(w_ref[...]))
    cos, s_lo, s_hi = cos_ref[...], sl_ref[...], sh_ref[...]

    def norm_rope(a, g):
        r = lax.rsqrt(jnp.mean(a * a, axis=-1, keepdims=True) + EPS)
        return _rope(a * r * g, cos, s_lo, s_hi)

    qg = qg_ref[...]
    kg = kg_ref[...]
    for i in range(qw // hd):
        q_ref[i * hd:(i + 1) * hd, :] = (norm_rope(h[:, i * hd:(i + 1) * hd], qg) * q_scale).T.astype(BF16)
    for i in range(kw // hd):
        k_ref[:, i * hd:(i + 1) * hd] = norm_rope(h[:, qw + i * hd:qw + (i + 1) * hd], kg).astype(BF16)
        v_ref[i * hd:(i + 1) * hd, :] = h[:, qw + kw + i * hd:qw + kw + (i + 1) * hd].T.astype(BF16)
    a0 = qw + 2 * kw
    u_ref[...] = h[:, a0:a0 + mix_w] * jax.nn.sigmoid(h[:, a0 + mix_w:a0 + 2 * mix_w])


def _odd_in(x, w_in, layer, q_g, k_g, cos, sin_lo, sin_hi, mix_w, kv_w, tm):
    s, d_model = x.shape
    hd = cos.shape[1]
    q_scale = hd ** -0.5 * LOG2E
    vec = pl.BlockSpec((1, hd), lambda i: (0, 0))
    tab = pl.BlockSpec((tm, hd), lambda i: (i, 0))
    return pl.pallas_call(
        functools.partial(_odd_in_kernel, q_scale),
        out_shape=(jax.ShapeDtypeStruct((mix_w, s), BF16), jax.ShapeDtypeStruct((s, kv_w), BF16),
                   jax.ShapeDtypeStruct((kv_w, s), BF16), jax.ShapeDtypeStruct((s, mix_w), F32)),
        grid=(s // tm,),
        in_specs=[pl.BlockSpec((tm, d_model), lambda i: (i, 0)),
                  _layer_block(w_in, layer),
                  vec, vec, tab, tab, tab],
        out_specs=(pl.BlockSpec((mix_w, tm), lambda i: (0, i)), pl.BlockSpec((tm, kv_w), lambda i: (i, 0)),
                   pl.BlockSpec((kv_w, tm), lambda i: (0, i)), pl.BlockSpec((tm, mix_w), lambda i: (i, 0))),
        compiler_params=_cparams(("parallel",)),
        name="odd_in",
    )(x, w_in, q_g, k_g, cos, sin_lo, sin_hi)


def _attn_kernel(tk, qt_ref, k_ref, vt_ref, o_ref, acc_ref, s_ref, p_ref):
    hd = k_ref.shape[1]
    grp = qt_ref.shape[0] // hd
    tq = qt_ref.shape[1]
    s_len = k_ref.shape[0]
    n = grp * tq
    qt = jnp.concatenate([qt_ref[g * hd:(g + 1) * hd, :] for g in range(grp)], axis=1)
    nt = s_len // tk

    def scores(t):
        off = pl.multiple_of(t * tk, tk)
        return _dot(k_ref[pl.ds(off, tk), :], qt)

    ones_rows = jnp.ones((ATTN_SUM_ROWS, tk), BF16)

    def weighted_values(t, slot):
        off = pl.multiple_of(t * tk, tk)
        vt = jnp.concatenate([vt_ref[:, pl.ds(off, tk)], ones_rows], axis=0)
        return _dot(vt, p_ref[slot])

    def step(t, cur, carry):
        m_old, a_prev, mx = carry
        s_next = scores(jnp.minimum(t + 1, nt - 1))
        s_ref[1 - cur] = s_next
        mx_next = jnp.max(s_next, axis=0, keepdims=True)
        acc_ref[...] = a_prev * acc_ref[...] + weighted_values(jnp.maximum(t - 1, 0), 1 - cur)
        m_new = jnp.maximum(m_old, mx)
        p_ref[cur] = jnp.exp2(s_ref[cur] - m_new).astype(BF16)
        return m_new, jnp.exp2(m_old - m_new), mx_next

    def body(j, carry):
        return step(2 * j + 1, 1, step(2 * j, 0, carry))

    acc_ref[...] = jnp.zeros(acc_ref.shape, F32)
    p_ref[1] = jnp.zeros(p_ref.shape[1:], BF16)
    s_first = scores(0)
    s_ref[0] = s_first
    init = (jnp.full((1, n), -jnp.inf, F32), jnp.ones((1, n), F32), jnp.max(s_first, axis=0, keepdims=True))
    _, a_last, _ = lax.fori_loop(0, nt // 2, body, init)
    acc = a_last * acc_ref[...] + weighted_values(nt - 1, 1)
    out = acc[0:hd, :] / acc[hd:hd + 1, :]
    for g in range(grp):
        o_ref[:, g * hd:(g + 1) * hd] = out[:, g * tq:(g + 1) * tq].T.astype(o_ref.dtype)


def _attention(qt, k, vt, hd, tq, tk):
    qw, s = qt.shape
    kvh = k.shape[1] // hd
    gw = qw // kvh
    return pl.pallas_call(
        functools.partial(_attn_kernel, tk),
        out_shape=jax.ShapeDtypeStruct((s, qw), BF16),
        grid=(kvh, s // tq),
        in_specs=[pl.BlockSpec((gw, tq), lambda h, i: (h, i)),
                  pl.BlockSpec((s, hd), lambda h, i: (0, h)),
                  pl.BlockSpec((hd, s), lambda h, i: (h, 0))],
        out_specs=pl.BlockSpec((tq, gw), lambda h, i: (i, h)),
        scratch_shapes=[pltpu.VMEM((hd + ATTN_SUM_ROWS, gw // hd * tq), F32),
                        pltpu.VMEM((2, tk, gw // hd * tq), F32),
                        pltpu.VMEM((2, tk, gw // hd * tq), BF16)],
        compiler_params=_cparams(("parallel", "parallel")),
        name="gqa_attention",
    )(qt, k, vt)


def _odd_tail_kernel(alpha, yc_ref, up_ref, u_ref, un_ref, x_ref, cw_ref, cb_ref, cg_ref, cbeta_ref,
                     wo_ref, lg_ref, lb_ref, o_ref, ext_ref, conv_ref):
    t = x_ref.shape[0]
    mix_w = u_ref.shape[1]
    ext = _halo_ext(up_ref, u_ref, un_ref)
    rows = ext.shape[0]
    ext_ref[0] = ext
    for r in range(1, SUBLANES):
        ext_ref[r] = pltpu.roll(ext, rows - r, axis=0)
    base = HALO - CONV_W // 2
    cb = cb_ref[...]
    groups = CONV_ROWS // SUBLANES
    for c in range(t // CONV_ROWS):
        acc = jnp.zeros((groups, SUBLANES, mix_w), F32)
        for j in range(CONV_W):
            r = (base + j) % SUBLANES
            start = c * CONV_ROWS + (base + j - r)
            xw = ext_ref[r, start:start + CONV_ROWS, :].reshape(groups, SUBLANES, mix_w)
            acc = acc + xw * cw_ref[j]
        conv_ref[c * CONV_ROWS:(c + 1) * CONV_ROWS, :] = acc.reshape(CONV_ROWS, mix_w) + cb
    conv = conv_ref[...]
    yd = _silu(_layer_norm(conv, cg_ref[...], cbeta_ref[...])).astype(BF16)
    y = _dot(yc_ref[...], _w(wo_ref[0:mix_w, :])) + _dot(yd, _w(wo_ref[mix_w:2 * mix_w, :]))
    o_ref[...] = _layer_norm(alpha * x_ref[...] + y, lg_ref[...], lb_ref[...])


def _odd_tail(y_c, u, x, conv_w, conv_b, conv_g, conv_beta, w_out, layer, ln_g, ln_b, alpha, tm):
    s, d_model = x.shape
    mix_w = u.shape[1]
    hb = tm // HALO
    nhalo = s // HALO
    row = pl.BlockSpec((tm, mix_w), lambda i: (i, 0))
    vec = lambda n: pl.BlockSpec((1, n), lambda i: (0, 0))
    return pl.pallas_call(
        functools.partial(_odd_tail_kernel, alpha),
        out_shape=jax.ShapeDtypeStruct((s, d_model), F32),
        grid=(s // tm,),
        in_specs=[row,
                  pl.BlockSpec((HALO, mix_w), lambda i: (jnp.maximum(i * hb - 1, 0), 0)),
                  row,
                  pl.BlockSpec((HALO, mix_w), lambda i: (jnp.minimum((i + 1) * hb, nhalo - 1), 0)),
                  pl.BlockSpec((tm, d_model), lambda i: (i, 0)),
                  pl.BlockSpec(conv_w.shape, lambda i: (0, 0, 0)),
                  vec(mix_w), vec(mix_w), vec(mix_w),
                  _layer_block(w_out, layer),
                  vec(d_model), vec(d_model)],
        out_specs=pl.BlockSpec((tm, d_model), lambda i: (i, 0)),
        scratch_shapes=[pltpu.VMEM((SUBLANES, tm + 2 * HALO, mix_w), F32), pltpu.VMEM((tm, mix_w), F32)],
        compiler_params=_cparams(("parallel",)),
        name="odd_tail",
    )(y_c, u, u, u, x, conv_w, conv_b, conv_g, conv_beta, w_out, ln_g, ln_b)


def _xattn_kernel(alpha, x_ref, wq_ref, k_ref, v_ref, wo_ref, lg_ref, lb_ref, o_ref):
    d_model = x_ref.shape[1]
    hd = d_model // XA_HEADS
    x = x_ref[...]
    q = (_dot(x.astype(BF16), _w(wq_ref[...])) * (hd ** -0.5)).astype(BF16)
    outs = []
    for h in range(XA_HEADS):
        sl = slice(h * hd, (h + 1) * hd)
        s = _dot_nt(q[:, sl], k_ref[:, sl])
        m = jnp.max(s, axis=-1, keepdims=True)
        p = jnp.exp(s - m)
        l = jnp.sum(p, axis=-1, keepdims=True)
        outs.append(_dot(p.astype(BF16), v_ref[:, sl]) / l)
    o = jnp.concatenate(outs, axis=-1).astype(BF16)
    y = _dot(o, _w(wo_ref[...]))
    o_ref[...] = _layer_norm(alpha * x + y, lg_ref[...], lb_ref[...])


def _xattn(x, wq, kv_bf, wo, layer, ln_g, ln_b, alpha, tm):
    s, d_model = x.shape
    vec = pl.BlockSpec((1, d_model), lambda i: (0, 0))
    return pl.pallas_call(
        functools.partial(_xattn_kernel, alpha),
        out_shape=jax.ShapeDtypeStruct((s, d_model), F32),
        grid=(s // tm,),
        in_specs=[pl.BlockSpec((tm, d_model), lambda i: (i, 0)),
                  _layer_block(wq, layer),
                  pl.BlockSpec((kv_bf.shape[0], d_model), lambda i: (0, 0)),
                  pl.BlockSpec((kv_bf.shape[0], d_model), lambda i: (0, 1)),
                  _layer_block(wo, layer), vec, vec],
        out_specs=pl.BlockSpec((tm, d_model), lambda i: (i, 0)),
        compiler_params=_cparams(("parallel",)),
        name="mem_xattn",
    )(x, wq, kv_bf, kv_bf, wo, ln_g, ln_b)


def _ffn_kernel(ck, alpha, x_ref, wg_ref, wu_ref, wd_ref, lg_ref, lb_ref, o_ref):
    d_ff = wg_ref.shape[1]
    x = x_ref[...]
    xb = x.astype(BF16)
    y = jnp.zeros(x.shape, F32)
    for c in range(d_ff // ck):
        sl = slice(c * ck, (c + 1) * ck)
        hcn = _silu(_dot(xb, _w(wg_ref[:, sl]))) * _dot(xb, _w(wu_ref[:, sl]))
        y = y + _dot(hcn.astype(BF16), _w(wd_ref[sl, :]))
    o_ref[...] = _layer_norm(alpha * x + y, lg_ref[...], lb_ref[...])


def _ffn(x, w_gu, w_down, layer, ln_g, ln_b, alpha, tm, ck):
    s, d_model = x.shape
    d_ff = w_down.shape[1]
    assert d_ff % ck == 0, (d_ff, ck)
    vec = pl.BlockSpec((1, d_model), lambda i: (0, 0))
    return pl.pallas_call(
        functools.partial(_ffn_kernel, ck, alpha),
        out_shape=jax.ShapeDtypeStruct((s, d_model), F32),
        grid=(s // tm,),
        in_specs=[pl.BlockSpec((tm, d_model), lambda i: (i, 0)),
                  _layer_block(w_gu, layer, d_ff, 0), _layer_block(w_gu, layer, d_ff, 1),
                  _layer_block(w_down, layer), vec, vec],
        out_specs=pl.BlockSpec((tm, d_model), lambda i: (i, 0)),
        compiler_params=_cparams(("parallel",)),
        name="swiglu_ffn",
    )(x, w_gu, w_gu, w_down, ln_g, ln_b)


def _rope_tables(s, hd):
    rows = s // GRID_W
    row = np.repeat(np.arange(rows), GRID_W)
    col = np.tile(np.arange(GRID_W), rows)
    half = hd // 2
    freqs = ROPE_THETA ** (-np.arange(0, half, 2, dtype=np.float64) / half)

    def ang(p):
        a = p.astype(np.float64)[:, None] * freqs[None, :]
        return np.concatenate([a, a], axis=-1)

    angles = np.concatenate([ang(row), ang(col)], axis=-1)
    cos, sin = np.cos(angles), np.sin(angles)
    lo = (np.arange(hd) % (hd // 2)) < (hd // 4)
    as_f32 = lambda a: jnp.asarray(a.astype(np.float32))
    return as_f32(cos), as_f32(np.where(lo, -sin, 0.0)), as_f32(np.where(lo, 0.0, sin))


def kernel(x, mem, w_in_ab, hgrn_lb_logits, hgrn_norm_g, pool_w, pool_scale, w_out_ab, w_in_cd, q_norm_g, k_norm_g, conv_w, conv_b, conv_ln_g, conv_ln_b, w_out_cd, xa_wq, xa_wkv, xa_wo, ffn_w_gu, ffn_w_down, ln_g, ln_b):
    depth = xa_wq.shape[0]
    alpha = (2 * depth) ** 0.25
    bsz, s, d_model = x.shape
    mix_w = d_model // 2
    hd_c = mix_w // C_HEADS
    kv_w = C_KV_HEADS * hd_c
    tm = min(ROW_TILE, s)

    cum = jnp.cumsum(jax.nn.softmax(hgrn_lb_logits.astype(F32), axis=1), axis=1)
    lb = jnp.maximum(cum - cum[:, :1], 0.0)
    cos, sin_lo, sin_hi = _rope_tables(s, hd_c)
    row = lambda a: a.reshape(1, -1)

    outs = []
    for bi in range(bsz):
        xb = x[bi]
        memb = mem[bi]
        for l in range(depth):
            j = l // 2
            if l % 2 == 0:
                h = _proj(xb, w_in_ab, j, tm, F32, "even_in")
                o_f, o_b = _hgrn(h, row(lb[0, l]), row(lb[1, l]), mix_w)
                xb = _even_tail(o_f, o_b, h, xb, row(hgrn_norm_g[j]), pool_w, row(pool_scale[j]),
                                w_out_ab, j, row(ln_g[l, 0]), row(ln_b[l, 0]), alpha, tm)
            else:
                q, k, v, u = _odd_in(xb, w_in_cd, j, row(q_norm_g[j]), row(k_norm_g[j]),
                                     cos, sin_lo, sin_hi, mix_w, kv_w, tm)
                y_c = _attention(q, k, v, hd_c, min(ATTN_TQ, s), min(ATTN_TK, s))
                taps = jnp.broadcast_to(conv_w[j][:, None, :], (conv_w.shape[1], SUBLANES, mix_w))
                xb = _odd_tail(y_c, u, xb, taps, row(conv_b[j]), row(conv_ln_g[j]), row(conv_ln_b[j]),
                               w_out_cd, j, row(ln_g[l, 0]), row(ln_b[l, 0]), alpha, tm)
            kv = _proj(memb, xa_wkv, l, memb.shape[0], BF16, "mem_kv")
            xb = _xattn(xb, xa_wq, kv, xa_wo, l, row(ln_g[l, 1]), row(ln_b[l, 1]), alpha,
                        min(XA_ROW_TILE, s))
            xb = _ffn(xb, ffn_w_gu, ffn_w_down, l, row(ln_g[l, 2]), row(ln_b[l, 2]), alpha, tm, FFN_CHUNK)
        outs.append(xb)
    return jnp.stack(outs, axis=0)
```

```python
import functools
import math

import jax
import jax.numpy as jnp
import numpy as np
from jax import lax
from jax.experimental import pallas as pl
from jax.experimental.pallas import tpu as pltpu

F32 = jnp.float32
BF16 = jnp.bfloat16

A_HEADS = 4
POOL_WINDOWS = (2, 4, 8, 16)
C_HEADS = 4
C_KV_HEADS = 2
GRID_W = 64
ROPE_THETA = 10000.0
CONV_W = 31
XA_HEADS = 4
EPS = 1e-6
LOG2E = math.log2(math.e)

LANES = 128
SUBLANES = 8
VMEM_LIMIT = 56 * 1024 * 1024

ROW_TILE = 512
HG_BLOCK = 128
HG_SAFE_SPAN = 80.0
ATTN_TQ = 2048
ATTN_TK = 512
FFN_CHUNK = 256
XA_ROW_TILE = 1024
CONV_ROWS = 32
HALO = 16
ATTN_SUM_ROWS = 16


def _cparams(sem):
    return pltpu.CompilerParams(dimension_semantics=sem, vmem_limit_bytes=VMEM_LIMIT)


def _silu(x):
    return x * jax.nn.sigmoid(x)


def _layer_norm(y, g, b):
    mu = jnp.mean(y, axis=-1, keepdims=True)
    d = y - mu
    var = jnp.mean(d * d, axis=-1, keepdims=True)
    return d * lax.rsqrt(var + EPS) * g + b


def _dot(a, b):
    return jnp.dot(a, b, preferred_element_type=F32)


def _w(w):
    return w.astype(BF16)


def _dot_nt(a, b):
    return lax.dot_general(a, b, (((1,), (1,)), ((), ())), preferred_element_type=F32)


def _layer_block(stack, layer, cols=None, col_block=0):
    shape = tuple(stack.shape[1:])
    if cols is not None:
        shape = shape[:-1] + (cols,)
    index = (layer,) + (0,) * (len(shape) - 1) + (col_block,)
    return pl.BlockSpec((None,) + shape, lambda *_: index, pipeline_mode=pl.Buffered(1))


def _proj_kernel(x_ref, w_ref, o_ref):
    o_ref[...] = _dot(x_ref[...].astype(BF16), _w(w_ref[...])).astype(o_ref.dtype)


def _proj(x, w_stack, layer, tm, out_dtype, name):
    m, k = x.shape
    n = w_stack.shape[2]
    return pl.pallas_call(
        _proj_kernel,
        out_shape=jax.ShapeDtypeStruct((m, n), out_dtype),
        grid=(m // tm,),
        in_specs=[pl.BlockSpec((tm, k), lambda i: (i, 0)), _layer_block(w_stack, layer)],
        out_specs=pl.BlockSpec((tm, n), lambda i: (i, 0)),
        compiler_params=_cparams(("parallel",)),
        name=name,
    )(x, w_stack)


def _hgrn_direction(q_ref, v_ref, z_ref, lb, lv_ref, st_ref, o_ref, reverse):
    n = HG_BLOCK
    w = q_ref.shape[1]
    hd = w // A_HEADS
    z = z_ref[...]
    qs = _silu(q_ref[...])
    v = v_ref[...]
    t = jnp.exp(-jnp.abs(z))
    k = (1.0 - lb) * (jnp.where(z > 0, t, 1.0) / (1.0 + t))
    log_lb = jnp.log(lb)
    c = jnp.log1p(-lb) + (jnp.minimum(z, 0.0) - jnp.log1p(t))
    g = jnp.maximum(log_lb, c) + jnp.log1p(jnp.exp(-jnp.abs(log_lb - c)))

    rowv = lax.broadcasted_iota(jnp.int32, (n, 1), 0)
    tri = jnp.where(lv_ref[...] >= 0, 1.0, 0.0).astype(BF16)
    g1 = g.astype(BF16)
    r1 = g - g1.astype(F32)
    g2 = r1.astype(BF16)
    g3 = (r1 - g2.astype(F32)).astype(BF16)
    yield
    b = _dot(tri, g1) + _dot(tri, g2) + _dot(tri, g3)
    yield

    end = 0 if reverse else n - 1
    b_end = b[end:end + 1, :]
    qe = (qs * jnp.exp(b)).astype(BF16)
    kd = (k * jnp.exp(b_end - b)).astype(BF16)
    dec = jnp.exp(b_end)
    yield

    lv = lv_ref[...]
    heads = [slice(h * hd, (h + 1) * hd) for h in range(A_HEADS)]
    d_mid = b - b[n // 2:n // 2 + 1, :]

    def single_reference(_):
        ql = (qs * jnp.exp2(d_mid * LOG2E)).astype(BF16)
        kl = (k * jnp.exp2(d_mid * -LOG2E)).astype(BF16)
        interacts = lv >= 0
        return [jnp.where(interacts, _dot_nt(ql[:, sl], kl[:, sl]), 0.0) for sl in heads]

    def by_levels(_):
        ng = n // SUBLANES
        b3 = b.reshape(ng, SUBLANES, w)
        sub3 = lax.broadcasted_iota(jnp.int32, (ng, SUBLANES, 1), 1)
        qs_bf = qs.astype(BF16)
        k_bf = k.astype(BF16)
        lvl = [(qs_bf, k_bf, 0)]
        half = n // 2
        while half >= 1:
            two = 2 * half
            off = half if reverse else half - 1
            if half >= SUBLANES:
                parts = [jnp.broadcast_to(b[blk * two + off:blk * two + off + 1, :], (two, w))
                         for blk in range(n // two)]
                bref = jnp.concatenate(parts, axis=0) if len(parts) > 1 else parts[0]
            else:
                bref3 = jnp.broadcast_to(b3[:, off:off + 1, :], (ng, SUBLANES, w))
                for blk in range(1, SUBLANES // two):
                    r = blk * two + off
                    bref3 = jnp.where(sub3 >= blk * two,
                                      jnp.broadcast_to(b3[:, r:r + 1, :], (ng, SUBLANES, w)), bref3)
                bref = bref3.reshape(n, w)
            second = (rowv % two) >= half
            q_side = jnp.logical_not(second) if reverse else second
            e = jnp.exp2((b - bref) * jnp.where(q_side, LOG2E, -LOG2E)).astype(BF16)
            lvl.append((qs_bf * e, k_bf * e, half))
            half //= 2
        sms = [jnp.zeros((n, n), F32) for _ in heads]
        for ql, kl, level_id in lvl:
            mask = lv == level_id
            sms = [jnp.where(mask, _dot_nt(ql[:, sl], kl[:, sl]), sm) for sl, sm in zip(heads, sms)]
        return sms

    reach = jnp.max(jnp.abs(d_mid)) + jnp.log(jnp.maximum(jnp.max(jnp.abs(qs)), 1.0))
    sms = lax.cond(reach < HG_SAFE_SPAN, single_reference, by_levels, None)
    yield

    for h, (sl, sm) in enumerate(zip(heads, sms)):
        st = st_ref[h]
        vh = v[:, sl]
        o = _dot_nt(qe[:, sl], st.astype(BF16)) + _dot(sm.astype(BF16), vh.astype(BF16))
        o_ref[:, sl] = o
        st_ref[h] = st * dec[:, sl] + _dot(vh.T.astype(BF16), kd[:, sl])
        yield


_DONE = object()


def _hgrn_kernel(lbf_ref, lbb_ref, lvf_ref, lvb_ref, qf_ref, vf_ref, zf_ref, qb_ref, vb_ref, zb_ref,
                 of_ref, ob_ref, stf_ref, stb_ref):
    @pl.when(pl.program_id(0) == 0)
    def _():
        stf_ref[...] = jnp.zeros_like(stf_ref)
        stb_ref[...] = jnp.zeros_like(stb_ref)

    pending = [_hgrn_direction(qf_ref, vf_ref, zf_ref, lbf_ref[...], lvf_ref, stf_ref, of_ref, False),
               _hgrn_direction(qb_ref, vb_ref, zb_ref, lbb_ref[...], lvb_ref, stb_ref, ob_ref, True)]
    while pending:
        pending = [d for d in pending if next(d, _DONE) is not _DONE]


def _hgrn_level_tables(n):
    t, s = np.indices((n, n))
    x = t ^ s
    half = np.where(x > 0, 1 << (np.floor(np.log2(np.maximum(x, 1))).astype(np.int64)), 0)
    fwd = np.where(s <= t, half, -1).astype(np.int32)
    bwd = np.where(s >= t, half, -1).astype(np.int32)
    return jnp.asarray(fwd), jnp.asarray(bwd)


def _hgrn(h, lb_f, lb_b, mix_w):
    s = h.shape[0]
    n = HG_BLOCK
    nb = s // n
    hd = mix_w // A_HEADS
    fwd = lambda c: pl.BlockSpec((n, mix_w), lambda i: (i, c))
    bwd = lambda c: pl.BlockSpec((n, mix_w), lambda i: (nb - 1 - i, c))
    vec = pl.BlockSpec((1, mix_w), lambda i: (0, 0))
    lvs = pl.BlockSpec((n, n), lambda i: (0, 0))
    lv_f, lv_b = _hgrn_level_tables(n)
    return pl.pallas_call(
        _hgrn_kernel,
        out_shape=(jax.ShapeDtypeStruct((s, mix_w), F32), jax.ShapeDtypeStruct((s, mix_w), F32)),
        grid=(nb,),
        in_specs=[vec, vec, lvs, lvs, fwd(0), fwd(1), fwd(2), bwd(0), bwd(1), bwd(3)],
        out_specs=(pl.BlockSpec((n, mix_w), lambda i: (i, 0)),
                   pl.BlockSpec((n, mix_w), lambda i: (nb - 1 - i, 0))),
        scratch_shapes=[pltpu.VMEM((A_HEADS, hd, hd), F32), pltpu.VMEM((A_HEADS, hd, hd), F32)],
        compiler_params=_cparams(("arbitrary",)),
        name="hgrn_scan",
    )(lb_f, lb_b, lv_f, lv_b, h, h, h, h, h, h)


def _halo_ext(prev_ref, cur_ref, next_ref):
    i = pl.program_id(0)
    last = pl.num_programs(0) - 1
    prev = jnp.where(i == 0, 0.0, prev_ref[...])
    nxt = jnp.where(i == last, 0.0, next_ref[...])
    return jnp.concatenate([prev, cur_ref[...], nxt], axis=0)


def _even_tail_kernel(seq_len, alpha, of_ref, ob_ref, og_ref, up_ref, u_ref, un_ref, x_ref,
                      ng_ref, pw_ref, ps_ref, wo_ref, lg_ref, lb_ref, o_ref):
    t = x_ref.shape[0]
    mix_w = of_ref.shape[1]
    hd = mix_w // A_HEADS
    o = of_ref[...] + ob_ref[...]
    gate = _silu(og_ref[...])
    ng = ng_ref[...]
    ya = []
    for h in range(A_HEADS):
        sl = slice(h * hd, (h + 1) * hd)
        oh = o[:, sl]
        r = lax.rsqrt(jnp.mean(oh * oh, axis=-1, keepdims=True) + EPS)
        ya.append(oh * r * ng[:, sl] * gate[:, sl])
    ya = jnp.concatenate(ya, axis=-1).astype(BF16)

    ext = _halo_ext(up_ref, u_ref, un_ref)
    rows = ext.shape[0]
    gw = mix_w // len(POOL_WINDOWS)
    tpos = pl.program_id(0) * t + lax.broadcasted_iota(jnp.int32, (t, 1), 0)
    ps = ps_ref[...]
    yb = []
    for gi, win in enumerate(POOL_WINDOWS):
        sl = slice(gi * gw, (gi + 1) * gw)
        e = ext[:, sl]
        acc = e + pltpu.roll(e, 1, axis=0)
        span = 2
        while span < win:
            sh = span // 2
            acc = pltpu.roll(acc, rows - sh, axis=0) + pltpu.roll(acc, sh, axis=0)
            span *= 2
        wsum = acc[HALO:HALO + t, :]
        lo = jnp.maximum(tpos - win // 2, 0)
        hi = jnp.minimum(tpos - win // 2 + win - 1, seq_len - 1)
        cnt = (hi - lo + 1).astype(F32)
        d = wsum / cnt - e[HALO:HALO + t, :]
        yb.append(_dot(d.astype(BF16), _w(pw_ref[gi])) * ps[:, sl])
    yb = jnp.concatenate(yb, axis=-1).astype(BF16)

    y = _dot(ya, _w(wo_ref[0:mix_w, :])) + _dot(yb, _w(wo_ref[mix_w:2 * mix_w, :]))
    o_ref[...] = _layer_norm(alpha * x_ref[...] + y, lg_ref[...], lb_ref[...])


def _even_tail(o_f, o_b, h, x, norm_g, pool_w, pool_scale, w_out, layer, ln_g, ln_b, alpha, tm):
    s, d_model = x.shape
    mix_w = o_f.shape[1]
    hb = tm // HALO
    nhalo = s // HALO
    row = lambda c: pl.BlockSpec((tm, mix_w), lambda i: (i, c))
    vec = lambda n: pl.BlockSpec((1, n), lambda i: (0, 0))
    return pl.pallas_call(
        functools.partial(_even_tail_kernel, s, alpha),
        out_shape=jax.ShapeDtypeStruct((s, d_model), F32),
        grid=(s // tm,),
        in_specs=[row(0), row(0), row(4),
                  pl.BlockSpec((HALO, mix_w), lambda i: (jnp.maximum(i * hb - 1, 0), 5)),
                  row(5),
                  pl.BlockSpec((HALO, mix_w), lambda i: (jnp.minimum((i + 1) * hb, nhalo - 1), 5)),
                  pl.BlockSpec((tm, d_model), lambda i: (i, 0)),
                  vec(mix_w),
                  _layer_block(pool_w, layer),
                  vec(mix_w),
                  _layer_block(w_out, layer),
                  vec(d_model), vec(d_model)],
        out_specs=pl.BlockSpec((tm, d_model), lambda i: (i, 0)),
        compiler_params=_cparams(("parallel",)),
        name="even_tail",
    )(o_f, o_b, h, h, h, h, x, norm_g, pool_w, pool_scale, w_out, ln_g, ln_b)


def _rope(x, cos, sin_lo, sin_hi):
    n = x.shape[-1]
    r = n // 4
    return x * cos + pltpu.roll(x, n - r, axis=1) * sin_lo + pltpu.roll(x, r, axis=1) * sin_hi


def _odd_in_kernel(q_scale, x_ref, w_ref, qg_ref, kg_ref, cos_ref, sl_ref, sh_ref,
                   q_ref, k_ref, v_ref, u_ref):
    hd = cos_ref.shape[1]
    qw = q_ref.shape[0]
    kw = k_ref.shape[1]
    mix_w = u_ref.shape[1]
    h = _dot(x_ref[...].astype(BF16), _w(w_ref[...]))
    cos, s_lo, s_hi = cos_ref[...], sl_ref[...], sh_ref[...]

    def norm_rope(a, g):
        r = lax.rsqrt(jnp.mean(a * a, axis=-1, keepdims=True) + EPS)
        return _rope(a * r * g, cos, s_lo, s_hi)

    qg = qg_ref[...]
    kg = kg_ref[...]
    for i in range(qw // hd):
        q_ref[i * hd:(i + 1) * hd, :] = (norm_rope(h[:, i * hd:(i + 1) * hd], qg) * q_scale).T.astype(BF16)
    for i in range(kw // hd):
        k_ref[:, i * hd:(i + 1) * hd] = norm_rope(h[:, qw + i * hd:qw + (i + 1) * hd], kg).astype(BF16)
        v_ref[i * hd:(i + 1) * hd, :] = h[:, qw + kw + i * hd:qw + kw + (i + 1) * hd].T.astype(BF16)
    a0 = qw + 2 * kw
    u_ref[...] = h[:, a0:a0 + mix_w] * jax.nn.sigmoid(h[:, a0 + mix_w:a0 + 2 * mix_w])


def _odd_in(x, w_in, layer, q_g, k_g, cos, sin_lo, sin_hi, mix_w, kv_w, tm):
    s, d_model = x.shape
    hd = cos.shape[1]
    q_scale = hd ** -0.5 * LOG2E
    vec = pl.BlockSpec((1, hd), lambda i: (0, 0))
    tab = pl.BlockSpec((tm, hd), lambda i: (i, 0))
    return pl.pallas_call(
        functools.partial(_odd_in_kernel, q_scale),
        out_shape=(jax.ShapeDtypeStruct((mix_w, s), BF16), jax.ShapeDtypeStruct((s, kv_w), BF16),
                   jax.ShapeDtypeStruct((kv_w, s), BF16), jax.ShapeDtypeStruct((s, mix_w), F32)),
        grid=(s // tm,),
        in_specs=[pl.BlockSpec((tm, d_model), lambda i: (i, 0)),
                  _layer_block(w_in, layer),
                  vec, vec, tab, tab, tab],
        out_specs=(pl.BlockSpec((mix_w, tm), lambda i: (0, i)), pl.BlockSpec((tm, kv_w), lambda i: (i, 0)),
                   pl.BlockSpec((kv_w, tm), lambda i: (0, i)), pl.BlockSpec((tm, mix_w), lambda i: (i, 0))),
        compiler_params=_cparams(("parallel",)),
        name="odd_in",
    )(x, w_in, q_g, k_g, cos, sin_lo, sin_hi)


def _attn_kernel(tk, qt_ref, k_ref, vt_ref, o_ref, acc_ref, s_ref, p_ref):
    hd = k_ref.shape[1]
    grp = qt_ref.shape[0] // hd
    tq = qt_ref.shape[1]
    s_len = k_ref.shape[0]
    n = grp * tq
    qt = jnp.concatenate([qt_ref[g * hd:(g + 1) * hd, :] for g in range(grp)], axis=1)
    nt = s_len // tk

    def scores(t):
        off = pl.multiple_of(t * tk, tk)
        return _dot(k_ref[pl.ds(off, tk), :], qt)

    ones_rows = jnp.ones((ATTN_SUM_ROWS, tk), BF16)

    def weighted_values(t, slot):
        off = pl.multiple_of(t * tk, tk)
        vt = jnp.concatenate([vt_ref[:, pl.ds(off, tk)], ones_rows], axis=0)
        return _dot(vt, p_ref[slot])

    def step(t, cur, carry):
        m_old, a_prev, mx = carry
        s_next = scores(jnp.minimum(t + 1, nt - 1))
        s_ref[1 - cur] = s_next
        mx_next = jnp.max(s_next, axis=0, keepdims=True)
        acc_ref[...] = a_prev * acc_ref[...] + weighted_values(jnp.maximum(t - 1, 0), 1 - cur)
        m_new = jnp.maximum(m_old, mx)
        p_ref[cur] = jnp.exp2(s_ref[cur] - m_new).astype(BF16)
        return m_new, jnp.exp2(m_old - m_new), mx_next

    def body(j, carry):
        return step(2 * j + 1, 1, step(2 * j, 0, carry))

    acc_ref[...] = jnp.zeros(acc_ref.shape, F32)
    p_ref[1] = jnp.zeros(p_ref.shape[1:], BF16)
    s_first = scores(0)
    s_ref[0] = s_first
    init = (jnp.full((1, n), -jnp.inf, F32), jnp.ones((1, n), F32), jnp.max(s_first, axis=0, keepdims=True))
    _, a_last, _ = lax.fori_loop(0, nt // 2, body, init)
    acc = a_last * acc_ref[...] + weighted_values(nt - 1, 1)
    out = acc[0:hd, :] / acc[hd:hd + 1, :]
    for g in range(grp):
        o_ref[:, g * hd:(g + 1) * hd] = out[:, g * tq:(g + 1) * tq].T.astype(o_ref.dtype)


def _attention(qt, k, vt, hd, tq, tk):
    qw, s = qt.shape
    kvh = k.shape[1] // hd
    gw = qw // kvh
    return pl.pallas_call(
        functools.partial(_attn_kernel, tk),
        out_shape=jax.ShapeDtypeStruct((s, qw), BF16),
        grid=(kvh, s // tq),
        in_specs=[pl.BlockSpec((gw, tq), lambda h, i: (h, i)),
                  pl.BlockSpec((s, hd), lambda h, i: (0, h)),
                  pl.BlockSpec((hd, s), lambda h, i: (h, 0))],
        out_specs=pl.BlockSpec((tq, gw), lambda h, i: (i, h)),
        scratch_shapes=[pltpu.VMEM((hd + ATTN_SUM_ROWS, gw // hd * tq), F32),
                        pltpu.VMEM((2, tk, gw // hd * tq), F32),
                        pltpu.VMEM((2, tk, gw // hd * tq), BF16)],
        compiler_params=_cparams(("parallel", "parallel")),
        name="gqa_attention",
    )(qt, k, vt)


def _odd_tail_kernel(alpha, yc_ref, up_ref, u_ref, un_ref, x_ref, cw_ref, cb_ref, cg_ref, cbeta_ref,
                     wo_ref, lg_ref, lb_ref, o_ref, ext_ref, conv_ref):
    t = x_ref.shape[0]
    mix_w = u_ref.shape[1]
    ext = _halo_ext(up_ref, u_ref, un_ref)
    rows = ext.shape[0]
    ext_ref[0] = ext
    for r in range(1, SUBLANES):
        ext_ref[r] = pltpu.roll(ext, rows - r, axis=0)
    base = HALO - CONV_W // 2
    cb = cb_ref[...]
    groups = CONV_ROWS // SUBLANES
    for c in range(t // CONV_ROWS):
        acc = jnp.zeros((groups, SUBLANES, mix_w), F32)
        for j in range(CONV_W):
            r = (base + j) % SUBLANES
            start = c * CONV_ROWS + (base + j - r)
            xw = ext_ref[r, start:start + CONV_ROWS, :].reshape(groups, SUBLANES, mix_w)
            acc = acc + xw * cw_ref[j]
        conv_ref[c * CONV_ROWS:(c + 1) * CONV_ROWS, :] = acc.reshape(CONV_ROWS, mix_w) + cb
    conv = conv_ref[...]
    yd = _silu(_layer_norm(conv, cg_ref[...], cbeta_ref[...])).astype(BF16)
    y = _dot(yc_ref[...], _w(wo_ref[0:mix_w, :])) + _dot(yd, _w(wo_ref[mix_w:2 * mix_w, :]))
    o_ref[...] = _layer_norm(alpha * x_ref[...] + y, lg_ref[...], lb_ref[...])


def _odd_tail(y_c, u, x, conv_w, conv_b, conv_g, conv_beta, w_out, layer, ln_g, ln_b, alpha, tm):
    s, d_model = x.shape
    mix_w = u.shape[1]
    hb = tm // HALO
    nhalo = s // HALO
    row = pl.BlockSpec((tm, mix_w), lambda i: (i, 0))
    vec = lambda n: pl.BlockSpec((1, n), lambda i: (0, 0))
    return pl.pallas_call(
        functools.partial(_odd_tail_kernel, alpha),
        out_shape=jax.ShapeDtypeStruct((s, d_model), F32),
        grid=(s // tm,),
        in_specs=[row,
                  pl.BlockSpec((HALO, mix_w), lambda i: (jnp.maximum(i * hb - 1, 0), 0)),
                  row,
                  pl.BlockSpec((HALO, mix_w), lambda i: (jnp.minimum((i + 1) * hb, nhalo - 1), 0)),
                  pl.BlockSpec((tm, d_model), lambda i: (i, 0)),
                  pl.BlockSpec(conv_w.shape, lambda i: (0, 0, 0)),
                  vec(mix_w), vec(mix_w), vec(mix_w),
                  _layer_block(w_out, layer),
                  vec(d_model), vec(d_model)],
        out_specs=pl.BlockSpec((tm, d_model), lambda i: (i, 0)),
        scratch_shapes=[pltpu.VMEM((SUBLANES, tm + 2 * HALO, mix_w), F32), pltpu.VMEM((tm, mix_w), F32)],
        compiler_params=_cparams(("parallel",)),
        name="odd_tail",
    )(y_c, u, u, u, x, conv_w, conv_b, conv_g, conv_beta, w_out, ln_g, ln_b)


def _xattn_kernel(alpha, x_ref, wq_ref, k_ref, v_ref, wo_ref, lg_ref, lb_ref, o_ref):
    d_model = x_ref.shape[1]
    hd = d_model // XA_HEADS
    x = x_ref[...]
    q = (_dot(x.astype(BF16), _w(wq_ref[...])) * (hd ** -0.5)).astype(BF16)
    outs = []
    for h in range(XA_HEADS):
        sl = slice(h * hd, (h + 1) * hd)
        s = _dot_nt(q[:, sl], k_ref[:, sl])
        m = jnp.max(s, axis=-1, keepdims=True)
        p = jnp.exp(s - m)
        l = jnp.sum(p, axis=-1, keepdims=True)
        outs.append(_dot(p.astype(BF16), v_ref[:, sl]) / l)
    o = jnp.concatenate(outs, axis=-1).astype(BF16)
    y = _dot(o, _w(wo_ref[...]))
    o_ref[...] = _layer_norm(alpha * x + y, lg_ref[...], lb_ref[...])


def _xattn(x, wq, kv_bf, wo, layer, ln_g, ln_b, alpha, tm):
    s, d_model = x.shape
    vec = pl.BlockSpec((1, d_model), lambda i: (0, 0))
    return pl.pallas_call(
        functools.partial(_xattn_kernel, alpha),
        out_shape=jax.ShapeDtypeStruct((s, d_model), F32),
        grid=(s // tm,),
        in_specs=[pl.BlockSpec((tm, d_model), lambda i: (i, 0)),
                  _layer_block(wq, layer),
                  pl.BlockSpec((kv_bf.shape[0], d_model), lambda i: (0, 0)),
                  pl.BlockSpec((kv_bf.shape[0], d_model), lambda i: (0, 1)),
                  _layer_block(wo, layer), vec, vec],
        out_specs=pl.BlockSpec((tm, d_model), lambda i: (i, 0)),
        compiler_params=_cparams(("parallel",)),
        name="mem_xattn",
    )(x, wq, kv_bf, kv_bf, wo, ln_g, ln_b)


def _ffn_kernel(ck, alpha, x_ref, wg_ref, wu_ref, wd_ref, lg_ref, lb_ref, o_ref):
    d_ff = wg_ref.shape[1]
    x = x_ref[...]
    xb = x.astype(BF16)
    y = jnp.zeros(x.shape, F32)
    for c in range(d_ff // ck):
        sl = slice(c * ck, (c + 1) * ck)
        hcn = _silu(_dot(xb, _w(wg_ref[:, sl]))) * _dot(xb, _w(wu_ref[:, sl]))
        y = y + _dot(hcn.astype(BF16), _w(wd_ref[sl, :]))
    o_ref[...] = _layer_norm(alpha * x + y, lg_ref[...], lb_ref[...])


def _ffn(x, w_gu, w_down, layer, ln_g, ln_b, alpha, tm, ck):
    s, d_model = x.shape
    d_ff = w_down.shape[1]
    assert d_ff % ck == 0, (d_ff, ck)
    vec = pl.BlockSpec((1, d_model), lambda i: (0, 0))
    return pl.pallas_call(
        functools.partial(_ffn_kernel, ck, alpha),
        out_shape=jax.ShapeDtypeStruct((s, d_model), F32),
        grid=(s // tm,),
        in_specs=[pl.BlockSpec((tm, d_model), lambda i: (i, 0)),
                  _layer_block(w_gu, layer, d_ff, 0), _layer_block(w_gu, layer, d_ff, 1),
                  _layer_block(w_down, layer), vec, vec],
        out_specs=pl.BlockSpec((tm, d_model), lambda i: (i, 0)),
        compiler_params=_cparams(("parallel",)),
        name="swiglu_ffn",
    )(x, w_gu, w_gu, w_down, ln_g, ln_b)


def _rope_tables(s, hd):
    rows = s // GRID_W
    row = np.repeat(np.arange(rows), GRID_W)
    col = np.tile(np.arange(GRID_W), rows)
    half = hd // 2
    freqs = ROPE_THETA ** (-np.arange(0, half, 2, dtype=np.float64) / half)

    def ang(p):
        a = p.astype(np.float64)[:, None] * freqs[None, :]
        return np.concatenate([a, a], axis=-1)

    angles = np.concatenate([ang(row), ang(col)], axis=-1)
    cos, sin = np.cos(angles), np.sin(angles)
    lo = (np.arange(hd) % (hd // 2)) < (hd // 4)
    as_f32 = lambda a: jnp.asarray(a.astype(np.float32))
    return as_f32(cos), as_f32(np.where(lo, -sin, 0.0)), as_f32(np.where(lo, 0.0, sin))


def kernel(x, mem, w_in_ab, hgrn_lb_logits, hgrn_norm_g, pool_w, pool_scale, w_out_ab, w_in_cd, q_norm_g, k_norm_g, conv_w, conv_b, conv_ln_g, conv_ln_b, w_out_cd, xa_wq, xa_wkv, xa_wo, ffn_w_gu, ffn_w_down, ln_g, ln_b):
    depth = xa_wq.shape[0]
    alpha = (2 * depth) ** 0.25
    bsz, s, d_model = x.shape
    mix_w = d_model // 2
    hd_c = mix_w // C_HEADS
    kv_w = C_KV_HEADS * hd_c
    tm = min(ROW_TILE, s)
    tq, tk, txa = min(ATTN_TQ, s), min(ATTN_TK, s), min(XA_ROW_TILE, s)
    assert s % tm == 0 and s % HG_BLOCK == 0 and s % GRID_W == 0 and tm % HALO == 0 and tm % CONV_ROWS == 0
    assert s % tq == 0 and s % txa == 0 and (s // tk) % 2 == 0 and s % tk == 0, (s, tq, tk, txa)
    assert d_model % (2 * A_HEADS * LANES) == 0 and hd_c % LANES == 0, d_model

    cum = jnp.cumsum(jax.nn.softmax(hgrn_lb_logits.astype(F32), axis=1), axis=1)
    lb = jnp.maximum(cum - cum[:, :1], 0.0)
    cos, sin_lo, sin_hi = _rope_tables(s, hd_c)
    row = lambda a: a.reshape(1, -1)

    outs = []
    for bi in range(bsz):
        xb = x[bi]
        memb = mem[bi]
        for l in range(depth):
            j = l // 2
            if l % 2 == 0:
                h = _proj(xb, w_in_ab, j, tm, F32, "even_in")
                o_f, o_b = _hgrn(h, row(lb[0, l]), row(lb[1, l]), mix_w)
                xb = _even_tail(o_f, o_b, h, xb, row(hgrn_norm_g[j]), pool_w, row(pool_scale[j]),
                                w_out_ab, j, row(ln_g[l, 0]), row(ln_b[l, 0]), alpha, tm)
            else:
                q, k, v, u = _odd_in(xb, w_in_cd, j, row(q_norm_g[j]), row(k_norm_g[j]),
                                     cos, sin_lo, sin_hi, mix_w, kv_w, tm)
                y_c = _attention(q, k, v, hd_c, tq, tk)
                taps = jnp.broadcast_to(conv_w[j][:, None, :], (conv_w.shape[1], SUBLANES, mix_w))
                xb = _odd_tail(y_c, u, xb, taps, row(conv_b[j]), row(conv_ln_g[j]), row(conv_ln_b[j]),
                               w_out_cd, j, row(ln_g[l, 0]), row(ln_b[l, 0]), alpha, tm)
            kv = _proj(memb, xa_wkv, l, memb.shape[0], BF16, "mem_kv")
            xb = _xattn(xb, xa_wq, kv, xa_wo, l, row(ln_g[l, 1]), row(ln_b[l, 1]), alpha, txa)
            xb = _ffn(xb, ffn_w_gu, ffn_w_down, l, row(ln_g[l, 2]), row(ln_b[l, 2]), alpha, tm, FFN_CHUNK)
        outs.append(xb)
    return jnp.stack(outs, axis=0)
```

```python
import functools
import math

import jax
import jax.numpy as jnp
import numpy as np
from jax import lax
from jax.experimental import pallas as pl
from jax.experimental.pallas import tpu as pltpu

F32 = jnp.float32
BF16 = jnp.bfloat16

A_HEADS = 4
POOL_WINDOWS = (2, 4, 8, 16)
C_HEADS = 4
C_KV_HEADS = 2
GRID_W = 64
ROPE_THETA = 10000.0
CONV_W = 31
XA_HEADS = 4
EPS = 1e-6
LOG2E = math.log2(math.e)

LANES = 128
SUBLANES = 8
VMEM_LIMIT = 56 * 1024 * 1024

ROW_TILE = 512
HG_BLOCK = 128
HG_SAFE_SPAN = 80.0
ATTN_TQ = 2048
ATTN_TK = 512
FFN_CHUNK = 256
XA_ROW_TILE = 1024
CONV_ROWS = 32
HALO = 16
ATTN_SUM_ROWS = 16


def _cparams(sem):
    return pltpu.CompilerParams(dimension_semantics=sem, vmem_limit_bytes=VMEM_LIMIT)


def _silu(x):
    return x * jax.nn.sigmoid(x)


def _layer_norm(y, g, b):
    mu = jnp.mean(y, axis=-1, keepdims=True)
    d = y - mu
    var = jnp.mean(d * d, axis=-1, keepdims=True)
    return d * lax.rsqrt(var + EPS) * g + b


def _dot(a, b):
    return jnp.dot(a, b, preferred_element_type=F32)


def _w(w):
    return w.astype(BF16)


def _dot_nt(a, b):
    return lax.dot_general(a, b, (((1,), (1,)), ((), ())), preferred_element_type=F32)


def _layer_block(stack, layer, cols=None, col_block=0):
    shape = tuple(stack.shape[1:])
    if cols is not None:
        shape = shape[:-1] + (cols,)
    index = (layer,) + (0,) * (len(shape) - 1) + (col_block,)
    return pl.BlockSpec((None,) + shape, lambda *_: index, pipeline_mode=pl.Buffered(1))


def _proj_kernel(x_ref, w_ref, o_ref):
    o_ref[...] = _dot(x_ref[...].astype(BF16), _w(w_ref[...])).astype(o_ref.dtype)


def _proj(x, w_stack, layer, tm, out_dtype, name):
    m, k = x.shape
    n = w_stack.shape[2]
    return pl.pallas_call(
        _proj_kernel,
        out_shape=jax.ShapeDtypeStruct((m, n), out_dtype),
        grid=(m // tm,),
        in_specs=[pl.BlockSpec((tm, k), lambda i: (i, 0)), _layer_block(w_stack, layer)],
        out_specs=pl.BlockSpec((tm, n), lambda i: (i, 0)),
        compiler_params=_cparams(("parallel",)),
        name=name,
    )(x, w_stack)


def _hgrn_direction(q_ref, v_ref, z_ref, lb, lv_ref, st_ref, o_ref, reverse):
    n = HG_BLOCK
    w = q_ref.shape[1]
    hd = w // A_HEADS
    z = z_ref[...]
    qs = _silu(q_ref[...])
    v = v_ref[...]
    t = jnp.exp(-jnp.abs(z))
    k = (1.0 - lb) * (jnp.where(z > 0, t, 1.0) / (1.0 + t))
    log_lb = jnp.log(lb)
    c = jnp.log1p(-lb) + (jnp.minimum(z, 0.0) - jnp.log(1.0 + t))
    g = jnp.maximum(log_lb, c) + jnp.log(1.0 + jnp.exp(-jnp.abs(log_lb - c)))

    rowv = lax.broadcasted_iota(jnp.int32, (n, 1), 0)
    tri = jnp.where(lv_ref[...] >= 0, 1.0, 0.0).astype(BF16)
    g1 = g.astype(BF16)
    r1 = g - g1.astype(F32)
    g2 = r1.astype(BF16)
    g3 = (r1 - g2.astype(F32)).astype(BF16)
    yield
    b = _dot(tri, g1) + _dot(tri, g2) + _dot(tri, g3)
    yield

    end = 0 if reverse else n - 1
    b_end = b[end:end + 1, :]
    qe = (qs * jnp.exp(b)).astype(BF16)
    kd = (k * jnp.exp(b_end - b)).astype(BF16)
    dec = jnp.exp(b_end)
    yield

    lv = lv_ref[...]
    heads = [slice(h * hd, (h + 1) * hd) for h in range(A_HEADS)]
    d_mid = b - b[n // 2:n // 2 + 1, :]

    def single_reference(_):
        ql = (qs * jnp.exp2(d_mid * LOG2E)).astype(BF16)
        kl = (k * jnp.exp2(d_mid * -LOG2E)).astype(BF16)
        interacts = lv >= 0
        return [jnp.where(interacts, _dot_nt(ql[:, sl], kl[:, sl]), 0.0) for sl in heads]

    def by_levels(_):
        ng = n // SUBLANES
        b3 = b.reshape(ng, SUBLANES, w)
        sub3 = lax.broadcasted_iota(jnp.int32, (ng, SUBLANES, 1), 1)
        qs_bf = qs.astype(BF16)
        k_bf = k.astype(BF16)
        lvl = [(qs_bf, k_bf, 0)]
        half = n // 2
        while half >= 1:
            two = 2 * half
            off = half if reverse else half - 1
            if half >= SUBLANES:
                parts = [jnp.broadcast_to(b[blk * two + off:blk * two + off + 1, :], (two, w))
                         for blk in range(n // two)]
                bref = jnp.concatenate(parts, axis=0) if len(parts) > 1 else parts[0]
            else:
                bref3 = jnp.broadcast_to(b3[:, off:off + 1, :], (ng, SUBLANES, w))
                for blk in range(1, SUBLANES // two):
                    r = blk * two + off
                    bref3 = jnp.where(sub3 >= blk * two,
                                      jnp.broadcast_to(b3[:, r:r + 1, :], (ng, SUBLANES, w)), bref3)
                bref = bref3.reshape(n, w)
            second = (rowv % two) >= half
            q_side = jnp.logical_not(second) if reverse else second
            e = jnp.exp2((b - bref) * jnp.where(q_side, LOG2E, -LOG2E)).astype(BF16)
            lvl.append((qs_bf * e, k_bf * e, half))
            half //= 2
        sms = [jnp.zeros((n, n), F32) for _ in heads]
        for ql, kl, level_id in lvl:
            mask = lv == level_id
            sms = [jnp.where(mask, _dot_nt(ql[:, sl], kl[:, sl]), sm) for sl, sm in zip(heads, sms)]
        return sms

    reach = jnp.max(jnp.abs(d_mid)) + jnp.log(jnp.maximum(jnp.max(jnp.abs(qs)), 1.0))
    sms = lax.cond(reach < HG_SAFE_SPAN, single_reference, by_levels, None)
    yield

    for h, (sl, sm) in enumerate(zip(heads, sms)):
        st = st_ref[h]
        vh = v[:, sl]
        o = _dot_nt(qe[:, sl], st.astype(BF16)) + _dot(sm.astype(BF16), vh.astype(BF16))
        o_ref[:, sl] = o
        st_ref[h] = st * dec[:, sl] + _dot(vh.T.astype(BF16), kd[:, sl])
        yield


_DONE = object()


def _hgrn_kernel(lbf_ref, lbb_ref, lvf_ref, lvb_ref, qf_ref, vf_ref, zf_ref, qb_ref, vb_ref, zb_ref,
                 of_ref, ob_ref, stf_ref, stb_ref):
    @pl.when(pl.program_id(0) == 0)
    def _():
        stf_ref[...] = jnp.zeros_like(stf_ref)
        stb_ref[...] = jnp.zeros_like(stb_ref)

    pending = [_hgrn_direction(qf_ref, vf_ref, zf_ref, lbf_ref[...], lvf_ref, stf_ref, of_ref, False),
               _hgrn_direction(qb_ref, vb_ref, zb_ref, lbb_ref[...], lvb_ref, stb_ref, ob_ref, True)]
    while pending:
        pending = [d for d in pending if next(d, _DONE) is not _DONE]


def _hgrn_level_tables(n):
    t, s = np.indices((n, n))
    x = t ^ s
    half = np.where(x > 0, 1 << (np.floor(np.log2(np.maximum(x, 1))).astype(np.int64)), 0)
    fwd = np.where(s <= t, half, -1).astype(np.int32)
    bwd = np.where(s >= t, half, -1).astype(np.int32)
    return jnp.asarray(fwd), jnp.asarray(bwd)


def _hgrn(h, lb_f, lb_b, mix_w):
    s = h.shape[0]
    n = HG_BLOCK
    nb = s // n
    hd = mix_w // A_HEADS
    fwd = lambda c: pl.BlockSpec((n, mix_w), lambda i: (i, c))
    bwd = lambda c: pl.BlockSpec((n, mix_w), lambda i: (nb - 1 - i, c))
    vec = pl.BlockSpec((1, mix_w), lambda i: (0, 0))
    lvs = pl.BlockSpec((n, n), lambda i: (0, 0))
    lv_f, lv_b = _hgrn_level_tables(n)
    return pl.pallas_call(
        _hgrn_kernel,
        out_shape=(jax.ShapeDtypeStruct((s, mix_w), F32), jax.ShapeDtypeStruct((s, mix_w), F32)),
        grid=(nb,),
        in_specs=[vec, vec, lvs, lvs, fwd(0), fwd(1), fwd(2), bwd(0), bwd(1), bwd(3)],
        out_specs=(pl.BlockSpec((n, mix_w), lambda i: (i, 0)),
                   pl.BlockSpec((n, mix_w), lambda i: (nb - 1 - i, 0))),
        scratch_shapes=[pltpu.VMEM((A_HEADS, hd, hd), F32), pltpu.VMEM((A_HEADS, hd, hd), F32)],
        compiler_params=_cparams(("arbitrary",)),
        name="hgrn_scan",
    )(lb_f, lb_b, lv_f, lv_b, h, h, h, h, h, h)


def _halo_ext(prev_ref, cur_ref, next_ref):
    i = pl.program_id(0)
    last = pl.num_programs(0) - 1
    prev = jnp.where(i == 0, 0.0, prev_ref[...])
    nxt = jnp.where(i == last, 0.0, next_ref[...])
    return jnp.concatenate([prev, cur_ref[...], nxt], axis=0)


def _even_tail_kernel(seq_len, alpha, of_ref, ob_ref, og_ref, up_ref, u_ref, un_ref, x_ref,
                      ng_ref, pw_ref, ps_ref, wo_ref, lg_ref, lb_ref, o_ref):
    t = x_ref.shape[0]
    mix_w = of_ref.shape[1]
    hd = mix_w // A_HEADS
    o = of_ref[...] + ob_ref[...]
    gate = _silu(og_ref[...])
    ng = ng_ref[...]
    ya = []
    for h in range(A_HEADS):
        sl = slice(h * hd, (h + 1) * hd)
        oh = o[:, sl]
        r = lax.rsqrt(jnp.mean(oh * oh, axis=-1, keepdims=True) + EPS)
        ya.append(oh * r * ng[:, sl] * gate[:, sl])
    ya = jnp.concatenate(ya, axis=-1).astype(BF16)

    ext = _halo_ext(up_ref, u_ref, un_ref)
    rows = ext.shape[0]
    gw = mix_w // len(POOL_WINDOWS)
    tpos = pl.program_id(0) * t + lax.broadcasted_iota(jnp.int32, (t, 1), 0)
    ps = ps_ref[...]
    yb = []
    for gi, win in enumerate(POOL_WINDOWS):
        sl = slice(gi * gw, (gi + 1) * gw)
        e = ext[:, sl]
        acc = e + pltpu.roll(e, 1, axis=0)
        span = 2
        while span < win:
            sh = span // 2
            acc = pltpu.roll(acc, rows - sh, axis=0) + pltpu.roll(acc, sh, axis=0)
            span *= 2
        wsum = acc[HALO:HALO + t, :]
        lo = jnp.maximum(tpos - win // 2, 0)
        hi = jnp.minimum(tpos - win // 2 + win - 1, seq_len - 1)
        cnt = (hi - lo + 1).astype(F32)
        d = wsum / cnt - e[HALO:HALO + t, :]
        yb.append(_dot(d.astype(BF16), _w(pw_ref[gi])) * ps[:, sl])
    yb = jnp.concatenate(yb, axis=-1).astype(BF16)

    y = _dot(ya, _w(wo_ref[0:mix_w, :])) + _dot(yb, _w(wo_ref[mix_w:2 * mix_w, :]))
    o_ref[...] = _layer_norm(alpha * x_ref[...] + y, lg_ref[...], lb_ref[...])


def _even_tail(o_f, o_b, h, x, norm_g, pool_w, pool_scale, w_out, layer, ln_g, ln_b, alpha, tm):
    s, d_model = x.shape
    mix_w = o_f.shape[1]
    hb = tm // HALO
    nhalo = s // HALO
    row = lambda c: pl.BlockSpec((tm, mix_w), lambda i: (i, c))
    vec = lambda n: pl.BlockSpec((1, n), lambda i: (0, 0))
    return pl.pallas_call(
        functools.partial(_even_tail_kernel, s, alpha),
        out_shape=jax.ShapeDtypeStruct((s, d_model), F32),
        grid=(s // tm,),
        in_specs=[row(0), row(0), row(4),
                  pl.BlockSpec((HALO, mix_w), lambda i: (jnp.maximum(i * hb - 1, 0), 5)),
                  row(5),
                  pl.BlockSpec((HALO, mix_w), lambda i: (jnp.minimum((i + 1) * hb, nhalo - 1), 5)),
                  pl.BlockSpec((tm, d_model), lambda i: (i, 0)),
                  vec(mix_w),
                  _layer_block(pool_w, layer),
                  vec(mix_w),
                  _layer_block(w_out, layer),
                  vec(d_model), vec(d_model)],
        out_specs=pl.BlockSpec((tm, d_model), lambda i: (i, 0)),
        compiler_params=_cparams(("parallel",)),
        name="even_tail",
    )(o_f, o_b, h, h, h, h, x, norm_g, pool_w, pool_scale, w_out, ln_g, ln_b)


def _rope(x, cos, sin_lo, sin_hi):
    n = x.shape[-1]
    r = n // 4
    return x * cos + pltpu.roll(x, n - r, axis=1) * sin_lo + pltpu.roll(x, r, axis=1) * sin_hi


def _odd_in_kernel(q_scale, x_ref, w_ref, qg_ref, kg_ref, cos_ref, sl_ref, sh_ref,
                   q_ref, k_ref, v_ref, u_ref):
    hd = cos_ref.shape[1]
    qw = q_ref.shape[0]
    kw = k_ref.shape[1]
    mix_w = u_ref.shape[1]
    h = _dot(x_ref[...].astype(BF16), _w(w_ref[...]))
    cos, s_lo, s_hi = cos_ref[...], sl_ref[...], sh_ref[...]

    def norm_rope(a, g):
        r = lax.rsqrt(jnp.mean(a * a, axis=-1, keepdims=True) + EPS)
        return _rope(a * r * g, cos, s_lo, s_hi)

    qg = qg_ref[...]
    kg = kg_ref[...]
    for i in range(qw // hd):
        q_ref[i * hd:(i + 1) * hd, :] = (norm_rope(h[:, i * hd:(i + 1) * hd], qg) * q_scale).T.astype(BF16)
    for i in range(kw // hd):
        k_ref[:, i * hd:(i + 1) * hd] = norm_rope(h[:, qw + i * hd:qw + (i + 1) * hd], kg).astype(BF16)
        v_ref[i * hd:(i + 1) * hd, :] = h[:, qw + kw + i * hd:qw + kw + (i + 1) * hd].T.astype(BF16)
    a0 = qw + 2 * kw
    u_ref[...] = h[:, a0:a0 + mix_w] * jax.nn.sigmoid(h[:, a0 + mix_w:a0 + 2 * mix_w])


def _odd_in(x, w_in, layer, q_g, k_g, cos, sin_lo, sin_hi, mix_w, kv_w, tm):
    s, d_model = x.shape
    hd = cos.shape[1]
    q_scale = hd ** -0.5 * LOG2E
    vec = pl.BlockSpec((1, hd), lambda i: (0, 0))
    tab = pl.BlockSpec((tm, hd), lambda i: (i, 0))
    return pl.pallas_call(
        functools.partial(_odd_in_kernel, q_scale),
        out_shape=(jax.ShapeDtypeStruct((mix_w, s), BF16), jax.ShapeDtypeStruct((s, kv_w), BF16),
                   jax.ShapeDtypeStruct((kv_w, s), BF16), jax.ShapeDtypeStruct((s, mix_w), F32)),
        grid=(s // tm,),
        in_specs=[pl.BlockSpec((tm, d_model), lambda i: (i, 0)),
                  _layer_block(w_in, layer),
                  vec, vec, tab, tab, tab],
        out_specs=(pl.BlockSpec((mix_w, tm), lambda i: (0, i)), pl.BlockSpec((tm, kv_w), lambda i: (i, 0)),
                   pl.BlockSpec((kv_w, tm), lambda i: (0, i)), pl.BlockSpec((tm, mix_w), lambda i: (i, 0))),
        compiler_params=_cparams(("parallel",)),
        name="odd_in",
    )(x, w_in, q_g, k_g, cos, sin_lo, sin_hi)


def _attn_kernel(tk, qt_ref, k_ref, vt_ref, o_ref, acc_ref, s_ref, p_ref):
    hd = k_ref.shape[1]
    grp = qt_ref.shape[0] // hd
    tq = qt_ref.shape[1]
    s_len = k_ref.shape[0]
    n = grp * tq
    qt = jnp.concatenate([qt_ref[g * hd:(g + 1) * hd, :] for g in range(grp)], axis=1)
    nt = s_len // tk

    def scores(t):
        off = pl.multiple_of(t * tk, tk)
        return _dot(k_ref[pl.ds(off, tk), :], qt)

    ones_rows = jnp.ones((ATTN_SUM_ROWS, tk), BF16)

    def weighted_values(t, slot):
        off = pl.multiple_of(t * tk, tk)
        vt = jnp.concatenate([vt_ref[:, pl.ds(off, tk)], ones_rows], axis=0)
        return _dot(vt, p_ref[slot])

    def step(t, cur, carry):
        m_old, a_prev, mx = carry
        s_next = scores(jnp.minimum(t + 1, nt - 1))
        s_ref[1 - cur] = s_next
        mx_next = jnp.max(s_next, axis=0, keepdims=True)
        acc_ref[...] = a_prev * acc_ref[...] + weighted_values(jnp.maximum(t - 1, 0), 1 - cur)
        m_new = jnp.maximum(m_old, mx)
        p_ref[cur] = jnp.exp2(s_ref[cur] - m_new).astype(BF16)
        return m_new, jnp.exp2(m_old - m_new), mx_next

    def body(j, carry):
        return step(2 * j + 1, 1, step(2 * j, 0, carry))

    acc_ref[...] = jnp.zeros(acc_ref.shape, F32)
    p_ref[1] = jnp.zeros(p_ref.shape[1:], BF16)
    s_first = scores(0)
    s_ref[0] = s_first
    init = (jnp.full((1, n), -jnp.inf, F32), jnp.ones((1, n), F32), jnp.max(s_first, axis=0, keepdims=True))
    _, a_last, _ = lax.fori_loop(0, nt // 2, body, init)
    acc = a_last * acc_ref[...] + weighted_values(nt - 1, 1)
    out = acc[0:hd, :] / acc[hd:hd + 1, :]
    for g in range(grp):
        o_ref[:, g * hd:(g + 1) * hd] = out[:, g * tq:(g + 1) * tq].T.astype(o_ref.dtype)


def _attention(qt, k, vt, hd, tq, tk):
    qw, s = qt.shape
    kvh = k.shape[1] // hd
    gw = qw // kvh
    return pl.pallas_call(
        functools.partial(_attn_kernel, tk),
        out_shape=jax.ShapeDtypeStruct((s, qw), BF16),
        grid=(kvh, s // tq),
        in_specs=[pl.BlockSpec((gw, tq), lambda h, i: (h, i)),
                  pl.BlockSpec((s, hd), lambda h, i: (0, h)),
                  pl.BlockSpec((hd, s), lambda h, i: (h, 0))],
        out_specs=pl.BlockSpec((tq, gw), lambda h, i: (i, h)),
        scratch_shapes=[pltpu.VMEM((hd + ATTN_SUM_ROWS, gw // hd * tq), F32),
                        pltpu.VMEM((2, tk, gw // hd * tq), F32),
                        pltpu.VMEM((2, tk, gw // hd * tq), BF16)],
        compiler_params=_cparams(("parallel", "parallel")),
        name="gqa_attention",
    )(qt, k, vt)


def _odd_tail_kernel(alpha, yc_ref, up_ref, u_ref, un_ref, x_ref, cw_ref, cb_ref, cg_ref, cbeta_ref,
                     wo_ref, lg_ref, lb_ref, o_ref, ext_ref, conv_ref):
    t = x_ref.shape[0]
    mix_w = u_ref.shape[1]
    ext = _halo_ext(up_ref, u_ref, un_ref)
    rows = ext.shape[0]
    ext_ref[0] = ext
    for r in range(1, SUBLANES):
        ext_ref[r] = pltpu.roll(ext, rows - r, axis=0)
    base = HALO - CONV_W // 2
    cb = cb_ref[...]
    groups = CONV_ROWS // SUBLANES
    for c in range(t // CONV_ROWS):
        acc = jnp.zeros((groups, SUBLANES, mix_w), F32)
        for j in range(CONV_W):
            r = (base + j) % SUBLANES
            start = c * CONV_ROWS + (base + j - r)
            xw = ext_ref[r, start:start + CONV_ROWS, :].reshape(groups, SUBLANES, mix_w)
            acc = acc + xw * cw_ref[j]
        conv_ref[c * CONV_ROWS:(c + 1) * CONV_ROWS, :] = acc.reshape(CONV_ROWS, mix_w) + cb
    conv = conv_ref[...]
    yd = _silu(_layer_norm(conv, cg_ref[...], cbeta_ref[...])).astype(BF16)
    y = _dot(yc_ref[...], _w(wo_ref[0:mix_w, :])) + _dot(yd, _w(wo_ref[mix_w:2 * mix_w, :]))
    o_ref[...] = _layer_norm(alpha * x_ref[...] + y, lg_ref[...], lb_ref[...])


def _odd_tail(y_c, u, x, conv_w, conv_b, conv_g, conv_beta, w_out, layer, ln_g, ln_b, alpha, tm):
    s, d_model = x.shape
    mix_w = u.shape[1]
    hb = tm // HALO
    nhalo = s // HALO
    row = pl.BlockSpec((tm, mix_w), lambda i: (i, 0))
    vec = lambda n: pl.BlockSpec((1, n), lambda i: (0, 0))
    return pl.pallas_call(
        functools.partial(_odd_tail_kernel, alpha),
        out_shape=jax.ShapeDtypeStruct((s, d_model), F32),
        grid=(s // tm,),
        in_specs=[row,
                  pl.BlockSpec((HALO, mix_w), lambda i: (jnp.maximum(i * hb - 1, 0), 0)),
                  row,
                  pl.BlockSpec((HALO, mix_w), lambda i: (jnp.minimum((i + 1) * hb, nhalo - 1), 0)),
                  pl.BlockSpec((tm, d_model), lambda i: (i, 0)),
                  pl.BlockSpec(conv_w.shape, lambda i: (0, 0, 0)),
                  vec(mix_w), vec(mix_w), vec(mix_w),
                  _layer_block(w_out, layer),
                  vec(d_model), vec(d_model)],
        out_specs=pl.BlockSpec((tm, d_model), lambda i: (i, 0)),
        scratch_shapes=[pltpu.VMEM((SUBLANES, tm + 2 * HALO, mix_w), F32), pltpu.VMEM((tm, mix_w), F32)],
        compiler_params=_cparams(("parallel",)),
        name="odd_tail",
    )(y_c, u, u, u, x, conv_w, conv_b, conv_g, conv_beta, w_out, ln_g, ln_b)


def _mem_kv(mem, wkv):
    depth, d_model, n = wkv.shape
    m = mem.shape[0]
    return pl.pallas_call(
        _proj_kernel,
        out_shape=jax.ShapeDtypeStruct((depth, m, n), BF16),
        grid=(depth,),
        in_specs=[pl.BlockSpec((m, d_model), lambda l: (0, 0)),
                  pl.BlockSpec((None, d_model, n), lambda l: (l, 0, 0))],
        out_specs=pl.BlockSpec((None, m, n), lambda l: (l, 0, 0)),
        compiler_params=_cparams(("parallel",)),
        name="mem_kv",
    )(mem, wkv)


def _xattn_kernel(alpha, x_ref, wq_ref, k_ref, v_ref, wo_ref, lg_ref, lb_ref, o_ref):
    d_model = x_ref.shape[1]
    hd = d_model // XA_HEADS
    x = x_ref[...]
    q = (_dot(x.astype(BF16), _w(wq_ref[...])) * (hd ** -0.5)).astype(BF16)
    outs = []
    for h in range(XA_HEADS):
        sl = slice(h * hd, (h + 1) * hd)
        s = _dot_nt(q[:, sl], k_ref[:, sl])
        m = jnp.max(s, axis=-1, keepdims=True)
        p = jnp.exp(s - m)
        l = jnp.sum(p, axis=-1, keepdims=True)
        outs.append(_dot(p.astype(BF16), v_ref[:, sl]) / l)
    o = jnp.concatenate(outs, axis=-1).astype(BF16)
    y = _dot(o, _w(wo_ref[...]))
    o_ref[...] = _layer_norm(alpha * x + y, lg_ref[...], lb_ref[...])


def _xattn(x, wq, kv_bf, wo, layer, ln_g, ln_b, alpha, tm):
    s, d_model = x.shape
    vec = pl.BlockSpec((1, d_model), lambda i: (0, 0))
    return pl.pallas_call(
        functools.partial(_xattn_kernel, alpha),
        out_shape=jax.ShapeDtypeStruct((s, d_model), F32),
        grid=(s // tm,),
        in_specs=[pl.BlockSpec((tm, d_model), lambda i: (i, 0)),
                  _layer_block(wq, layer),
                  _layer_block(kv_bf, layer, d_model, 0), _layer_block(kv_bf, layer, d_model, 1),
                  _layer_block(wo, layer), vec, vec],
        out_specs=pl.BlockSpec((tm, d_model), lambda i: (i, 0)),
        compiler_params=_cparams(("parallel",)),
        name="mem_xattn",
    )(x, wq, kv_bf, kv_bf, wo, ln_g, ln_b)


def _ffn_kernel(ck, alpha, x_ref, wg_ref, wu_ref, wd_ref, lg_ref, lb_ref, o_ref):
    d_ff = wg_ref.shape[1]
    x = x_ref[...]
    xb = x.astype(BF16)
    y = jnp.zeros(x.shape, F32)
    for c in range(d_ff // ck):
        sl = slice(c * ck, (c + 1) * ck)
        hcn = _silu(_dot(xb, _w(wg_ref[:, sl]))) * _dot(xb, _w(wu_ref[:, sl]))
        y = y + _dot(hcn.astype(BF16), _w(wd_ref[sl, :]))
    o_ref[...] = _layer_norm(alpha * x + y, lg_ref[...], lb_ref[...])


def _ffn(x, w_gu, w_down, layer, ln_g, ln_b, alpha, tm, ck):
    s, d_model = x.shape
    d_ff = w_down.shape[1]
    assert d_ff % ck == 0, (d_ff, ck)
    vec = pl.BlockSpec((1, d_model), lambda i: (0, 0))
    return pl.pallas_call(
        functools.partial(_ffn_kernel, ck, alpha),
        out_shape=jax.ShapeDtypeStruct((s, d_model), F32),
        grid=(s // tm,),
        in_specs=[pl.BlockSpec((tm, d_model), lambda i: (i, 0)),
                  _layer_block(w_gu, layer, d_ff, 0), _layer_block(w_gu, layer, d_ff, 1),
                  _layer_block(w_down, layer), vec, vec],
        out_specs=pl.BlockSpec((tm, d_model), lambda i: (i, 0)),
        compiler_params=_cparams(("parallel",)),
        name="swiglu_ffn",
    )(x, w_gu, w_gu, w_down, ln_g, ln_b)


def _rope_tables(s, hd):
    rows = s // GRID_W
    row = np.repeat(np.arange(rows), GRID_W)
    col = np.tile(np.arange(GRID_W), rows)
    half = hd // 2
    freqs = ROPE_THETA ** (-np.arange(0, half, 2, dtype=np.float64) / half)

    def ang(p):
        a = p.astype(np.float64)[:, None] * freqs[None, :]
        return np.concatenate([a, a], axis=-1)

    angles = np.concatenate([ang(row), ang(col)], axis=-1)
    cos, sin = np.cos(angles), np.sin(angles)
    lo = (np.arange(hd) % (hd // 2)) < (hd // 4)
    as_f32 = lambda a: jnp.asarray(a.astype(np.float32))
    return as_f32(cos), as_f32(np.where(lo, -sin, 0.0)), as_f32(np.where(lo, 0.0, sin))


def kernel(x, mem, w_in_ab, hgrn_lb_logits, hgrn_norm_g, pool_w, pool_scale, w_out_ab, w_in_cd, q_norm_g, k_norm_g, conv_w, conv_b, conv_ln_g, conv_ln_b, w_out_cd, xa_wq, xa_wkv, xa_wo, ffn_w_gu, ffn_w_down, ln_g, ln_b):
    depth = xa_wq.shape[0]
    alpha = (2 * depth) ** 0.25
    bsz, s, d_model = x.shape
    mix_w = d_model // 2
    hd_c = mix_w // C_HEADS
    kv_w = C_KV_HEADS * hd_c
    tm = min(ROW_TILE, s)
    tq, tk, txa = min(ATTN_TQ, s), min(ATTN_TK, s), min(XA_ROW_TILE, s)
    assert s % tm == 0 and s % HG_BLOCK == 0 and s % GRID_W == 0 and tm % HALO == 0 and tm % CONV_ROWS == 0
    assert s % tq == 0 and s % txa == 0 and (s // tk) % 2 == 0 and s % tk == 0, (s, tq, tk, txa)
    assert d_model % (2 * A_HEADS * LANES) == 0 and hd_c % LANES == 0, d_model

    cum = jnp.cumsum(jax.nn.softmax(hgrn_lb_logits.astype(F32), axis=1), axis=1)
    lb = jnp.maximum(cum - cum[:, :1], 0.0)
    cos, sin_lo, sin_hi = _rope_tables(s, hd_c)
    row = lambda a: a.reshape(1, -1)

    outs = []
    for bi in range(bsz):
        xb = x[bi]
        kv = _mem_kv(mem[bi], xa_wkv)
        for l in range(depth):
            j = l // 2
            if l % 2 == 0:
                h = _proj(xb, w_in_ab, j, tm, F32, "even_in")
                o_f, o_b = _hgrn(h, row(lb[0, l]), row(lb[1, l]), mix_w)
                xb = _even_tail(o_f, o_b, h, xb, row(hgrn_norm_g[j]), pool_w, row(pool_scale[j]),
                                w_out_ab, j, row(ln_g[l, 0]), row(ln_b[l, 0]), alpha, tm)
            else:
                q, k, v, u = _odd_in(xb, w_in_cd, j, row(q_norm_g[j]), row(k_norm_g[j]),
                                     cos, sin_lo, sin_hi, mix_w, kv_w, tm)
                y_c = _attention(q, k, v, hd_c, tq, tk)
                taps = jnp.broadcast_to(conv_w[j][:, None, :], (conv_w.shape[1], SUBLANES, mix_w))
                xb = _odd_tail(y_c, u, xb, taps, row(conv_b[j]), row(conv_ln_g[j]), row(conv_ln_b[j]),
                               w_out_cd, j, row(ln_g[l, 0]), row(ln_b[l, 0]), alpha, tm)
            xb = _xattn(xb, xa_wq, kv, xa_wo, l, row(ln_g[l, 1]), row(ln_b[l, 1]), alpha, txa)
            xb = _ffn(xb, ffn_w_gu, ffn_w_down, l, row(ln_g[l, 2]), row(ln_b[l, 2]), alpha, tm, FFN_CHUNK)
        outs.append(xb)
    return jnp.stack(outs, axis=0)
```

```python
import functools
import math

import jax
import jax.numpy as jnp
import numpy as np
from jax import lax
from jax.experimental import pallas as pl
from jax.experimental.pallas import tpu as pltpu

F32 = jnp.float32
BF16 = jnp.bfloat16

A_HEADS = 4
POOL_WINDOWS = (2, 4, 8, 16)
C_HEADS = 4
C_KV_HEADS = 2
GRID_W = 64
ROPE_THETA = 10000.0
CONV_W = 31
XA_HEADS = 4
EPS = 1e-6
LOG2E = math.log2(math.e)

LANES = 128
SUBLANES = 8
VMEM_LIMIT = 56 * 1024 * 1024

ROW_TILE = 512
HG_BLOCK = 128
HG_SAFE_SPAN = 80.0
ATTN_TQ = 2048
ATTN_TK = 512
FFN_CHUNK = 256
XA_ROW_TILE = 1024
CONV_ROWS = 32
HALO = 16
ATTN_SUM_ROWS = 16


def _cparams(sem):
    return pltpu.CompilerParams(dimension_semantics=sem, vmem_limit_bytes=VMEM_LIMIT)


def _silu(x):
    return x * jax.nn.sigmoid(x)


def _layer_norm(y, g, b):
    mu = jnp.mean(y, axis=-1, keepdims=True)
    d = y - mu
    var = jnp.mean(d * d, axis=-1, keepdims=True)
    return d * lax.rsqrt(var + EPS) * g + b


def _dot(a, b):
    return jnp.dot(a, b, preferred_element_type=F32)


def _w(w):
    return w.astype(BF16)


def _dot_nt(a, b):
    return lax.dot_general(a, b, (((1,), (1,)), ((), ())), preferred_element_type=F32)


def _layer_block(stack, layer, cols=None, col_block=0):
    shape = tuple(stack.shape[1:])
    if cols is not None:
        shape = shape[:-1] + (cols,)
    index = (layer,) + (0,) * (len(shape) - 1) + (col_block,)
    return pl.BlockSpec((None,) + shape, lambda *_: index, pipeline_mode=pl.Buffered(1))


def _proj_kernel(x_ref, w_ref, o_ref):
    o_ref[...] = _dot(x_ref[...].astype(BF16), _w(w_ref[...])).astype(o_ref.dtype)


def _proj(x, w_stack, layer, tm, out_dtype, name):
    m, k = x.shape
    n = w_stack.shape[2]
    return pl.pallas_call(
        _proj_kernel,
        out_shape=jax.ShapeDtypeStruct((m, n), out_dtype),
        grid=(m // tm,),
        in_specs=[pl.BlockSpec((tm, k), lambda i: (i, 0)), _layer_block(w_stack, layer)],
        out_specs=pl.BlockSpec((tm, n), lambda i: (i, 0)),
        compiler_params=_cparams(("parallel",)),
        name=name,
    )(x, w_stack)


def _hgrn_direction(q_ref, v_ref, z_ref, lb, lv_ref, st_ref, o_ref, reverse):
    n = HG_BLOCK
    w = q_ref.shape[1]
    hd = w // A_HEADS
    z = z_ref[...]
    qs = _silu(q_ref[...])
    v = v_ref[...]
    t = jnp.exp(-jnp.abs(z))
    k = (1.0 - lb) * (jnp.where(z > 0, t, 1.0) / (1.0 + t))
    log_lb = jnp.log(lb)
    c = jnp.log1p(-lb) + (jnp.minimum(z, 0.0) - jnp.log(1.0 + t))
    g = jnp.maximum(log_lb, c) + jnp.log(1.0 + jnp.exp(-jnp.abs(log_lb - c)))

    rowv = lax.broadcasted_iota(jnp.int32, (n, 1), 0)
    tri = jnp.where(lv_ref[...] >= 0, 1.0, 0.0).astype(BF16)
    g1 = g.astype(BF16)
    r1 = g - g1.astype(F32)
    g2 = r1.astype(BF16)
    g3 = (r1 - g2.astype(F32)).astype(BF16)
    yield
    b = _dot(tri, g1) + _dot(tri, g2) + _dot(tri, g3)
    yield

    end = 0 if reverse else n - 1
    b_end = b[end:end + 1, :]
    qe = (qs * jnp.exp(b)).astype(BF16)
    kd = (k * jnp.exp(b_end - b)).astype(BF16)
    dec = jnp.exp(b_end)
    yield

    lv = lv_ref[...]
    heads = [slice(h * hd, (h + 1) * hd) for h in range(A_HEADS)]
    d_mid = b - b[n // 2:n // 2 + 1, :]

    def single_reference(_):
        ql = (qs * jnp.exp2(d_mid * LOG2E)).astype(BF16)
        kl = (k * jnp.exp2(d_mid * -LOG2E)).astype(BF16)
        interacts = lv >= 0
        return [jnp.where(interacts, _dot_nt(ql[:, sl], kl[:, sl]), 0.0) for sl in heads]

    def by_levels(_):
        ng = n // SUBLANES
        b3 = b.reshape(ng, SUBLANES, w)
        sub3 = lax.broadcasted_iota(jnp.int32, (ng, SUBLANES, 1), 1)
        qs_bf = qs.astype(BF16)
        k_bf = k.astype(BF16)
        lvl = [(qs_bf, k_bf, 0)]
        half = n // 2
        while half >= 1:
            two = 2 * half
            off = half if reverse else half - 1
            if half >= SUBLANES:
                parts = [jnp.broadcast_to(b[blk * two + off:blk * two + off + 1, :], (two, w))
                         for blk in range(n // two)]
                bref = jnp.concatenate(parts, axis=0) if len(parts) > 1 else parts[0]
            else:
                bref3 = jnp.broadcast_to(b3[:, off:off + 1, :], (ng, SUBLANES, w))
                for blk in range(1, SUBLANES // two):
                    r = blk * two + off
                    bref3 = jnp.where(sub3 >= blk * two,
                                      jnp.broadcast_to(b3[:, r:r + 1, :], (ng, SUBLANES, w)), bref3)
                bref = bref3.reshape(n, w)
            second = (rowv % two) >= half
            q_side = jnp.logical_not(second) if reverse else second
            e = jnp.exp2((b - bref) * jnp.where(q_side, LOG2E, -LOG2E)).astype(BF16)
            lvl.append((qs_bf * e, k_bf * e, half))
            half //= 2
        sms = [jnp.zeros((n, n), F32) for _ in heads]
        for ql, kl, level_id in lvl:
            mask = lv == level_id
            sms = [jnp.where(mask, _dot_nt(ql[:, sl], kl[:, sl]), sm) for sl, sm in zip(heads, sms)]
        return sms

    reach = jnp.max(jnp.abs(d_mid)) + jnp.log(jnp.maximum(jnp.max(jnp.abs(qs)), 1.0))
    sms = yield (reach, single_reference, by_levels)

    for h, (sl, sm) in enumerate(zip(heads, sms)):
        st = st_ref[h]
        vh = v[:, sl]
        o = _dot_nt(qe[:, sl], st.astype(BF16)) + _dot(sm.astype(BF16), vh.astype(BF16))
        o_ref[:, sl] = o
        st_ref[h] = st * dec[:, sl] + _dot(vh.T.astype(BF16), kd[:, sl])
        yield


_DONE = object()


def _hgrn_kernel(lbf_ref, lbb_ref, lvf_ref, lvb_ref, qf_ref, vf_ref, zf_ref, qb_ref, vb_ref, zb_ref,
                 of_ref, ob_ref, stf_ref, stb_ref):
    @pl.when(pl.program_id(0) == 0)
    def _():
        stf_ref[...] = jnp.zeros_like(stf_ref)
        stb_ref[...] = jnp.zeros_like(stb_ref)

    pending = [_hgrn_direction(qf_ref, vf_ref, zf_ref, lbf_ref[...], lvf_ref, stf_ref, of_ref, False),
               _hgrn_direction(qb_ref, vb_ref, zb_ref, lbb_ref[...], lvb_ref, stb_ref, ob_ref, True)]
    values = [next(d) for d in pending]
    while pending:
        if all(v is not None for v in values):
            reaches, single, levels = zip(*values)
            safe = functools.reduce(jnp.logical_and, [r < HG_SAFE_SPAN for r in reaches])
            decay = lax.cond(safe, lambda _: [f(None) for f in single], lambda _: [f(None) for f in levels], None)
            values = [d.send(sms) for d, sms in zip(pending, decay)]
        else:
            values = [next(d, _DONE) for d in pending]
        alive = [(d, v) for d, v in zip(pending, values) if v is not _DONE]
        pending, values = [d for d, _ in alive], [v for _, v in alive]


def _hgrn_level_tables(n):
    t, s = np.indices((n, n))
    x = t ^ s
    half = np.where(x > 0, 1 << (np.floor(np.log2(np.maximum(x, 1))).astype(np.int64)), 0)
    fwd = np.where(s <= t, half, -1).astype(np.int32)
    bwd = np.where(s >= t, half, -1).astype(np.int32)
    return jnp.asarray(fwd), jnp.asarray(bwd)


def _hgrn(h, lb_f, lb_b, mix_w):
    s = h.shape[0]
    n = HG_BLOCK
    nb = s // n
    hd = mix_w // A_HEADS
    fwd = lambda c: pl.BlockSpec((n, mix_w), lambda i: (i, c))
    bwd = lambda c: pl.BlockSpec((n, mix_w), lambda i: (nb - 1 - i, c))
    vec = pl.BlockSpec((1, mix_w), lambda i: (0, 0))
    lvs = pl.BlockSpec((n, n), lambda i: (0, 0))
    lv_f, lv_b = _hgrn_level_tables(n)
    return pl.pallas_call(
        _hgrn_kernel,
        out_shape=(jax.ShapeDtypeStruct((s, mix_w), F32), jax.ShapeDtypeStruct((s, mix_w), F32)),
        grid=(nb,),
        in_specs=[vec, vec, lvs, lvs, fwd(0), fwd(1), fwd(2), bwd(0), bwd(1), bwd(3)],
        out_specs=(pl.BlockSpec((n, mix_w), lambda i: (i, 0)),
                   pl.BlockSpec((n, mix_w), lambda i: (nb - 1 - i, 0))),
        scratch_shapes=[pltpu.VMEM((A_HEADS, hd, hd), F32), pltpu.VMEM((A_HEADS, hd, hd), F32)],
        compiler_params=_cparams(("arbitrary",)),
        name="hgrn_scan",
    )(lb_f, lb_b, lv_f, lv_b, h, h, h, h, h, h)


def _halo_ext(prev_ref, cur_ref, next_ref):
    i = pl.program_id(0)
    last = pl.num_programs(0) - 1
    prev = jnp.where(i == 0, 0.0, prev_ref[...])
    nxt = jnp.where(i == last, 0.0, next_ref[...])
    return jnp.concatenate([prev, cur_ref[...], nxt], axis=0)


def _even_tail_kernel(seq_len, alpha, of_ref, ob_ref, og_ref, up_ref, u_ref, un_ref, x_ref,
                      ng_ref, pw_ref, ps_ref, wo_ref, lg_ref, lb_ref, o_ref):
    t = x_ref.shape[0]
    mix_w = of_ref.shape[1]
    hd = mix_w // A_HEADS
    o = of_ref[...] + ob_ref[...]
    gate = _silu(og_ref[...])
    ng = ng_ref[...]
    ya = []
    for h in range(A_HEADS):
        sl = slice(h * hd, (h + 1) * hd)
        oh = o[:, sl]
        r = lax.rsqrt(jnp.mean(oh * oh, axis=-1, keepdims=True) + EPS)
        ya.append(oh * r * ng[:, sl] * gate[:, sl])
    ya = jnp.concatenate(ya, axis=-1).astype(BF16)

    ext = _halo_ext(up_ref, u_ref, un_ref)
    rows = ext.shape[0]
    gw = mix_w // len(POOL_WINDOWS)
    tpos = pl.program_id(0) * t + lax.broadcasted_iota(jnp.int32, (t, 1), 0)
    ps = ps_ref[...]
    yb = []
    for gi, win in enumerate(POOL_WINDOWS):
        sl = slice(gi * gw, (gi + 1) * gw)
        e = ext[:, sl]
        acc = e + pltpu.roll(e, 1, axis=0)
        span = 2
        while span < win:
            sh = span // 2
            acc = pltpu.roll(acc, rows - sh, axis=0) + pltpu.roll(acc, sh, axis=0)
            span *= 2
        wsum = acc[HALO:HALO + t, :]
        lo = jnp.maximum(tpos - win // 2, 0)
        hi = jnp.minimum(tpos - win // 2 + win - 1, seq_len - 1)
        cnt = (hi - lo + 1).astype(F32)
        d = wsum / cnt - e[HALO:HALO + t, :]
        yb.append(_dot(d.astype(BF16), _w(pw_ref[gi])) * ps[:, sl])
    yb = jnp.concatenate(yb, axis=-1).astype(BF16)

    y = _dot(ya, _w(wo_ref[0:mix_w, :])) + _dot(yb, _w(wo_ref[mix_w:2 * mix_w, :]))
    o_ref[...] = _layer_norm(alpha * x_ref[...] + y, lg_ref[...], lb_ref[...])


def _even_tail(o_f, o_b, h, x, norm_g, pool_w, pool_scale, w_out, layer, ln_g, ln_b, alpha, tm):
    s, d_model = x.shape
    mix_w = o_f.shape[1]
    hb = tm // HALO
    nhalo = s // HALO
    row = lambda c: pl.BlockSpec((tm, mix_w), lambda i: (i, c))
    vec = lambda n: pl.BlockSpec((1, n), lambda i: (0, 0))
    return pl.pallas_call(
        functools.partial(_even_tail_kernel, s, alpha),
        out_shape=jax.ShapeDtypeStruct((s, d_model), F32),
        grid=(s // tm,),
        in_specs=[row(0), row(0), row(4),
                  pl.BlockSpec((HALO, mix_w), lambda i: (jnp.maximum(i * hb - 1, 0), 5)),
                  row(5),
                  pl.BlockSpec((HALO, mix_w), lambda i: (jnp.minimum((i + 1) * hb, nhalo - 1), 5)),
                  pl.BlockSpec((tm, d_model), lambda i: (i, 0)),
                  vec(mix_w),
                  _layer_block(pool_w, layer),
                  vec(mix_w),
                  _layer_block(w_out, layer),
                  vec(d_model), vec(d_model)],
        out_specs=pl.BlockSpec((tm, d_model), lambda i: (i, 0)),
        compiler_params=_cparams(("parallel",)),
        name="even_tail",
    )(o_f, o_b, h, h, h, h, x, norm_g, pool_w, pool_scale, w_out, ln_g, ln_b)


def _rope(x, cos, sin_lo, sin_hi):
    n = x.shape[-1]
    r = n // 4
    return x * cos + pltpu.roll(x, n - r, axis=1) * sin_lo + pltpu.roll(x, r, axis=1) * sin_hi


def _odd_in_kernel(q_scale, x_ref, w_ref, qg_ref, kg_ref, cos_ref, sl_ref, sh_ref,
                   q_ref, k_ref, v_ref, u_ref):
    hd = cos_ref.shape[1]
    qw = q_ref.shape[0]
    kw = k_ref.shape[1]
    mix_w = u_ref.shape[1]
    h = _dot(x_ref[...].astype(BF16), _w(w_ref[...]))
    cos, s_lo, s_hi = cos_ref[...], sl_ref[...], sh_ref[...]

    def norm_rope(a, g):
        r = lax.rsqrt(jnp.mean(a * a, axis=-1, keepdims=True) + EPS)
        return _rope(a * r * g, cos, s_lo, s_hi)

    qg = qg_ref[...]
    kg = kg_ref[...]
    for i in range(qw // hd):
        q_ref[i * hd:(i + 1) * hd, :] = (norm_rope(h[:, i * hd:(i + 1) * hd], qg) * q_scale).T.astype(BF16)
    for i in range(kw // hd):
        k_ref[:, i * hd:(i + 1) * hd] = norm_rope(h[:, qw + i * hd:qw + (i + 1) * hd], kg).astype(BF16)
        v_ref[i * hd:(i + 1) * hd, :] = h[:, qw + kw + i * hd:qw + kw + (i + 1) * hd].T.astype(BF16)
    a0 = qw + 2 * kw
    u_ref[...] = h[:, a0:a0 + mix_w] * jax.nn.sigmoid(h[:, a0 + mix_w:a0 + 2 * mix_w])


def _odd_in(x, w_in, layer, q_g, k_g, cos, sin_lo, sin_hi, mix_w, kv_w, tm):
    s, d_model = x.shape
    hd = cos.shape[1]
    q_scale = hd ** -0.5 * LOG2E
    vec = pl.BlockSpec((1, hd), lambda i: (0, 0))
    tab = pl.BlockSpec((tm, hd), lambda i: (i, 0))
    return pl.pallas_call(
        functools.partial(_odd_in_kernel, q_scale),
        out_shape=(jax.ShapeDtypeStruct((mix_w, s), BF16), jax.ShapeDtypeStruct((s, kv_w), BF16),
                   jax.ShapeDtypeStruct((kv_w, s), BF16), jax.ShapeDtypeStruct((s, mix_w), F32)),
        grid=(s // tm,),
        in_specs=[pl.BlockSpec((tm, d_model), lambda i: (i, 0)),
                  _layer_block(w_in, layer),
                  vec, vec, tab, tab, tab],
        out_specs=(pl.BlockSpec((mix_w, tm), lambda i: (0, i)), pl.BlockSpec((tm, kv_w), lambda i: (i, 0)),
                   pl.BlockSpec((kv_w, tm), lambda i: (0, i)), pl.BlockSpec((tm, mix_w), lambda i: (i, 0))),
        compiler_params=_cparams(("parallel",)),
        name="odd_in",
    )(x, w_in, q_g, k_g, cos, sin_lo, sin_hi)


def _attn_kernel(tk, qt_ref, k_ref, vt_ref, o_ref, acc_ref, s_ref, p_ref):
    hd = k_ref.shape[1]
    grp = qt_ref.shape[0] // hd
    tq = qt_ref.shape[1]
    s_len = k_ref.shape[0]
    n = grp * tq
    qt = jnp.concatenate([qt_ref[g * hd:(g + 1) * hd, :] for g in range(grp)], axis=1)
    nt = s_len // tk

    def scores(t):
        off = pl.multiple_of(t * tk, tk)
        return _dot(k_ref[pl.ds(off, tk), :], qt)

    ones_rows = jnp.ones((ATTN_SUM_ROWS, tk), BF16)

    def weighted_values(t, slot):
        off = pl.multiple_of(t * tk, tk)
        vt = jnp.concatenate([vt_ref[:, pl.ds(off, tk)], ones_rows], axis=0)
        return _dot(vt, p_ref[slot])

    def step(t, cur, carry):
        m_old, a_prev, mx = carry
        s_next = scores(jnp.minimum(t + 1, nt - 1))
        s_ref[1 - cur] = s_next
        mx_next = jnp.max(s_next, axis=0, keepdims=True)
        acc_ref[...] = a_prev * acc_ref[...] + weighted_values(jnp.maximum(t - 1, 0), 1 - cur)
        m_new = jnp.maximum(m_old, mx)
        p_ref[cur] = jnp.exp2(s_ref[cur] - m_new).astype(BF16)
        return m_new, jnp.exp2(m_old - m_new), mx_next

    def body(j, carry):
        return step(2 * j + 1, 1, step(2 * j, 0, carry))

    acc_ref[...] = jnp.zeros(acc_ref.shape, F32)
    p_ref[1] = jnp.zeros(p_ref.shape[1:], BF16)
    s_first = scores(0)
    s_ref[0] = s_first
    init = (jnp.full((1, n), -jnp.inf, F32), jnp.ones((1, n), F32), jnp.max(s_first, axis=0, keepdims=True))
    _, a_last, _ = lax.fori_loop(0, nt // 2, body, init)
    acc = a_last * acc_ref[...] + weighted_values(nt - 1, 1)
    out = acc[0:hd, :] / acc[hd:hd + 1, :]
    for g in range(grp):
        o_ref[:, g * hd:(g + 1) * hd] = out[:, g * tq:(g + 1) * tq].T.astype(o_ref.dtype)


def _attention(qt, k, vt, hd, tq, tk):
    qw, s = qt.shape
    kvh = k.shape[1] // hd
    gw = qw // kvh
    return pl.pallas_call(
        functools.partial(_attn_kernel, tk),
        out_shape=jax.ShapeDtypeStruct((s, qw), BF16),
        grid=(kvh, s // tq),
        in_specs=[pl.BlockSpec((gw, tq), lambda h, i: (h, i)),
                  pl.BlockSpec((s, hd), lambda h, i: (0, h)),
                  pl.BlockSpec((hd, s), lambda h, i: (h, 0))],
        out_specs=pl.BlockSpec((tq, gw), lambda h, i: (i, h)),
        scratch_shapes=[pltpu.VMEM((hd + ATTN_SUM_ROWS, gw // hd * tq), F32),
                        pltpu.VMEM((2, tk, gw // hd * tq), F32),
                        pltpu.VMEM((2, tk, gw // hd * tq), BF16)],
        compiler_params=_cparams(("parallel", "parallel")),
        name="gqa_attention",
    )(qt, k, vt)


def _odd_tail_kernel(alpha, yc_ref, up_ref, u_ref, un_ref, x_ref, cw_ref, cb_ref, cg_ref, cbeta_ref,
                     wo_ref, lg_ref, lb_ref, o_ref, ext_ref, conv_ref):
    t = x_ref.shape[0]
    mix_w = u_ref.shape[1]
    ext = _halo_ext(up_ref, u_ref, un_ref)
    rows = ext.shape[0]
    ext_ref[0] = ext
    for r in range(1, SUBLANES):
        ext_ref[r] = pltpu.roll(ext, rows - r, axis=0)
    base = HALO - CONV_W // 2
    cb = cb_ref[...]
    groups = CONV_ROWS // SUBLANES
    for c in range(t // CONV_ROWS):
        acc = jnp.zeros((groups, SUBLANES, mix_w), F32)
        for j in range(CONV_W):
            r = (base + j) % SUBLANES
            start = c * CONV_ROWS + (base + j - r)
            xw = ext_ref[r, start:start + CONV_ROWS, :].reshape(groups, SUBLANES, mix_w)
            acc = acc + xw * cw_ref[j]
        conv_ref[c * CONV_ROWS:(c + 1) * CONV_ROWS, :] = acc.reshape(CONV_ROWS, mix_w) + cb
    conv = conv_ref[...]
    yd = _silu(_layer_norm(conv, cg_ref[...], cbeta_ref[...])).astype(BF16)
    y = _dot(yc_ref[...], _w(wo_ref[0:mix_w, :])) + _dot(yd, _w(wo_ref[mix_w:2 * mix_w, :]))
    o_ref[...] = _layer_norm(alpha * x_ref[...] + y, lg_ref[...], lb_ref[...])


def _odd_tail(y_c, u, x, conv_w, conv_b, conv_g, conv_beta, w_out, layer, ln_g, ln_b, alpha, tm):
    s, d_model = x.shape
    mix_w = u.shape[1]
    hb = tm // HALO
    nhalo = s // HALO
    row = pl.BlockSpec((tm, mix_w), lambda i: (i, 0))
    vec = lambda n: pl.BlockSpec((1, n), lambda i: (0, 0))
    return pl.pallas_call(
        functools.partial(_odd_tail_kernel, alpha),
        out_shape=jax.ShapeDtypeStruct((s, d_model), F32),
        grid=(s // tm,),
        in_specs=[row,
                  pl.BlockSpec((HALO, mix_w), lambda i: (jnp.maximum(i * hb - 1, 0), 0)),
                  row,
                  pl.BlockSpec((HALO, mix_w), lambda i: (jnp.minimum((i + 1) * hb, nhalo - 1), 0)),
                  pl.BlockSpec((tm, d_model), lambda i: (i, 0)),
                  pl.BlockSpec(conv_w.shape, lambda i: (0, 0, 0)),
                  vec(mix_w), vec(mix_w), vec(mix_w),
                  _layer_block(w_out, layer),
                  vec(d_model), vec(d_model)],
        out_specs=pl.BlockSpec((tm, d_model), lambda i: (i, 0)),
        scratch_shapes=[pltpu.VMEM((SUBLANES, tm + 2 * HALO, mix_w), F32), pltpu.VMEM((tm, mix_w), F32)],
        compiler_params=_cparams(("parallel",)),
        name="odd_tail",
    )(y_c, u, u, u, x, conv_w, conv_b, conv_g, conv_beta, w_out, ln_g, ln_b)


def _mem_kv(mem, wkv):
    depth, d_model, n = wkv.shape
    m = mem.shape[0]
    return pl.pallas_call(
        _proj_kernel,
        out_shape=jax.ShapeDtypeStruct((depth, m, n), BF16),
        grid=(depth,),
        in_specs=[pl.BlockSpec((m, d_model), lambda l: (0, 0)),
                  pl.BlockSpec((None, d_model, n), lambda l: (l, 0, 0))],
        out_specs=pl.BlockSpec((None, m, n), lambda l: (l, 0, 0)),
        compiler_params=_cparams(("parallel",)),
        name="mem_kv",
    )(mem, wkv)


def _xattn_kernel(alpha, x_ref, wq_ref, k_ref, v_ref, wo_ref, lg_ref, lb_ref, o_ref):
    d_model = x_ref.shape[1]
    hd = d_model // XA_HEADS
    x = x_ref[...]
    q = (_dot(x.astype(BF16), _w(wq_ref[...])) * (hd ** -0.5)).astype(BF16)
    outs = []
    for h in range(XA_HEADS):
        sl = slice(h * hd, (h + 1) * hd)
        s = _dot_nt(q[:, sl], k_ref[:, sl])
        m = jnp.max(s, axis=-1, keepdims=True)
        p = jnp.exp(s - m)
        l = jnp.sum(p, axis=-1, keepdims=True)
        outs.append(_dot(p.astype(BF16), v_ref[:, sl]) / l)
    o = jnp.concatenate(outs, axis=-1).astype(BF16)
    y = _dot(o, _w(wo_ref[...]))
    o_ref[...] = _layer_norm(alpha * x + y, lg_ref[...], lb_ref[...])


def _xattn(x, wq, kv_bf, wo, layer, ln_g, ln_b, alpha, tm):
    s, d_model = x.shape
    vec = pl.BlockSpec((1, d_model), lambda i: (0, 0))
    return pl.pallas_call(
        functools.partial(_xattn_kernel, alpha),
        out_shape=jax.ShapeDtypeStruct((s, d_model), F32),
        grid=(s // tm,),
        in_specs=[pl.BlockSpec((tm, d_model), lambda i: (i, 0)),
                  _layer_block(wq, layer),
                  _layer_block(kv_bf, layer, d_model, 0), _layer_block(kv_bf, layer, d_model, 1),
                  _layer_block(wo, layer), vec, vec],
        out_specs=pl.BlockSpec((tm, d_model), lambda i: (i, 0)),
        compiler_params=_cparams(("parallel",)),
        name="mem_xattn",
    )(x, wq, kv_bf, kv_bf, wo, ln_g, ln_b)


def _ffn_kernel(ck, alpha, x_ref, wg_ref, wu_ref, wd_ref, lg_ref, lb_ref, o_ref):
    d_ff = wg_ref.shape[1]
    x = x_ref[...]
    xb = x.astype(BF16)
    y = jnp.zeros(x.shape, F32)
    for c in range(d_ff // ck):
        sl = slice(c * ck, (c + 1) * ck)
        hcn = _silu(_dot(xb, _w(wg_ref[:, sl]))) * _dot(xb, _w(wu_ref[:, sl]))
        y = y + _dot(hcn.astype(BF16), _w(wd_ref[sl, :]))
    o_ref[...] = _layer_norm(alpha * x + y, lg_ref[...], lb_ref[...])


def _ffn(x, w_gu, w_down, layer, ln_g, ln_b, alpha, tm, ck):
    s, d_model = x.shape
    d_ff = w_down.shape[1]
    assert d_ff % ck == 0, (d_ff, ck)
    vec = pl.BlockSpec((1, d_model), lambda i: (0, 0))
    return pl.pallas_call(
        functools.partial(_ffn_kernel, ck, alpha),
        out_shape=jax.ShapeDtypeStruct((s, d_model), F32),
        grid=(s // tm,),
        in_specs=[pl.BlockSpec((tm, d_model), lambda i: (i, 0)),
                  _layer_block(w_gu, layer, d_ff, 0), _layer_block(w_gu, layer, d_ff, 1),
                  _layer_block(w_down, layer), vec, vec],
        out_specs=pl.BlockSpec((tm, d_model), lambda i: (i, 0)),
        compiler_params=_cparams(("parallel",)),
        name="swiglu_ffn",
    )(x, w_gu, w_gu, w_down, ln_g, ln_b)


def _rope_tables(s, hd):
    rows = s // GRID_W
    row = np.repeat(np.arange(rows), GRID_W)
    col = np.tile(np.arange(GRID_W), rows)
    half = hd // 2
    freqs = ROPE_THETA ** (-np.arange(0, half, 2, dtype=np.float64) / half)

    def ang(p):
        a = p.astype(np.float64)[:, None] * freqs[None, :]
        return np.concatenate([a, a], axis=-1)

    angles = np.concatenate([ang(row), ang(col)], axis=-1)
    cos, sin = np.cos(angles), np.sin(angles)
    lo = (np.arange(hd) % (hd // 2)) < (hd // 4)
    as_f32 = lambda a: jnp.asarray(a.astype(np.float32))
    return as_f32(cos), as_f32(np.where(lo, -sin, 0.0)), as_f32(np.where(lo, 0.0, sin))


def kernel(x, mem, w_in_ab, hgrn_lb_logits, hgrn_norm_g, pool_w, pool_scale, w_out_ab, w_in_cd, q_norm_g, k_norm_g, conv_w, conv_b, conv_ln_g, conv_ln_b, w_out_cd, xa_wq, xa_wkv, xa_wo, ffn_w_gu, ffn_w_down, ln_g, ln_b):
    depth = xa_wq.shape[0]
    alpha = (2 * depth) ** 0.25
    bsz, s, d_model = x.shape
    mix_w = d_model // 2
    hd_c = mix_w // C_HEADS
    kv_w = C_KV_HEADS * hd_c
    tm = min(ROW_TILE, s)
    tq, tk, txa = min(ATTN_TQ, s), min(ATTN_TK, s), min(XA_ROW_TILE, s)
    assert s % tm == 0 and s % HG_BLOCK == 0 and s % GRID_W == 0 and tm % HALO == 0 and tm % CONV_ROWS == 0
    assert s % tq == 0 and s % txa == 0 and (s // tk) % 2 == 0 and s % tk == 0, (s, tq, tk, txa)
    assert d_model % (2 * A_HEADS * LANES) == 0 and hd_c % LANES == 0, d_model

    cum = jnp.cumsum(jax.nn.softmax(hgrn_lb_logits.astype(F32), axis=1), axis=1)
    lb = jnp.maximum(cum - cum[:, :1], 0.0)
    cos, sin_lo, sin_hi = _rope_tables(s, hd_c)
    row = lambda a: a.reshape(1, -1)

    outs = []
    for bi in range(bsz):
        xb = x[bi]
        kv = _mem_kv(mem[bi], xa_wkv)
        for l in range(depth):
            j = l // 2
            if l % 2 == 0:
                h = _proj(xb, w_in_ab, j, tm, F32, "even_in")
                o_f, o_b = _hgrn(h, row(lb[0, l]), row(lb[1, l]), mix_w)
                xb = _even_tail(o_f, o_b, h, xb, row(hgrn_norm_g[j]), pool_w, row(pool_scale[j]),
                                w_out_ab, j, row(ln_g[l, 0]), row(ln_b[l, 0]), alpha, tm)
            else:
                q, k, v, u = _odd_in(xb, w_in_cd, j, row(q_norm_g[j]), row(k_norm_g[j]),
                                     cos, sin_lo, sin_hi, mix_w, kv_w, tm)
                y_c = _attention(q, k, v, hd_c, tq, tk)
                taps = jnp.broadcast_to(conv_w[j][:, None, :], (conv_w.shape[1], SUBLANES, mix_w))
                xb = _odd_tail(y_c, u, xb, taps, row(conv_b[j]), row(conv_ln_g[j]), row(conv_ln_b[j]),
                               w_out_cd, j, row(ln_g[l, 0]), row(ln_b[l, 0]), alpha, tm)
            xb = _xattn(xb, xa_wq, kv, xa_wo, l, row(ln_g[l, 1]), row(ln_b[l, 1]), alpha, txa)
            xb = _ffn(xb, ffn_w_gu, ffn_w_down, l, row(ln_g[l, 2]), row(ln_b[l, 2]), alpha, tm, FFN_CHUNK)
        outs.append(xb)
    return jnp.stack(outs, axis=0)
```

```python
import functools
import math

import jax
import jax.numpy as jnp
import numpy as np
from jax import lax
from jax.experimental import pallas as pl
from jax.experimental.pallas import tpu as pltpu

F32 = jnp.float32
BF16 = jnp.bfloat16

A_HEADS = 4
POOL_WINDOWS = (2, 4, 8, 16)
C_HEADS = 4
C_KV_HEADS = 2
GRID_W = 64
ROPE_THETA = 10000.0
CONV_W = 31
XA_HEADS = 4
EPS = 1e-6
LOG2E = math.log2(math.e)

LANES = 128
SUBLANES = 8
VMEM_LIMIT = 56 * 1024 * 1024

ROW_TILE = 512
HG_BLOCK = 128
HG_SAFE_SPAN = 80.0
ATTN_TQ = 2048
ATTN_TK = 512
FFN_CHUNK = 256
XA_ROW_TILE = 1024
CONV_ROWS = 32
HALO = 16
ATTN_SUM_ROWS = 16


def _cparams(sem):
    return pltpu.CompilerParams(dimension_semantics=sem, vmem_limit_bytes=VMEM_LIMIT)


def _silu(x):
    return x * jax.nn.sigmoid(x)


def _layer_norm(y, g, b):
    mu = jnp.mean(y, axis=-1, keepdims=True)
    d = y - mu
    var = jnp.mean(d * d, axis=-1, keepdims=True)
    return d * lax.rsqrt(var + EPS) * g + b


def _dot(a, b):
    return jnp.dot(a, b, preferred_element_type=F32)


def _w(w):
    return w.astype(BF16)


def _dot_nt(a, b):
    return lax.dot_general(a, b, (((1,), (1,)), ((), ())), preferred_element_type=F32)


def _layer_block(stack, layer, cols=None, col_block=0):
    shape = tuple(stack.shape[1:])
    if cols is not None:
        shape = shape[:-1] + (cols,)
    index = (layer,) + (0,) * (len(shape) - 1) + (col_block,)
    return pl.BlockSpec((None,) + shape, lambda *_: index, pipeline_mode=pl.Buffered(1))


def _proj_kernel(x_ref, w_ref, o_ref):
    o_ref[...] = _dot(x_ref[...].astype(BF16), _w(w_ref[...])).astype(o_ref.dtype)


def _proj(x, w_stack, layer, tm, out_dtype, name):
    m, k = x.shape
    n = w_stack.shape[2]
    return pl.pallas_call(
        _proj_kernel,
        out_shape=jax.ShapeDtypeStruct((m, n), out_dtype),
        grid=(m // tm,),
        in_specs=[pl.BlockSpec((tm, k), lambda i: (i, 0)), _layer_block(w_stack, layer)],
        out_specs=pl.BlockSpec((tm, n), lambda i: (i, 0)),
        compiler_params=_cparams(("parallel",)),
        name=name,
    )(x, w_stack)


def _hgrn_direction(q_ref, v_ref, z_ref, lb, lv_ref, st_ref, o_ref, reverse):
    n = HG_BLOCK
    w = q_ref.shape[1]
    hd = w // A_HEADS
    z = z_ref[...]
    qs = _silu(q_ref[...])
    v = v_ref[...]
    t = jnp.exp(-jnp.abs(z))
    k = (1.0 - lb) * (jnp.where(z > 0, t, 1.0) / (1.0 + t))
    log_lb = jnp.log(lb)
    c = jnp.log1p(-lb) + (jnp.minimum(z, 0.0) - jnp.log(1.0 + t))
    g = jnp.maximum(log_lb, c) + jnp.log(1.0 + jnp.exp(-jnp.abs(log_lb - c)))

    rowv = lax.broadcasted_iota(jnp.int32, (n, 1), 0)
    tri = jnp.where(lv_ref[...] >= 0, 1.0, 0.0).astype(BF16)
    g1 = g.astype(BF16)
    r1 = g - g1.astype(F32)
    g2 = r1.astype(BF16)
    g3 = (r1 - g2.astype(F32)).astype(BF16)
    yield
    b = _dot(tri, g1) + _dot(tri, g2) + _dot(tri, g3)
    yield

    end = 0 if reverse else n - 1
    b_end = b[end:end + 1, :]
    qe = (qs * jnp.exp(b)).astype(BF16)
    kd = (k * jnp.exp(b_end - b)).astype(BF16)
    dec = jnp.exp(b_end)
    yield

    lv = lv_ref[...]
    heads = [slice(h * hd, (h + 1) * hd) for h in range(A_HEADS)]
    d_mid = b - b[n // 2:n // 2 + 1, :]

    def single_reference(_):
        ql = (qs * jnp.exp2(d_mid * LOG2E)).astype(BF16)
        kl = (k * jnp.exp2(d_mid * -LOG2E)).astype(BF16)
        interacts = lv >= 0
        return [jnp.where(interacts, _dot_nt(ql[:, sl], kl[:, sl]), 0.0) for sl in heads]

    def by_levels(_):
        ng = n // SUBLANES
        b3 = b.reshape(ng, SUBLANES, w)
        sub3 = lax.broadcasted_iota(jnp.int32, (ng, SUBLANES, 1), 1)
        qs_bf = qs.astype(BF16)
        k_bf = k.astype(BF16)
        lvl = [(qs_bf, k_bf, 0)]
        half = n // 2
        while half >= 1:
            two = 2 * half
            off = half if reverse else half - 1
            if half >= SUBLANES:
                parts = [jnp.broadcast_to(b[blk * two + off:blk * two + off + 1, :], (two, w))
                         for blk in range(n // two)]
                bref = jnp.concatenate(parts, axis=0) if len(parts) > 1 else parts[0]
            else:
                bref3 = jnp.broadcast_to(b3[:, off:off + 1, :], (ng, SUBLANES, w))
                for blk in range(1, SUBLANES // two):
                    r = blk * two + off
                    bref3 = jnp.where(sub3 >= blk * two,
                                      jnp.broadcast_to(b3[:, r:r + 1, :], (ng, SUBLANES, w)), bref3)
                bref = bref3.reshape(n, w)
            second = (rowv % two) >= half
            q_side = jnp.logical_not(second) if reverse else second
            e = jnp.exp2((b - bref) * jnp.where(q_side, LOG2E, -LOG2E)).astype(BF16)
            lvl.append((qs_bf * e, k_bf * e, half))
            half //= 2
        sms = [jnp.zeros((n, n), F32) for _ in heads]
        for ql, kl, level_id in lvl:
            mask = lv == level_id
            sms = [jnp.where(mask, _dot_nt(ql[:, sl], kl[:, sl]), sm) for sl, sm in zip(heads, sms)]
        return sms

    reach = jnp.max(jnp.abs(d_mid)) + jnp.log(jnp.maximum(jnp.max(jnp.abs(qs)), 1.0))
    sms = yield (reach, single_reference, by_levels)

    for h, (sl, sm) in enumerate(zip(heads, sms)):
        st = st_ref[h]
        vh = v[:, sl]
        o = _dot_nt(qe[:, sl], st.astype(BF16)) + _dot(sm.astype(BF16), vh.astype(BF16))
        o_ref[:, sl] = o
        st_ref[h] = st * dec[:, sl] + _dot(vh.T.astype(BF16), kd[:, sl])
        yield


_DONE = object()


def _hgrn_kernel(lbf_ref, lbb_ref, lvf_ref, lvb_ref, qvzf_ref, qvb_ref, zb_ref,
                 of_ref, ob_ref, stf_ref, stb_ref):
    @pl.when(pl.program_id(0) == 0)
    def _():
        stf_ref[...] = jnp.zeros_like(stf_ref)
        stb_ref[...] = jnp.zeros_like(stb_ref)

    w = of_ref.shape[1]
    qf_ref, vf_ref, zf_ref = (qvzf_ref.at[:, pl.ds(c * w, w)] for c in range(3))
    qb_ref, vb_ref = (qvb_ref.at[:, pl.ds(c * w, w)] for c in range(2))

    pending = [_hgrn_direction(qf_ref, vf_ref, zf_ref, lbf_ref[...], lvf_ref, stf_ref, of_ref, False),
               _hgrn_direction(qb_ref, vb_ref, zb_ref, lbb_ref[...], lvb_ref, stb_ref, ob_ref, True)]
    values = [next(d) for d in pending]
    while pending:
        if all(v is not None for v in values):
            reaches, single, levels = zip(*values)
            safe = functools.reduce(jnp.logical_and, [r < HG_SAFE_SPAN for r in reaches])
            decay = lax.cond(safe, lambda _: [f(None) for f in single], lambda _: [f(None) for f in levels], None)
            values = [d.send(sms) for d, sms in zip(pending, decay)]
        else:
            values = [next(d, _DONE) for d in pending]
        alive = [(d, v) for d, v in zip(pending, values) if v is not _DONE]
        pending, values = [d for d, _ in alive], [v for _, v in alive]


def _hgrn_level_tables(n):
    t, s = np.indices((n, n))
    x = t ^ s
    half = np.where(x > 0, 1 << (np.floor(np.log2(np.maximum(x, 1))).astype(np.int64)), 0)
    fwd = np.where(s <= t, half, -1).astype(np.int32)
    bwd = np.where(s >= t, half, -1).astype(np.int32)
    return jnp.asarray(fwd), jnp.asarray(bwd)


def _hgrn(h, lb_f, lb_b, mix_w):
    s = h.shape[0]
    n = HG_BLOCK
    nb = s // n
    hd = mix_w // A_HEADS
    bwd = lambda c: pl.BlockSpec((n, mix_w), lambda i: (nb - 1 - i, c))
    vec = pl.BlockSpec((1, mix_w), lambda i: (0, 0))
    lvs = pl.BlockSpec((n, n), lambda i: (0, 0))
    lv_f, lv_b = _hgrn_level_tables(n)
    return pl.pallas_call(
        _hgrn_kernel,
        out_shape=(jax.ShapeDtypeStruct((s, mix_w), F32), jax.ShapeDtypeStruct((s, mix_w), F32)),
        grid=(nb,),
        in_specs=[vec, vec, lvs, lvs,
                  pl.BlockSpec((n, 3 * mix_w), lambda i: (i, 0)),
                  pl.BlockSpec((n, 2 * mix_w), lambda i: (nb - 1 - i, 0)),
                  bwd(3)],
        out_specs=(pl.BlockSpec((n, mix_w), lambda i: (i, 0)),
                   pl.BlockSpec((n, mix_w), lambda i: (nb - 1 - i, 0))),
        scratch_shapes=[pltpu.VMEM((A_HEADS, hd, hd), F32), pltpu.VMEM((A_HEADS, hd, hd), F32)],
        compiler_params=_cparams(("arbitrary",)),
        name="hgrn_scan",
    )(lb_f, lb_b, lv_f, lv_b, h, h, h)


def _halo_ext(prev_ref, cur_ref, next_ref):
    i = pl.program_id(0)
    last = pl.num_programs(0) - 1
    prev = jnp.where(i == 0, 0.0, prev_ref[...])
    nxt = jnp.where(i == last, 0.0, next_ref[...])
    return jnp.concatenate([prev, cur_ref[...], nxt], axis=0)


def _even_tail_kernel(seq_len, alpha, of_ref, ob_ref, og_ref, up_ref, u_ref, un_ref, x_ref,
                      ng_ref, pw_ref, ps_ref, wo_ref, lg_ref, lb_ref, o_ref):
    t = x_ref.shape[0]
    mix_w = of_ref.shape[1]
    hd = mix_w // A_HEADS
    o = of_ref[...] + ob_ref[...]
    gate = _silu(og_ref[...])
    ng = ng_ref[...]
    ya = []
    for h in range(A_HEADS):
        sl = slice(h * hd, (h + 1) * hd)
        oh = o[:, sl]
        r = lax.rsqrt(jnp.mean(oh * oh, axis=-1, keepdims=True) + EPS)
        ya.append(oh * r * ng[:, sl] * gate[:, sl])
    ya = jnp.concatenate(ya, axis=-1).astype(BF16)

    ext = _halo_ext(up_ref, u_ref, un_ref)
    rows = ext.shape[0]
    gw = mix_w // len(POOL_WINDOWS)
    tpos = pl.program_id(0) * t + lax.broadcasted_iota(jnp.int32, (t, 1), 0)
    ps = ps_ref[...]
    yb = []
    for gi, win in enumerate(POOL_WINDOWS):
        sl = slice(gi * gw, (gi + 1) * gw)
        e = ext[:, sl]
        acc = e + pltpu.roll(e, 1, axis=0)
        span = 2
        while span < win:
            sh = span // 2
            acc = pltpu.roll(acc, rows - sh, axis=0) + pltpu.roll(acc, sh, axis=0)
            span *= 2
        wsum = acc[HALO:HALO + t, :]
        lo = jnp.maximum(tpos - win // 2, 0)
        hi = jnp.minimum(tpos - win // 2 + win - 1, seq_len - 1)
        cnt = (hi - lo + 1).astype(F32)
        d = wsum / cnt - e[HALO:HALO + t, :]
        yb.append(_dot(d.astype(BF16), _w(pw_ref[gi])) * ps[:, sl])
    yb = jnp.concatenate(yb, axis=-1).astype(BF16)

    y = _dot(ya, _w(wo_ref[0:mix_w, :])) + _dot(yb, _w(wo_ref[mix_w:2 * mix_w, :]))
    o_ref[...] = _layer_norm(alpha * x_ref[...] + y, lg_ref[...], lb_ref[...])


def _even_tail(o_f, o_b, h, x, norm_g, pool_w, pool_scale, w_out, layer, ln_g, ln_b, alpha, tm):
    s, d_model = x.shape
    mix_w = o_f.shape[1]
    hb = tm // HALO
    nhalo = s // HALO
    row = lambda c: pl.BlockSpec((tm, mix_w), lambda i: (i, c))
    vec = lambda n: pl.BlockSpec((1, n), lambda i: (0, 0))
    return pl.pallas_call(
        functools.partial(_even_tail_kernel, s, alpha),
        out_shape=jax.ShapeDtypeStruct((s, d_model), F32),
        grid=(s // tm,),
        in_specs=[row(0), row(0), row(4),
                  pl.BlockSpec((HALO, mix_w), lambda i: (jnp.maximum(i * hb - 1, 0), 5)),
                  row(5),
                  pl.BlockSpec((HALO, mix_w), lambda i: (jnp.minimum((i + 1) * hb, nhalo - 1), 5)),
                  pl.BlockSpec((tm, d_model), lambda i: (i, 0)),
                  vec(mix_w),
                  _layer_block(pool_w, layer),
                  vec(mix_w),
                  _layer_block(w_out, layer),
                  vec(d_model), vec(d_model)],
        out_specs=pl.BlockSpec((tm, d_model), lambda i: (i, 0)),
        compiler_params=_cparams(("parallel",)),
        name="even_tail",
    )(o_f, o_b, h, h, h, h, x, norm_g, pool_w, pool_scale, w_out, ln_g, ln_b)


def _rope(x, cos, sin_lo, sin_hi):
    n = x.shape[-1]
    r = n // 4
    return x * cos + pltpu.roll(x, n - r, axis=1) * sin_lo + pltpu.roll(x, r, axis=1) * sin_hi


def _odd_in_kernel(q_scale, x_ref, w_ref, qg_ref, kg_ref, cos_ref, sl_ref, sh_ref,
                   q_ref, k_ref, v_ref, u_ref):
    hd = cos_ref.shape[1]
    qw = q_ref.shape[0]
    kw = k_ref.shape[1]
    mix_w = u_ref.shape[1]
    h = _dot(x_ref[...].astype(BF16), _w(w_ref[...]))
    cos, s_lo, s_hi = cos_ref[...], sl_ref[...], sh_ref[...]

    def norm_rope(a, g):
        r = lax.rsqrt(jnp.mean(a * a, axis=-1, keepdims=True) + EPS)
        return _rope(a * r * g, cos, s_lo, s_hi)

    qg = qg_ref[...]
    kg = kg_ref[...]
    for i in range(qw // hd):
        q_ref[i * hd:(i + 1) * hd, :] = (norm_rope(h[:, i * hd:(i + 1) * hd], qg) * q_scale).T.astype(BF16)
    for i in range(kw // hd):
        k_ref[:, i * hd:(i + 1) * hd] = norm_rope(h[:, qw + i * hd:qw + (i + 1) * hd], kg).astype(BF16)
        v_ref[i * hd:(i + 1) * hd, :] = h[:, qw + kw + i * hd:qw + kw + (i + 1) * hd].T.astype(BF16)
    a0 = qw + 2 * kw
    u_ref[...] = h[:, a0:a0 + mix_w] * jax.nn.sigmoid(h[:, a0 + mix_w:a0 + 2 * mix_w])


def _odd_in(x, w_in, layer, q_g, k_g, cos, sin_lo, sin_hi, mix_w, kv_w, tm):
    s, d_model = x.shape
    hd = cos.shape[1]
    q_scale = hd ** -0.5 * LOG2E
    vec = pl.BlockSpec((1, hd), lambda i: (0, 0))
    tab = pl.BlockSpec((tm, hd), lambda i: (i, 0))
    return pl.pallas_call(
        functools.partial(_odd_in_kernel, q_scale),
        out_shape=(jax.ShapeDtypeStruct((mix_w, s), BF16), jax.ShapeDtypeStruct((s, kv_w), BF16),
                   jax.ShapeDtypeStruct((kv_w, s), BF16), jax.ShapeDtypeStruct((s, mix_w), F32)),
        grid=(s // tm,),
        in_specs=[pl.BlockSpec((tm, d_model), lambda i: (i, 0)),
                  _layer_block(w_in, layer),
                  vec, vec, tab, tab, tab],
        out_specs=(pl.BlockSpec((mix_w, tm), lambda i: (0, i)), pl.BlockSpec((tm, kv_w), lambda i: (i, 0)),
                   pl.BlockSpec((kv_w, tm), lambda i: (0, i)), pl.BlockSpec((tm, mix_w), lambda i: (i, 0))),
        compiler_params=_cparams(("parallel",)),
        name="odd_in",
    )(x, w_in, q_g, k_g, cos, sin_lo, sin_hi)


def _attn_kernel(tk, qt_ref, k_ref, vt_ref, o_ref, acc_ref, s_ref, p_ref):
    hd = k_ref.shape[1]
    grp = qt_ref.shape[0] // hd
    tq = qt_ref.shape[1]
    s_len = k_ref.shape[0]
    n = grp * tq
    qt = jnp.concatenate([qt_ref[g * hd:(g + 1) * hd, :] for g in range(grp)], axis=1)
    nt = s_len // tk

    def scores(t):
        off = pl.multiple_of(t * tk, tk)
        return _dot(k_ref[pl.ds(off, tk), :], qt)

    ones_rows = jnp.ones((ATTN_SUM_ROWS, tk), BF16)

    def weighted_values(t, slot):
        off = pl.multiple_of(t * tk, tk)
        vt = jnp.concatenate([vt_ref[:, pl.ds(off, tk)], ones_rows], axis=0)
        return _dot(vt, p_ref[slot])

    def step(t, cur, carry):
        m_old, a_prev, mx = carry
        s_next = scores(jnp.minimum(t + 1, nt - 1))
        s_ref[1 - cur] = s_next
        mx_next = jnp.max(s_next, axis=0, keepdims=True)
        acc_ref[...] = a_prev * acc_ref[...] + weighted_values(jnp.maximum(t - 1, 0), 1 - cur)
        m_new = jnp.maximum(m_old, mx)
        p_ref[cur] = jnp.exp2(s_ref[cur] - m_new).astype(BF16)
        return m_new, jnp.exp2(m_old - m_new), mx_next

    def body(j, carry):
        return step(2 * j + 1, 1, step(2 * j, 0, carry))

    acc_ref[...] = jnp.zeros(acc_ref.shape, F32)
    p_ref[1] = jnp.zeros(p_ref.shape[1:], BF16)
    s_first = scores(0)
    s_ref[0] = s_first
    init = (jnp.full((1, n), -jnp.inf, F32), jnp.ones((1, n), F32), jnp.max(s_first, axis=0, keepdims=True))
    _, a_last, _ = lax.fori_loop(0, nt // 2, body, init)
    acc = a_last * acc_ref[...] + weighted_values(nt - 1, 1)
    out = acc[0:hd, :] / acc[hd:hd + 1, :]
    for g in range(grp):
        o_ref[:, g * hd:(g + 1) * hd] = out[:, g * tq:(g + 1) * tq].T.astype(o_ref.dtype)


def _attention(qt, k, vt, hd, tq, tk):
    qw, s = qt.shape
    kvh = k.shape[1] // hd
    gw = qw // kvh
    return pl.pallas_call(
        functools.partial(_attn_kernel, tk),
        out_shape=jax.ShapeDtypeStruct((s, qw), BF16),
        grid=(kvh, s // tq),
        in_specs=[pl.BlockSpec((gw, tq), lambda h, i: (h, i)),
                  pl.BlockSpec((s, hd), lambda h, i: (0, h)),
                  pl.BlockSpec((hd, s), lambda h, i: (h, 0))],
        out_specs=pl.BlockSpec((tq, gw), lambda h, i: (i, h)),
        scratch_shapes=[pltpu.VMEM((hd + ATTN_SUM_ROWS, gw // hd * tq), F32),
                        pltpu.VMEM((2, tk, gw // hd * tq), F32),
                        pltpu.VMEM((2, tk, gw // hd * tq), BF16)],
        compiler_params=_cparams(("parallel", "parallel")),
        name="gqa_attention",
    )(qt, k, vt)


def _odd_tail_kernel(alpha, yc_ref, up_ref, u_ref, un_ref, x_ref, cw_ref, cb_ref, cg_ref, cbeta_ref,
                     wo_ref, lg_ref, lb_ref, o_ref, ext_ref, conv_ref):
    t = x_ref.shape[0]
    mix_w = u_ref.shape[1]
    ext = _halo_ext(up_ref, u_ref, un_ref)
    rows = ext.shape[0]
    ext_ref[0] = ext
    for r in range(1, SUBLANES):
        ext_ref[r] = pltpu.roll(ext, rows - r, axis=0)
    base = HALO - CONV_W // 2
    cb = cb_ref[...]
    groups = CONV_ROWS // SUBLANES
    for c in range(t // CONV_ROWS):
        acc = jnp.zeros((groups, SUBLANES, mix_w), F32)
        for j in range(CONV_W):
            r = (base + j) % SUBLANES
            start = c * CONV_ROWS + (base + j - r)
            xw = ext_ref[r, start:start + CONV_ROWS, :].reshape(groups, SUBLANES, mix_w)
            acc = acc + xw * cw_ref[j]
        conv_ref[c * CONV_ROWS:(c + 1) * CONV_ROWS, :] = acc.reshape(CONV_ROWS, mix_w) + cb
    conv = conv_ref[...]
    yd = _silu(_layer_norm(conv, cg_ref[...], cbeta_ref[...])).astype(BF16)
    y = _dot(yc_ref[...], _w(wo_ref[0:mix_w, :])) + _dot(yd, _w(wo_ref[mix_w:2 * mix_w, :]))
    o_ref[...] = _layer_norm(alpha * x_ref[...] + y, lg_ref[...], lb_ref[...])


def _odd_tail(y_c, u, x, conv_w, conv_b, conv_g, conv_beta, w_out, layer, ln_g, ln_b, alpha, tm):
    s, d_model = x.shape
    mix_w = u.shape[1]
    hb = tm // HALO
    nhalo = s // HALO
    row = pl.BlockSpec((tm, mix_w), lambda i: (i, 0))
    vec = lambda n: pl.BlockSpec((1, n), lambda i: (0, 0))
    return pl.pallas_call(
        functools.partial(_odd_tail_kernel, alpha),
        out_shape=jax.ShapeDtypeStruct((s, d_model), F32),
        grid=(s // tm,),
        in_specs=[row,
                  pl.BlockSpec((HALO, mix_w), lambda i: (jnp.maximum(i * hb - 1, 0), 0)),
                  row,
                  pl.BlockSpec((HALO, mix_w), lambda i: (jnp.minimum((i + 1) * hb, nhalo - 1), 0)),
                  pl.BlockSpec((tm, d_model), lambda i: (i, 0)),
                  pl.BlockSpec(conv_w.shape, lambda i: (0, 0, 0)),
                  vec(mix_w), vec(mix_w), vec(mix_w),
                  _layer_block(w_out, layer),
                  vec(d_model), vec(d_model)],
        out_specs=pl.BlockSpec((tm, d_model), lambda i: (i, 0)),
        scratch_shapes=[pltpu.VMEM((SUBLANES, tm + 2 * HALO, mix_w), F32), pltpu.VMEM((tm, mix_w), F32)],
        compiler_params=_cparams(("parallel",)),
        name="odd_tail",
    )(y_c, u, u, u, x, conv_w, conv_b, conv_g, conv_beta, w_out, ln_g, ln_b)


def _mem_kv(mem, wkv):
    depth, d_model, n = wkv.shape
    m = mem.shape[0]
    return pl.pallas_call(
        _proj_kernel,
        out_shape=jax.ShapeDtypeStruct((depth, m, n), BF16),
        grid=(depth,),
        in_specs=[pl.BlockSpec((m, d_model), lambda l: (0, 0)),
                  pl.BlockSpec((None, d_model, n), lambda l: (l, 0, 0))],
        out_specs=pl.BlockSpec((None, m, n), lambda l: (l, 0, 0)),
        compiler_params=_cparams(("parallel",)),
        name="mem_kv",
    )(mem, wkv)


def _xattn_kernel(alpha, x_ref, wq_ref, k_ref, v_ref, wo_ref, lg_ref, lb_ref, o_ref):
    d_model = x_ref.shape[1]
    hd = d_model // XA_HEADS
    x = x_ref[...]
    q = (_dot(x.astype(BF16), _w(wq_ref[...])) * (hd ** -0.5)).astype(BF16)
    outs = []
    for h in range(XA_HEADS):
        sl = slice(h * hd, (h + 1) * hd)
        s = _dot_nt(q[:, sl], k_ref[:, sl])
        m = jnp.max(s, axis=-1, keepdims=True)
        p = jnp.exp(s - m)
        l = jnp.sum(p, axis=-1, keepdims=True)
        outs.append(_dot(p.astype(BF16), v_ref[:, sl]) / l)
    o = jnp.concatenate(outs, axis=-1).astype(BF16)
    y = _dot(o, _w(wo_ref[...]))
    o_ref[...] = _layer_norm(alpha * x + y, lg_ref[...], lb_ref[...])


def _xattn(x, wq, kv_bf, wo, layer, ln_g, ln_b, alpha, tm):
    s, d_model = x.shape
    vec = pl.BlockSpec((1, d_model), lambda i: (0, 0))
    return pl.pallas_call(
        functools.partial(_xattn_kernel, alpha),
        out_shape=jax.ShapeDtypeStruct((s, d_model), F32),
        grid=(s // tm,),
        in_specs=[pl.BlockSpec((tm, d_model), lambda i: (i, 0)),
                  _layer_block(wq, layer),
                  _layer_block(kv_bf, layer, d_model, 0), _layer_block(kv_bf, layer, d_model, 1),
                  _layer_block(wo, layer), vec, vec],
        out_specs=pl.BlockSpec((tm, d_model), lambda i: (i, 0)),
        compiler_params=_cparams(("parallel",)),
        name="mem_xattn",
    )(x, wq, kv_bf, kv_bf, wo, ln_g, ln_b)


def _ffn_kernel(ck, alpha, x_ref, wg_ref, wu_ref, wd_ref, lg_ref, lb_ref, o_ref):
    d_ff = wg_ref.shape[1]
    x = x_ref[...]
    xb = x.astype(BF16)
    y = jnp.zeros(x.shape, F32)
    for c in range(d_ff // ck):
        sl = slice(c * ck, (c + 1) * ck)
        hcn = _silu(_dot(xb, _w(wg_ref[:, sl]))) * _dot(xb, _w(wu_ref[:, sl]))
        y = y + _dot(hcn.astype(BF16), _w(wd_ref[sl, :]))
    o_ref[...] = _layer_norm(alpha * x + y, lg_ref[...], lb_ref[...])


def _ffn(x, w_gu, w_down, layer, ln_g, ln_b, alpha, tm, ck):
    s, d_model = x.shape
    d_ff = w_down.shape[1]
    assert d_ff % ck == 0, (d_ff, ck)
    vec = pl.BlockSpec((1, d_model), lambda i: (0, 0))
    return pl.pallas_call(
        functools.partial(_ffn_kernel, ck, alpha),
        out_shape=jax.ShapeDtypeStruct((s, d_model), F32),
        grid=(s // tm,),
        in_specs=[pl.BlockSpec((tm, d_model), lambda i: (i, 0)),
                  _layer_block(w_gu, layer, d_ff, 0), _layer_block(w_gu, layer, d_ff, 1),
                  _layer_block(w_down, layer), vec, vec],
        out_specs=pl.BlockSpec((tm, d_model), lambda i: (i, 0)),
        compiler_params=_cparams(("parallel",)),
        name="swiglu_ffn",
    )(x, w_gu, w_gu, w_down, ln_g, ln_b)


def _rope_tables(s, hd):
    rows = s // GRID_W
    row = np.repeat(np.arange(rows), GRID_W)
    col = np.tile(np.arange(GRID_W), rows)
    half = hd // 2
    freqs = ROPE_THETA ** (-np.arange(0, half, 2, dtype=np.float64) / half)

    def ang(p):
        a = p.astype(np.float64)[:, None] * freqs[None, :]
        return np.concatenate([a, a], axis=-1)

    angles = np.concatenate([ang(row), ang(col)], axis=-1)
    cos, sin = np.cos(angles), np.sin(angles)
    lo = (np.arange(hd) % (hd // 2)) < (hd // 4)
    as_f32 = lambda a: jnp.asarray(a.astype(np.float32))
    return as_f32(cos), as_f32(np.where(lo, -sin, 0.0)), as_f32(np.where(lo, 0.0, sin))


def kernel(x, mem, w_in_ab, hgrn_lb_logits, hgrn_norm_g, pool_w, pool_scale, w_out_ab, w_in_cd, q_norm_g, k_norm_g, conv_w, conv_b, conv_ln_g, conv_ln_b, w_out_cd, xa_wq, xa_wkv, xa_wo, ffn_w_gu, ffn_w_down, ln_g, ln_b):
    depth = xa_wq.shape[0]
    alpha = (2 * depth) ** 0.25
    bsz, s, d_model = x.shape
    mix_w = d_model // 2
    hd_c = mix_w // C_HEADS
    kv_w = C_KV_HEADS * hd_c
    tm = min(ROW_TILE, s)
    tq, tk, txa = min(ATTN_TQ, s), min(ATTN_TK, s), min(XA_ROW_TILE, s)
    assert s % tm == 0 and s % HG_BLOCK == 0 and s % GRID_W == 0 and tm % HALO == 0 and tm % CONV_ROWS == 0
    assert s % tq == 0 and s % txa == 0 and (s // tk) % 2 == 0 and s % tk == 0, (s, tq, tk, txa)
    assert d_model % (2 * A_HEADS * LANES) == 0 and hd_c % LANES == 0, d_model

    cum = jnp.cumsum(jax.nn.softmax(hgrn_lb_logits.astype(F32), axis=1), axis=1)
    lb = jnp.maximum(cum - cum[:, :1], 0.0)
    cos, sin_lo, sin_hi = _rope_tables(s, hd_c)
    row = lambda a: a.reshape(1, -1)

    outs = []
    for bi in range(bsz):
        xb = x[bi]
        kv = _mem_kv(mem[bi], xa_wkv)
        for l in range(depth):
            j = l // 2
            if l % 2 == 0:
                h = _proj(xb, w_in_ab, j, tm, F32, "even_in")
                o_f, o_b = _hgrn(h, row(lb[0, l]), row(lb[1, l]), mix_w)
                xb = _even_tail(o_f, o_b, h, xb, row(hgrn_norm_g[j]), pool_w, row(pool_scale[j]),
                                w_out_ab, j, row(ln_g[l, 0]), row(ln_b[l, 0]), alpha, tm)
            else:
                q, k, v, u = _odd_in(xb, w_in_cd, j, row(q_norm_g[j]), row(k_norm_g[j]),
                                     cos, sin_lo, sin_hi, mix_w, kv_w, tm)
                y_c = _attention(q, k, v, hd_c, tq, tk)
                taps = jnp.broadcast_to(conv_w[j][:, None, :], (conv_w.shape[1], SUBLANES, mix_w))
                xb = _odd_tail(y_c, u, xb, taps, row(conv_b[j]), row(conv_ln_g[j]), row(conv_ln_b[j]),
                               w_out_cd, j, row(ln_g[l, 0]), row(ln_b[l, 0]), alpha, tm)
            xb = _xattn(xb, xa_wq, kv, xa_wo, l, row(ln_g[l, 1]), row(ln_b[l, 1]), alpha, txa)
            xb = _ffn(xb, ffn_w_gu, ffn_w_down, l, row(ln_g[l, 2]), row(ln_b[l, 2]), alpha, tm, FFN_CHUNK)
        outs.append(xb)
    return jnp.stack(outs, axis=0)
```

```python
import functools
import math

import jax
import jax.numpy as jnp
import numpy as np
from jax import lax
from jax.experimental import pallas as pl
from jax.experimental.pallas import tpu as pltpu

F32 = jnp.float32
BF16 = jnp.bfloat16

A_HEADS = 4
POOL_WINDOWS = (2, 4, 8, 16)
C_HEADS = 4
C_KV_HEADS = 2
GRID_W = 64
ROPE_THETA = 10000.0
CONV_W = 31
XA_HEADS = 4
EPS = 1e-6
LOG2E = math.log2(math.e)

LANES = 128
SUBLANES = 8
VMEM_LIMIT = 56 * 1024 * 1024

ROW_TILE = 512
HG_BLOCK = 128
HG_SAFE_SPAN = 80.0
ATTN_TQ = 2048
ATTN_TK = 512
FFN_CHUNK = 256
XA_ROW_TILE = 1024
CONV_ROWS = 32
HALO = 16
ATTN_SUM_ROWS = 16


def _cparams(sem):
    return pltpu.CompilerParams(dimension_semantics=sem, vmem_limit_bytes=VMEM_LIMIT)


def _silu(x):
    return x * jax.nn.sigmoid(x)


def _layer_norm(y, g, b):
    mu = jnp.mean(y, axis=-1, keepdims=True)
    d = y - mu
    var = jnp.mean(d * d, axis=-1, keepdims=True)
    return d * lax.rsqrt(var + EPS) * g + b


def _dot(a, b):
    return jnp.dot(a, b, preferred_element_type=F32)


def _w(w):
    return w.astype(BF16)


def _dot_nt(a, b):
    return lax.dot_general(a, b, (((1,), (1,)), ((), ())), preferred_element_type=F32)


def _layer_block(stack, layer, cols=None, col_block=0):
    shape = tuple(stack.shape[1:])
    if cols is not None:
        shape = shape[:-1] + (cols,)
    index = (layer,) + (0,) * (len(shape) - 1) + (col_block,)
    return pl.BlockSpec((None,) + shape, lambda *_: index, pipeline_mode=pl.Buffered(1))


def _proj_kernel(x_ref, w_ref, o_ref):
    o_ref[...] = _dot(x_ref[...].astype(BF16), _w(w_ref[...])).astype(o_ref.dtype)


def _proj(x, w_stack, layer, tm, out_dtype, name):
    m, k = x.shape
    n = w_stack.shape[2]
    return pl.pallas_call(
        _proj_kernel,
        out_shape=jax.ShapeDtypeStruct((m, n), out_dtype),
        grid=(m // tm,),
        in_specs=[pl.BlockSpec((tm, k), lambda i: (i, 0)), _layer_block(w_stack, layer)],
        out_specs=pl.BlockSpec((tm, n), lambda i: (i, 0)),
        compiler_params=_cparams(("parallel",)),
        name=name,
    )(x, w_stack)


def _hgrn_direction(q_ref, v_ref, z_ref, lb, lv_ref, st_ref, o_ref, reverse):
    n = HG_BLOCK
    w = q_ref.shape[1]
    hd = w // A_HEADS
    z = z_ref[...]
    qs = _silu(q_ref[...])
    v = v_ref[...]
    t = jnp.exp(-jnp.abs(z))
    k = (1.0 - lb) * (jnp.where(z > 0, t, 1.0) / (1.0 + t))
    log_lb = jnp.log(lb)
    c = jnp.log1p(-lb) + (jnp.minimum(z, 0.0) - jnp.log(1.0 + t))
    g = jnp.maximum(log_lb, c) + jnp.log(1.0 + jnp.exp(-jnp.abs(log_lb - c)))

    rowv = lax.broadcasted_iota(jnp.int32, (n, 1), 0)
    tri = jnp.where(lv_ref[...] >= 0, 1.0, 0.0).astype(BF16)
    g1 = g.astype(BF16)
    r1 = g - g1.astype(F32)
    g2 = r1.astype(BF16)
    g3 = (r1 - g2.astype(F32)).astype(BF16)
    yield
    b = _dot(tri, g1) + _dot(tri, g2) + _dot(tri, g3)
    yield

    end = 0 if reverse else n - 1
    b_end = b[end:end + 1, :]
    qe = (qs * jnp.exp(b)).astype(BF16)
    kd = (k * jnp.exp(b_end - b)).astype(BF16)
    dec = jnp.exp(b_end)
    yield

    lv = lv_ref[...]
    heads = [slice(h * hd, (h + 1) * hd) for h in range(A_HEADS)]
    d_mid = b - b[n // 2:n // 2 + 1, :]

    def single_reference(_):
        ql = (qs * jnp.exp2(d_mid * LOG2E)).astype(BF16)
        kl = (k * jnp.exp2(d_mid * -LOG2E)).astype(BF16)
        interacts = lv >= 0
        return [jnp.where(interacts, _dot_nt(ql[:, sl], kl[:, sl]), 0.0) for sl in heads]

    def by_levels(_):
        ng = n // SUBLANES
        b3 = b.reshape(ng, SUBLANES, w)
        sub3 = lax.broadcasted_iota(jnp.int32, (ng, SUBLANES, 1), 1)
        qs_bf = qs.astype(BF16)
        k_bf = k.astype(BF16)
        lvl = [(qs_bf, k_bf, 0)]
        half = n // 2
        while half >= 1:
            two = 2 * half
            off = half if reverse else half - 1
            if half >= SUBLANES:
                parts = [jnp.broadcast_to(b[blk * two + off:blk * two + off + 1, :], (two, w))
                         for blk in range(n // two)]
                bref = jnp.concatenate(parts, axis=0) if len(parts) > 1 else parts[0]
            else:
                bref3 = jnp.broadcast_to(b3[:, off:off + 1, :], (ng, SUBLANES, w))
                for blk in range(1, SUBLANES // two):
                    r = blk * two + off
                    bref3 = jnp.where(sub3 >= blk * two,
                                      jnp.broadcast_to(b3[:, r:r + 1, :], (ng, SUBLANES, w)), bref3)
                bref = bref3.reshape(n, w)
            second = (rowv % two) >= half
            q_side = jnp.logical_not(second) if reverse else second
            e = jnp.exp2((b - bref) * jnp.where(q_side, LOG2E, -LOG2E)).astype(BF16)
            lvl.append((qs_bf * e, k_bf * e, half))
            half //= 2
        sms = [jnp.zeros((n, n), F32) for _ in heads]
        for ql, kl, level_id in lvl:
            mask = lv == level_id
            sms = [jnp.where(mask, _dot_nt(ql[:, sl], kl[:, sl]), sm) for sl, sm in zip(heads, sms)]
        return sms

    reach = jnp.max(jnp.abs(d_mid)) + jnp.log(jnp.maximum(jnp.max(jnp.abs(qs)), 1.0))
    sms = yield (reach, single_reference, by_levels)

    for h, (sl, sm) in enumerate(zip(heads, sms)):
        st = st_ref[h]
        vh = v[:, sl]
        o = _dot_nt(qe[:, sl], st.astype(BF16)) + _dot(sm.astype(BF16), vh.astype(BF16))
        o_ref[:, sl] = o
        st_ref[h] = st * dec[:, sl] + _dot(vh.T.astype(BF16), kd[:, sl])
        yield


_DONE = object()


def _hgrn_kernel(lbf_ref, lbb_ref, lvf_ref, lvb_ref, qvzf_ref, qvb_ref, zb_ref,
                 of_ref, ob_ref, stf_ref, stb_ref):
    @pl.when(pl.program_id(0) == 0)
    def _():
        stf_ref[...] = jnp.zeros_like(stf_ref)
        stb_ref[...] = jnp.zeros_like(stb_ref)

    w = of_ref.shape[1]
    qf_ref, vf_ref, zf_ref = (qvzf_ref.at[:, pl.ds(c * w, w)] for c in range(3))
    qb_ref, vb_ref = (qvb_ref.at[:, pl.ds(c * w, w)] for c in range(2))

    pending = [_hgrn_direction(qf_ref, vf_ref, zf_ref, lbf_ref[...], lvf_ref, stf_ref, of_ref, False),
               _hgrn_direction(qb_ref, vb_ref, zb_ref, lbb_ref[...], lvb_ref, stb_ref, ob_ref, True)]
    values = [next(d) for d in pending]
    while pending:
        if all(v is not None for v in values):
            reaches, single, levels = zip(*values)
            safe = functools.reduce(jnp.logical_and, [r < HG_SAFE_SPAN for r in reaches])
            decay = lax.cond(safe, lambda _: [f(None) for f in single], lambda _: [f(None) for f in levels], None)
            values = [d.send(sms) for d, sms in zip(pending, decay)]
        else:
            values = [next(d, _DONE) for d in pending]
        alive = [(d, v) for d, v in zip(pending, values) if v is not _DONE]
        pending, values = [d for d, _ in alive], [v for _, v in alive]


def _hgrn_level_tables(n):
    t, s = np.indices((n, n))
    x = t ^ s
    half = np.where(x > 0, 1 << (np.floor(np.log2(np.maximum(x, 1))).astype(np.int64)), 0)
    fwd = np.where(s <= t, half, -1).astype(np.int32)
    bwd = np.where(s >= t, half, -1).astype(np.int32)
    return jnp.asarray(fwd), jnp.asarray(bwd)


def _hgrn(h, lb_f, lb_b, mix_w):
    s = h.shape[0]
    n = HG_BLOCK
    nb = s // n
    hd = mix_w // A_HEADS
    bwd = lambda c: pl.BlockSpec((n, mix_w), lambda i: (nb - 1 - i, c))
    vec = pl.BlockSpec((1, mix_w), lambda i: (0, 0))
    lvs = pl.BlockSpec((n, n), lambda i: (0, 0))
    lv_f, lv_b = _hgrn_level_tables(n)
    return pl.pallas_call(
        _hgrn_kernel,
        out_shape=(jax.ShapeDtypeStruct((s, mix_w), F32), jax.ShapeDtypeStruct((s, mix_w), F32)),
        grid=(nb,),
        in_specs=[vec, vec, lvs, lvs,
                  pl.BlockSpec((n, 3 * mix_w), lambda i: (i, 0)),
                  pl.BlockSpec((n, 2 * mix_w), lambda i: (nb - 1 - i, 0)),
                  bwd(3)],
        out_specs=(pl.BlockSpec((n, mix_w), lambda i: (i, 0)),
                   pl.BlockSpec((n, mix_w), lambda i: (nb - 1 - i, 0))),
        scratch_shapes=[pltpu.VMEM((A_HEADS, hd, hd), F32), pltpu.VMEM((A_HEADS, hd, hd), F32)],
        compiler_params=_cparams(("arbitrary",)),
        name="hgrn_scan",
    )(lb_f, lb_b, lv_f, lv_b, h, h, h)


def _halo_ext(prev_ref, cur_ref, next_ref):
    i = pl.program_id(0)
    last = pl.num_programs(0) - 1
    prev = jnp.where(i == 0, 0.0, prev_ref[...])
    nxt = jnp.where(i == last, 0.0, next_ref[...])
    return jnp.concatenate([prev, cur_ref[...], nxt], axis=0)


def _even_tail_kernel(seq_len, alpha, of_ref, ob_ref, og_ref, up_ref, u_ref, un_ref, x_ref,
                      ng_ref, pw_ref, ps_ref, wo_ref, lg_ref, lb_ref, o_ref):
    t = x_ref.shape[0]
    mix_w = of_ref.shape[1]
    hd = mix_w // A_HEADS
    o = of_ref[...] + ob_ref[...]
    gate = _silu(og_ref[...])
    ng = ng_ref[...]
    ya = []
    for h in range(A_HEADS):
        sl = slice(h * hd, (h + 1) * hd)
        oh = o[:, sl]
        r = lax.rsqrt(jnp.mean(oh * oh, axis=-1, keepdims=True) + EPS)
        ya.append(oh * r * ng[:, sl] * gate[:, sl])
    ya = jnp.concatenate(ya, axis=-1).astype(BF16)

    ext = _halo_ext(up_ref, u_ref, un_ref)
    rows = ext.shape[0]
    gw = mix_w // len(POOL_WINDOWS)
    tpos = pl.program_id(0) * t + lax.broadcasted_iota(jnp.int32, (t, 1), 0)
    ps = ps_ref[...]
    yb = []
    for gi, win in enumerate(POOL_WINDOWS):
        sl = slice(gi * gw, (gi + 1) * gw)
        e = ext[:, sl]
        acc = e + pltpu.roll(e, 1, axis=0)
        span = 2
        while span < win:
            sh = span // 2
            acc = pltpu.roll(acc, rows - sh, axis=0) + pltpu.roll(acc, sh, axis=0)
            span *= 2
        wsum = acc[HALO:HALO + t, :]
        lo = jnp.maximum(tpos - win // 2, 0)
        hi = jnp.minimum(tpos - win // 2 + win - 1, seq_len - 1)
        cnt = (hi - lo + 1).astype(F32)
        d = wsum / cnt - e[HALO:HALO + t, :]
        yb.append(_dot(d.astype(BF16), _w(pw_ref[gi])) * ps[:, sl])
    yb = jnp.concatenate(yb, axis=-1).astype(BF16)

    y = _dot(ya, _w(wo_ref[0:mix_w, :])) + _dot(yb, _w(wo_ref[mix_w:2 * mix_w, :]))
    o_ref[...] = _layer_norm(alpha * x_ref[...] + y, lg_ref[...], lb_ref[...])


def _even_tail(o_f, o_b, h, x, norm_g, pool_w, pool_scale, w_out, layer, ln_g, ln_b, alpha, tm):
    s, d_model = x.shape
    mix_w = o_f.shape[1]
    hb = tm // HALO
    nhalo = s // HALO
    row = lambda c: pl.BlockSpec((tm, mix_w), lambda i: (i, c))
    vec = lambda n: pl.BlockSpec((1, n), lambda i: (0, 0))
    return pl.pallas_call(
        functools.partial(_even_tail_kernel, s, alpha),
        out_shape=jax.ShapeDtypeStruct((s, d_model), F32),
        grid=(s // tm,),
        in_specs=[row(0), row(0), row(4),
                  pl.BlockSpec((HALO, mix_w), lambda i: (jnp.maximum(i * hb - 1, 0), 5)),
                  row(5),
                  pl.BlockSpec((HALO, mix_w), lambda i: (jnp.minimum((i + 1) * hb, nhalo - 1), 5)),
                  pl.BlockSpec((tm, d_model), lambda i: (i, 0)),
                  vec(mix_w),
                  _layer_block(pool_w, layer),
                  vec(mix_w),
                  _layer_block(w_out, layer),
                  vec(d_model), vec(d_model)],
        out_specs=pl.BlockSpec((tm, d_model), lambda i: (i, 0)),
        compiler_params=_cparams(("parallel",)),
        name="even_tail",
    )(o_f, o_b, h, h, h, h, x, norm_g, pool_w, pool_scale, w_out, ln_g, ln_b)


def _rope(x, cos, sin_lo, sin_hi):
    n = x.shape[-1]
    r = n // 4
    return x * cos + pltpu.roll(x, n - r, axis=1) * sin_lo + pltpu.roll(x, r, axis=1) * sin_hi


def _odd_in_kernel(q_scale, x_ref, w_ref, qg_ref, kg_ref, cos_ref, sl_ref, sh_ref,
                   q_ref, k_ref, v_ref, u_ref):
    hd = cos_ref.shape[1]
    qw = q_ref.shape[0]
    kw = k_ref.shape[1]
    mix_w = u_ref.shape[1]
    h = _dot(x_ref[...].astype(BF16), _w(w_ref[...]))
    cos, s_lo, s_hi = cos_ref[...], sl_ref[...], sh_ref[...]

    def norm_rope(a, g):
        r = lax.rsqrt(jnp.mean(a * a, axis=-1, keepdims=True) + EPS)
        return _rope(a * r * g, cos, s_lo, s_hi)

    qg = qg_ref[...]
    kg = kg_ref[...]
    for i in range(qw // hd):
        q_ref[i * hd:(i + 1) * hd, :] = (norm_rope(h[:, i * hd:(i + 1) * hd], qg) * q_scale).T.astype(BF16)
    for i in range(kw // hd):
        k_ref[:, i * hd:(i + 1) * hd] = norm_rope(h[:, qw + i * hd:qw + (i + 1) * hd], kg).astype(BF16)
        v_ref[i * hd:(i + 1) * hd, :] = h[:, qw + kw + i * hd:qw + kw + (i + 1) * hd].T.astype(BF16)
    a0 = qw + 2 * kw
    u_ref[...] = h[:, a0:a0 + mix_w] * jax.nn.sigmoid(h[:, a0 + mix_w:a0 + 2 * mix_w])


def _odd_in(x, w_in, layer, q_g, k_g, cos, sin_lo, sin_hi, mix_w, kv_w, tm):
    s, d_model = x.shape
    hd = cos.shape[1]
    q_scale = hd ** -0.5 * LOG2E
    vec = pl.BlockSpec((1, hd), lambda i: (0, 0))
    tab = pl.BlockSpec((tm, hd), lambda i: (i, 0))
    return pl.pallas_call(
        functools.partial(_odd_in_kernel, q_scale),
        out_shape=(jax.ShapeDtypeStruct((mix_w, s), BF16), jax.ShapeDtypeStruct((s, kv_w), BF16),
                   jax.ShapeDtypeStruct((kv_w, s), BF16), jax.ShapeDtypeStruct((s, mix_w), F32)),
        grid=(s // tm,),
        in_specs=[pl.BlockSpec((tm, d_model), lambda i: (i, 0)),
                  _layer_block(w_in, layer),
                  vec, vec, tab, tab, tab],
        out_specs=(pl.BlockSpec((mix_w, tm), lambda i: (0, i)), pl.BlockSpec((tm, kv_w), lambda i: (i, 0)),
                   pl.BlockSpec((kv_w, tm), lambda i: (0, i)), pl.BlockSpec((tm, mix_w), lambda i: (i, 0))),
        compiler_params=_cparams(("parallel",)),
        name="odd_in",
    )(x, w_in, q_g, k_g, cos, sin_lo, sin_hi)


def _attn_kernel(tk, qt_ref, k_ref, vt_ref, o_ref, acc_ref, s_ref, p_ref):
    hd = k_ref.shape[1]
    grp = qt_ref.shape[0] // hd
    tq = qt_ref.shape[1]
    s_len = k_ref.shape[0]
    n = grp * tq
    qt = jnp.concatenate([qt_ref[g * hd:(g + 1) * hd, :] for g in range(grp)], axis=1)
    nt = s_len // tk

    def scores(t):
        off = pl.multiple_of(t * tk, tk)
        return _dot(k_ref[pl.ds(off, tk), :], qt)

    ones_rows = jnp.ones((ATTN_SUM_ROWS, tk), BF16)

    def weighted_values(t, slot):
        off = pl.multiple_of(t * tk, tk)
        vt = jnp.concatenate([vt_ref[:, pl.ds(off, tk)], ones_rows], axis=0)
        return _dot(vt, p_ref[slot])

    def step(t, cur, carry):
        m_old, a_prev, mx = carry
        s_next = scores(jnp.minimum(t + 1, nt - 1))
        s_ref[1 - cur] = s_next
        mx_next = jnp.max(s_next, axis=0, keepdims=True)
        acc_ref[...] = a_prev * acc_ref[...] + weighted_values(jnp.maximum(t - 1, 0), 1 - cur)
        m_new = jnp.maximum(m_old, mx)
        p_ref[cur] = jnp.exp2(s_ref[cur] - m_new).astype(BF16)
        return m_new, jnp.exp2(m_old - m_new), mx_next

    def body(j, carry):
        return step(2 * j + 1, 1, step(2 * j, 0, carry))

    acc_ref[...] = jnp.zeros(acc_ref.shape, F32)
    p_ref[1] = jnp.zeros(p_ref.shape[1:], BF16)
    s_first = scores(0)
    s_ref[0] = s_first
    init = (jnp.full((1, n), -jnp.inf, F32), jnp.ones((1, n), F32), jnp.max(s_first, axis=0, keepdims=True))
    _, a_last, _ = lax.fori_loop(0, nt // 2, body, init)
    acc = a_last * acc_ref[...] + weighted_values(nt - 1, 1)
    out = acc[0:hd, :] / acc[hd:hd + 1, :]
    for g in range(grp):
        o_ref[:, g * hd:(g + 1) * hd] = out[:, g * tq:(g + 1) * tq].T.astype(o_ref.dtype)


def _attention(qt, k, vt, hd, tq, tk):
    qw, s = qt.shape
    kvh = k.shape[1] // hd
    gw = qw // kvh
    return pl.pallas_call(
        functools.partial(_attn_kernel, tk),
        out_shape=jax.ShapeDtypeStruct((s, qw), BF16),
        grid=(kvh, s // tq),
        in_specs=[pl.BlockSpec((gw, tq), lambda h, i: (h, i)),
                  pl.BlockSpec((s, hd), lambda h, i: (0, h)),
                  pl.BlockSpec((hd, s), lambda h, i: (h, 0))],
        out_specs=pl.BlockSpec((tq, gw), lambda h, i: (i, h)),
        scratch_shapes=[pltpu.VMEM((hd + ATTN_SUM_ROWS, gw // hd * tq), F32),
                        pltpu.VMEM((2, tk, gw // hd * tq), F32),
                        pltpu.VMEM((2, tk, gw // hd * tq), BF16)],
        compiler_params=_cparams(("parallel", "parallel")),
        name="gqa_attention",
    )(qt, k, vt)


def _odd_tail_kernel(alpha, yc_ref, up_ref, u_ref, un_ref, x_ref, cw_ref, cb_ref, cg_ref, cbeta_ref,
                     wo_ref, lg_ref, lb_ref, o_ref, ext_ref, conv_ref):
    t = x_ref.shape[0]
    mix_w = u_ref.shape[1]
    ext = _halo_ext(up_ref, u_ref, un_ref)
    rows = ext.shape[0]
    ext_ref[0] = ext
    for r in range(1, SUBLANES):
        ext_ref[r] = pltpu.roll(ext, rows - r, axis=0)
    base = HALO - CONV_W // 2
    cb = cb_ref[...]
    groups = CONV_ROWS // SUBLANES
    for c in range(t // CONV_ROWS):
        acc = jnp.zeros((groups, SUBLANES, mix_w), F32)
        for j in range(CONV_W):
            r = (base + j) % SUBLANES
            start = c * CONV_ROWS + (base + j - r)
            xw = ext_ref[r, start:start + CONV_ROWS, :].reshape(groups, SUBLANES, mix_w)
            acc = acc + xw * cw_ref[j]
        conv_ref[c * CONV_ROWS:(c + 1) * CONV_ROWS, :] = acc.reshape(CONV_ROWS, mix_w) + cb
    conv = conv_ref[...]
    yd = _silu(_layer_norm(conv, cg_ref[...], cbeta_ref[...])).astype(BF16)
    y = _dot(yc_ref[...], _w(wo_ref[0:mix_w, :])) + _dot(yd, _w(wo_ref[mix_w:2 * mix_w, :]))
    o_ref[...] = _layer_norm(alpha * x_ref[...] + y, lg_ref[...], lb_ref[...])


def _odd_tail(y_c, u, x, conv_w, conv_b, conv_g, conv_beta, w_out, layer, ln_g, ln_b, alpha, tm):
    s, d_model = x.shape
    mix_w = u.shape[1]
    hb = tm // HALO
    nhalo = s // HALO
    row = pl.BlockSpec((tm, mix_w), lambda i: (i, 0))
    vec = lambda n: pl.BlockSpec((1, n), lambda i: (0, 0))
    return pl.pallas_call(
        functools.partial(_odd_tail_kernel, alpha),
        out_shape=jax.ShapeDtypeStruct((s, d_model), F32),
        grid=(s // tm,),
        in_specs=[row,
                  pl.BlockSpec((HALO, mix_w), lambda i: (jnp.maximum(i * hb - 1, 0), 0)),
                  row,
                  pl.BlockSpec((HALO, mix_w), lambda i: (jnp.minimum((i + 1) * hb, nhalo - 1), 0)),
                  pl.BlockSpec((tm, d_model), lambda i: (i, 0)),
                  pl.BlockSpec(conv_w.shape, lambda i: (0, 0, 0)),
                  vec(mix_w), vec(mix_w), vec(mix_w),
                  _layer_block(w_out, layer),
                  vec(d_model), vec(d_model)],
        out_specs=pl.BlockSpec((tm, d_model), lambda i: (i, 0)),
        scratch_shapes=[pltpu.VMEM((SUBLANES, tm + 2 * HALO, mix_w), F32), pltpu.VMEM((tm, mix_w), F32)],
        compiler_params=_cparams(("parallel",)),
        name="odd_tail",
    )(y_c, u, u, u, x, conv_w, conv_b, conv_g, conv_beta, w_out, ln_g, ln_b)


def _mem_kv(mem, wkv):
    depth, d_model, n = wkv.shape
    m = mem.shape[0]
    return pl.pallas_call(
        _proj_kernel,
        out_shape=jax.ShapeDtypeStruct((depth, m, n), BF16),
        grid=(depth,),
        in_specs=[pl.BlockSpec((m, d_model), lambda l: (0, 0)),
                  pl.BlockSpec((None, d_model, n), lambda l: (l, 0, 0))],
        out_specs=pl.BlockSpec((None, m, n), lambda l: (l, 0, 0)),
        compiler_params=_cparams(("parallel",)),
        name="mem_kv",
    )(mem, wkv)


def _xattn_kernel(alpha, x_ref, wq_ref, k_ref, v_ref, wo_ref, lg_ref, lb_ref, o_ref):
    d_model = x_ref.shape[1]
    hd = d_model // XA_HEADS
    x = x_ref[...]
    q = (_dot(x.astype(BF16), _w(wq_ref[...])) * (hd ** -0.5)).astype(BF16)
    outs = []
    for h in range(XA_HEADS):
        sl = slice(h * hd, (h + 1) * hd)
        s = _dot_nt(q[:, sl], k_ref[:, sl])
        m = jnp.max(s, axis=-1, keepdims=True)
        p = jnp.exp(s - m)
        l = jnp.sum(p, axis=-1, keepdims=True)
        outs.append(_dot(p.astype(BF16), v_ref[:, sl]) / l)
    o = jnp.concatenate(outs, axis=-1).astype(BF16)
    y = _dot(o, _w(wo_ref[...]))
    o_ref[...] = _layer_norm(alpha * x + y, lg_ref[...], lb_ref[...])


def _xattn(x, wq, kv_bf, wo, layer, ln_g, ln_b, alpha, tm):
    s, d_model = x.shape
    vec = pl.BlockSpec((1, d_model), lambda i: (0, 0))
    return pl.pallas_call(
        functools.partial(_xattn_kernel, alpha),
        out_shape=jax.ShapeDtypeStruct((s, d_model), F32),
        grid=(s // tm,),
        in_specs=[pl.BlockSpec((tm, d_model), lambda i: (i, 0)),
                  _layer_block(wq, layer),
                  _layer_block(kv_bf, layer, d_model, 0), _layer_block(kv_bf, layer, d_model, 1),
                  _layer_block(wo, layer), vec, vec],
        out_specs=pl.BlockSpec((tm, d_model), lambda i: (i, 0)),
        compiler_params=_cparams(("parallel",)),
        name="mem_xattn",
    )(x, wq, kv_bf, kv_bf, wo, ln_g, ln_b)


def _ffn_kernel(ck, alpha, x_ref, wg_ref, wu_ref, wd_ref, lg_ref, lb_ref, o_ref):
    d_ff = wg_ref.shape[1]
    x = x_ref[...]
    xb = x.astype(BF16)
    y = jnp.zeros(x.shape, F32)
    for c in range(d_ff // ck):
        sl = slice(c * ck, (c + 1) * ck)
        hcn = _silu(_dot(xb, _w(wg_ref[:, sl]))) * _dot(xb, _w(wu_ref[:, sl]))
        y = y + _dot(hcn.astype(BF16), _w(wd_ref[sl, :]))
    o_ref[...] = _layer_norm(alpha * x + y, lg_ref[...], lb_ref[...])


def _ffn(x, w_gu, w_down, layer, ln_g, ln_b, alpha, tm, ck):
    s, d_model = x.shape
    d_ff = w_down.shape[1]
    assert d_ff % ck == 0, (d_ff, ck)
    vec = pl.BlockSpec((1, d_model), lambda i: (0, 0))
    return pl.pallas_call(
        functools.partial(_ffn_kernel, ck, alpha),
        out_shape=jax.ShapeDtypeStruct((s, d_model), F32),
        grid=(s // tm,),
        in_specs=[pl.BlockSpec((tm, d_model), lambda i: (i, 0)),
                  _layer_block(w_gu, layer, d_ff, 0), _layer_block(w_gu, layer, d_ff, 1),
                  _layer_block(w_down, layer), vec, vec],
        out_specs=pl.BlockSpec((tm, d_model), lambda i: (i, 0)),
        compiler_params=_cparams(("parallel",)),
        name="swiglu_ffn",
    )(x, w_gu, w_gu, w_down, ln_g, ln_b)


def _rope_tables(s, hd):
    rows = s // GRID_W
    row = np.repeat(np.arange(rows), GRID_W)
    col = np.tile(np.arange(GRID_W), rows)
    half = hd // 2
    freqs = ROPE_THETA ** (-np.arange(0, half, 2, dtype=np.float64) / half)

    def ang(p):
        a = p.astype(np.float64)[:, None] * freqs[None, :]
        return np.concatenate([a, a], axis=-1)

    angles = np.concatenate([ang(row), ang(col)], axis=-1)
    cos, sin = np.cos(angles), np.sin(angles)
    lo = (np.arange(hd) % (hd // 2)) < (hd // 4)
    as_f32 = lambda a: jnp.asarray(a.astype(np.float32))
    return as_f32(cos), as_f32(np.where(lo, -sin, 0.0)), as_f32(np.where(lo, 0.0, sin))


def kernel(x, mem, w_in_ab, hgrn_lb_logits, hgrn_norm_g, pool_w, pool_scale, w_out_ab, w_in_cd, q_norm_g, k_norm_g, conv_w, conv_b, conv_ln_g, conv_ln_b, w_out_cd, xa_wq, xa_wkv, xa_wo, ffn_w_gu, ffn_w_down, ln_g, ln_b):
    depth = xa_wq.shape[0]
    alpha = (2 * depth) ** 0.25
    bsz, s, d_model = x.shape
    mix_w = d_model // 2
    hd_c = mix_w // C_HEADS
    kv_w = C_KV_HEADS * hd_c
    tm = min(ROW_TILE, s)
    tq, tk, txa = min(ATTN_TQ, s), min(ATTN_TK, s), min(XA_ROW_TILE, s)
    assert s % tm == 0 and s % HG_BLOCK == 0 and s % GRID_W == 0 and tm % HALO == 0 and tm % CONV_ROWS == 0
    assert s % tq == 0 and s % txa == 0 and (s // tk) % 2 == 0 and s % tk == 0, (s, tq, tk, txa)
    assert d_model % (2 * A_HEADS * LANES) == 0 and hd_c % LANES == 0, d_model

    cum = jnp.cumsum(jax.nn.softmax(hgrn_lb_logits.astype(F32), axis=1), axis=1)
    lb = jnp.maximum(cum - cum[:, :1], 0.0)
    cos, sin_lo, sin_hi = _rope_tables(s, hd_c)
    row = lambda a: a.reshape(1, -1)

    outs = []
    for bi in range(bsz):
        xb = x[bi]
        kv = _mem_kv(mem[bi], xa_wkv)
        for l in range(depth):
            j = l // 2
            if l % 2 == 0:
                h = _proj(xb, w_in_ab, j, tm, F32, "even_in")
                o_f, o_b = _hgrn(h, row(lb[0, l]), row(lb[1, l]), mix_w)
                xb = _even_tail(o_f, o_b, h, xb, row(hgrn_norm_g[j]), pool_w, row(pool_scale[j]),
                                w_out_ab, j, row(ln_g[l, 0]), row(ln_b[l, 0]), alpha, txa)
            else:
                q, k, v, u = _odd_in(xb, w_in_cd, j, row(q_norm_g[j]), row(k_norm_g[j]),
                                     cos, sin_lo, sin_hi, mix_w, kv_w, tm)
                y_c = _attention(q, k, v, hd_c, tq, tk)
                taps = jnp.broadcast_to(conv_w[j][:, None, :], (conv_w.shape[1], SUBLANES, mix_w))
                xb = _odd_tail(y_c, u, xb, taps, row(conv_b[j]), row(conv_ln_g[j]), row(conv_ln_b[j]),
                               w_out_cd, j, row(ln_g[l, 0]), row(ln_b[l, 0]), alpha, tm)
            xb = _xattn(xb, xa_wq, kv, xa_wo, l, row(ln_g[l, 1]), row(ln_b[l, 1]), alpha, txa)
            xb = _ffn(xb, ffn_w_gu, ffn_w_down, l, row(ln_g[l, 2]), row(ln_b[l, 2]), alpha, tm, FFN_CHUNK)
        outs.append(xb)
    return jnp.stack(outs, axis=0)
```
